```python
import math
import jax, jax.numpy as jnp
from jax import lax
import numpy as np

D_MODEL = 1024
BATCH = 2
SEQ = 8192
DEPTH = 2

N_BRANCHES = 4
HEAD_DIM = 64
N_HEADS = D_MODEL // (N_BRANCHES * HEAD_DIM)
W_MIX = N_HEADS * HEAD_DIM
DIFF_QK_DIM = HEAD_DIM // 2
FOX_FORGET_BIAS = 3.0
NSA_CMP_BLOCK = 32
NSA_CMP_STRIDE = 16
NSA_SLC_BLOCK = 64
NSA_TOPK = 16
NSA_WINDOW = 512
NSA_PHI_HIDDEN = 256
NSA_FORCE_SCORE = 1.0e4
GMLP_CHUNK = 128
Q_BLOCK = 128
D_FF = 2816
RMS_EPS = 1e-6
IN_SPLIT_SIZES = (W_MIX, W_MIX, W_MIX,
                  W_MIX, W_MIX, W_MIX, N_HEADS,
                  W_MIX, 6 * HEAD_DIM, 3 * N_HEADS,
                  2 * W_MIX,
                  N_BRANCHES * D_MODEL)
D_IN = sum(IN_SPLIT_SIZES)

kernel_name = "hybrid_diff_fox_nsa_gmlp_macaron"


def rms_norm(x, g=None):
    xf = x.astype(jnp.float32)
    y = xf * lax.rsqrt(jnp.mean(xf * xf, axis=-1, keepdims=True) + RMS_EPS)
    if g is not None:
        y = y * g.astype(jnp.float32)
    return y.astype(x.dtype)


def masked_softmax(s, mask):
    s = jnp.where(mask, s.astype(jnp.float32), -jnp.inf)
    m = jnp.max(s, axis=-1, keepdims=True)
    m = jnp.where(jnp.isfinite(m), m, 0.0)
    p = jnp.exp(s - m)
    return p / jnp.maximum(jnp.sum(p, axis=-1, keepdims=True), 1e-30)


def split_cols(z, sizes):
    idx = np.cumsum(np.array(sizes))[:-1].tolist()
    return jnp.split(z, idx, axis=-1)


def blocks_to_seq(o):
    nq, b, qb, h, d = o.shape
    return o.transpose(1, 0, 2, 3, 4).reshape(b, nq * qb, h, d)


def swiglu(h, w_in, w_out):
    a, b = jnp.split(h @ w_in, 2, axis=-1)
    return (jax.nn.silu(a) * b) @ w_out


def diff_attention(q, k, v, lam):
    T = q.shape[1]
    scale = q.shape[-1] ** -0.5
    kpos = jnp.arange(T)

    def block(i):
        q0 = i * Q_BLOCK
        qb = lax.dynamic_slice_in_dim(q, q0, Q_BLOCK, axis=1)
        s = jnp.einsum('bqhcd,bkhcd->bchqk', qb, k) * scale
        qpos = q0 + jnp.arange(Q_BLOCK)
        p = masked_softmax(s, kpos[None, :] <= qpos[:, None])
        w = p[:, 0] - lam * p[:, 1]
        return jnp.einsum('bhqk,bkhd->bqhd', w.astype(v.dtype), v)

    return blocks_to_seq(lax.map(block, jnp.arange(T // Q_BLOCK)))


def forgetting_attention(q, k, v, log_f):
    T = q.shape[1]
    scale = q.shape[-1] ** -0.5
    c = jnp.cumsum(log_f, axis=1).transpose(0, 2, 1)
    kpos = jnp.arange(T)

    def block(i):
        q0 = i * Q_BLOCK
        qb = lax.dynamic_slice_in_dim(q, q0, Q_BLOCK, axis=1)
        cq = lax.dynamic_slice_in_dim(c, q0, Q_BLOCK, axis=2)
        s = jnp.einsum('bqhd,bkhd->bhqk', qb, k).astype(jnp.float32) * scale
        s = s + cq[..., :, None] - c[..., None, :]
        qpos = q0 + jnp.arange(Q_BLOCK)
        p = masked_softmax(s, kpos[None, :] <= qpos[:, None])
        return jnp.einsum('bhqk,bkhd->bqhd', p.astype(v.dtype), v)

    return blocks_to_seq(lax.map(block, jnp.arange(T // Q_BLOCK)))


def nsa_compress(x, pe, w1, b1, w2, b2):
    B, T, D = x.shape
    xb = x.reshape(B, T // NSA_CMP_STRIDE, NSA_CMP_STRIDE, D)
    blocks = jnp.concatenate([xb[:, :-1], xb[:, 1:]], axis=2)
    zf = (blocks + pe).reshape(B, blocks.shape[1], NSA_CMP_BLOCK * D)
    return jax.nn.gelu(zf @ w1 + b1) @ w2 + b2


def nsa_attention(q, kc, vc, ks, vs, kw, vw, gates):
    B, T, H, D = q.shape
    nc = kc.shape[1]
    ns = T // NSA_SLC_BLOCK
    n_top = min(NSA_TOPK, ns)
    scale = D ** -0.5
    cidx = jnp.arange(nc)
    sidx = jnp.arange(ns)
    cmp_end = cidx * NSA_CMP_STRIDE + NSA_CMP_BLOCK - 1
    overlap = ((cidx[:, None] * NSA_CMP_STRIDE < (sidx[None, :] + 1) * NSA_SLC_BLOCK)
               & (cidx[:, None] * NSA_CMP_STRIDE + NSA_CMP_BLOCK > sidx[None, :] * NSA_SLC_BLOCK)
               ).astype(jnp.float32)
    ks_blocks = ks.reshape(B, ns, NSA_SLC_BLOCK, D)
    vs_blocks = vs.reshape(B, ns, NSA_SLC_BLOCK, D)
    kw_pad = jnp.pad(kw, ((0, 0), (NSA_WINDOW, 0), (0, 0)))
    vw_pad = jnp.pad(vw, ((0, 0), (NSA_WINDOW, 0), (0, 0)))
    tok_off = jnp.arange(NSA_SLC_BLOCK)
    win_off = jnp.arange(NSA_WINDOW + Q_BLOCK) - NSA_WINDOW
    gather = jax.vmap(lambda blocks, idx: blocks[idx])
    n_sel = n_top * NSA_SLC_BLOCK

    def block(i):
        q0 = i * Q_BLOCK
        qpos = q0 + jnp.arange(Q_BLOCK)
        qb = lax.dynamic_slice_in_dim(q, q0, Q_BLOCK, axis=1)
        gb = jax.nn.sigmoid(lax.dynamic_slice_in_dim(gates, q0, Q_BLOCK, axis=1).astype(jnp.float32))
        s_c = jnp.einsum('bqhd,bnd->bhqn', qb, kc) * scale
        p_c = masked_softmax(s_c, cmp_end[None, :] <= qpos[:, None])
        o_c = jnp.einsum('bhqn,bnd->bqhd', p_c.astype(vc.dtype), vc)
        imp = jnp.einsum('bhqn,nj->bqj', p_c, overlap)
        cur = qpos // NSA_SLC_BLOCK
        forced = ((sidx[None, :] == 0) | (sidx[None, :] == cur[:, None])
                  | (sidx[None, :] == cur[:, None] - 1))
        valid = sidx[None, :] * NSA_SLC_BLOCK <= qpos[:, None]
        imp = jnp.where(forced, NSA_FORCE_SCORE, jnp.where(valid, imp, -1.0))
        _, sel = lax.top_k(imp, n_top)
        kg = gather(ks_blocks, sel).reshape(B, Q_BLOCK, n_sel, D)
        vg = gather(vs_blocks, sel).reshape(B, Q_BLOCK, n_sel, D)
        tok = (sel[..., None] * NSA_SLC_BLOCK + tok_off).reshape(B, Q_BLOCK, n_sel)
        s_s = jnp.einsum('bqhd,bqmd->bhqm', qb, kg) * scale
        p_s = masked_softmax(s_s, (tok <= qpos[None, :, None])[:, None])
        o_s = jnp.einsum('bhqm,bqmd->bqhd', p_s.astype(vg.dtype), vg)
        kwb = lax.dynamic_slice_in_dim(kw_pad, q0, NSA_WINDOW + Q_BLOCK, axis=1)
        vwb = lax.dynamic_slice_in_dim(vw_pad, q0, NSA_WINDOW + Q_BLOCK, axis=1)
        kpos = q0 + win_off
        dist = qpos[:, None] - kpos[None, :]
        mask_w = (kpos[None, :] >= 0) & (dist >= 0) & (dist < NSA_WINDOW)
        s_w = jnp.einsum('bqhd,bkd->bhqk', qb, kwb) * scale
        p_w = masked_softmax(s_w, mask_w)
        o_w = jnp.einsum('bhqk,bkd->bqhd', p_w.astype(vwb.dtype), vwb)
        g = gb.astype(q.dtype)
        return g[..., 0:1] * o_c + g[..., 1:2] * o_s + g[..., 2:3] * o_w

    return blocks_to_seq(lax.map(block, jnp.arange(T // Q_BLOCK)))


def chunked_spatial_gating(uv, v_gain, w_s, b_s):
    B, T, _ = uv.shape
    u, v = jnp.split(jax.nn.gelu(uv), 2, axis=-1)
    v = rms_norm(v.reshape(B, T, N_HEADS, HEAD_DIM), v_gain.reshape(N_HEADS, HEAD_DIM))
    v = v.reshape(B, T // GMLP_CHUNK, GMLP_CHUNK, N_HEADS, HEAD_DIM)
    w = w_s * jnp.tril(jnp.ones((GMLP_CHUNK, GMLP_CHUNK), w_s.dtype))
    sv = jnp.einsum('gts,bcsgd->bctgd', w, v) + b_s.T[:, :, None]
    return u * sv.reshape(B, T, W_MIX)


def setup_inputs(seed: int = 0) -> dict:
    key = jax.random.key(seed)
    keys = list(jax.random.split(key, 32))

    def nrm(shape, scale):
        return jax.random.normal(keys.pop(), shape, jnp.float32) * scale

    L, D = DEPTH, D_MODEL
    return {
        "x": nrm((BATCH, SEQ, D), 1.0),
        "ffn1_norm": 1.0 + nrm((L, D), 0.02),
        "ffn1_w_in": nrm((L, D, 2 * D_FF), D ** -0.5),
        "ffn1_w_out": nrm((L, D_FF, D), D_FF ** -0.5),
        "mix_norm": 1.0 + nrm((L, D), 0.02),
        "w_in": nrm((L, D, D_IN), D ** -0.5),
        "diff_q_gain": 1.0 + nrm((L, DIFF_QK_DIM), 0.02),
        "diff_k_gain": 1.0 + nrm((L, DIFF_QK_DIM), 0.02),
        "diff_lambda": nrm((L, 4, DIFF_QK_DIM), 0.1),
        "fox_q_gain": 1.0 + nrm((L, HEAD_DIM), 0.02),
        "fox_k_gain": 1.0 + nrm((L, HEAD_DIM), 0.02),
        "fox_f_bias": FOX_FORGET_BIAS + nrm((L, N_HEADS), 0.1),
        "nsa_q_gain": 1.0 + nrm((L, HEAD_DIM), 0.02),
        "nsa_k_gain": 1.0 + nrm((L, HEAD_DIM), 0.02),
        "nsa_cmp_pe": nrm((L, 2, NSA_CMP_BLOCK, HEAD_DIM), 0.1),
        "nsa_phi_w1": nrm((L, 2, NSA_CMP_BLOCK * HEAD_DIM, NSA_PHI_HIDDEN), (NSA_CMP_BLOCK * HEAD_DIM) ** -0.5),
        "nsa_phi_b1": nrm((L, 2, NSA_PHI_HIDDEN), 0.02),
        "nsa_phi_w2": nrm((L, 2, NSA_PHI_HIDDEN, HEAD_DIM), NSA_PHI_HIDDEN ** -0.5),
        "nsa_phi_b2": nrm((L, 2, HEAD_DIM), 0.02),
        "gmlp_v_gain": 1.0 + nrm((L, W_MIX), 0.02),
        "gmlp_w_s": nrm((L, N_HEADS, GMLP_CHUNK, GMLP_CHUNK), GMLP_CHUNK ** -0.5),
        "gmlp_b_s": 1.0 + nrm((L, N_HEADS, GMLP_CHUNK), 0.02),
        "w_branch": nrm((L, N_BRANCHES, W_MIX, D), W_MIX ** -0.5),
        "w_out": nrm((L, D, D), D ** -0.5),
        "ffn2_norm": 1.0 + nrm((L, D), 0.02),
        "ffn2_w_in": nrm((L, D, 2 * D_FF), D ** -0.5),
        "ffn2_w_out": nrm((L, D_FF, D), D_FF ** -0.5),
    }


def reference(x, ffn1_norm, ffn1_w_in, ffn1_w_out, mix_norm, w_in, diff_q_gain, diff_k_gain,
              diff_lambda, fox_q_gain, fox_k_gain, fox_f_bias, nsa_q_gain, nsa_k_gain, nsa_cmp_pe,
              nsa_phi_w1, nsa_phi_b1, nsa_phi_w2, nsa_phi_b2, gmlp_v_gain, gmlp_w_s, gmlp_b_s,
              w_branch, w_out, ffn2_norm, ffn2_w_in, ffn2_w_out):
    B, T, _ = x.shape
    for l in range(DEPTH):
        x = x + 0.5 * swiglu(rms_norm(x, ffn1_norm[l]), ffn1_w_in[l], ffn1_w_out[l])

        h = rms_norm(x, mix_norm[l])
        z = h @ w_in[l]
        (a_q, a_k, a_v, f_q, f_k, f_v, f_logit, n_q, n_kv, n_g, g_uv, m_g) = split_cols(z, IN_SPLIT_SIZES)

        a_q = rms_norm(a_q.reshape(B, T, N_HEADS, 2, DIFF_QK_DIM), diff_q_gain[l])
        a_k = rms_norm(a_k.reshape(B, T, N_HEADS, 2, DIFF_QK_DIM), diff_k_gain[l])
        a_v = a_v.reshape(B, T, N_HEADS, HEAD_DIM)
        lam_init = 0.8 - 0.6 * math.exp(-0.3 * l)
        lam_p = diff_lambda[l].astype(jnp.float32)
        lam = (jnp.exp(jnp.sum(lam_p[0] * lam_p[1])) - jnp.exp(jnp.sum(lam_p[2] * lam_p[3])) + lam_init)
        o_a = rms_norm(diff_attention(a_q, a_k, a_v, lam)) * (1.0 - lam_init)

        f_q = rms_norm(f_q.reshape(B, T, N_HEADS, HEAD_DIM), fox_q_gain[l])
        f_k = rms_norm(f_k.reshape(B, T, N_HEADS, HEAD_DIM), fox_k_gain[l])
        f_v = f_v.reshape(B, T, N_HEADS, HEAD_DIM)
        log_f = jax.nn.log_sigmoid(f_logit.astype(jnp.float32) + fox_f_bias[l].astype(jnp.float32))
        o_b = forgetting_attention(f_q, f_k, f_v, log_f)

        n_q = rms_norm(n_q.reshape(B, T, N_HEADS, HEAD_DIM), nsa_q_gain[l])
        kc_in, vc_in, k_s, v_s, k_w, v_w = jnp.split(n_kv, 6, axis=-1)
        k_c = rms_norm(nsa_compress(kc_in, nsa_cmp_pe[l, 0], nsa_phi_w1[l, 0], nsa_phi_b1[l, 0],
                                    nsa_phi_w2[l, 0], nsa_phi_b2[l, 0]), nsa_k_gain[l])
        v_c = nsa_compress(vc_in, nsa_cmp_pe[l, 1], nsa_phi_w1[l, 1], nsa_phi_b1[l, 1],
                           nsa_phi_w2[l, 1], nsa_phi_b2[l, 1])
        o_c = nsa_attention(n_q, k_c, v_c, rms_norm(k_s, nsa_k_gain[l]), v_s,
                            rms_norm(k_w, nsa_k_gain[l]), v_w, n_g.reshape(B, T, N_HEADS, 3))

        o_d = chunked_spatial_gating(g_uv, gmlp_v_gain[l], gmlp_w_s[l], gmlp_b_s[l])

        branches = jnp.stack([o_a.reshape(B, T, W_MIX), o_b.reshape(B, T, W_MIX),
                              o_c.reshape(B, T, W_MIX), o_d], axis=2)
        proj = jnp.einsum('btni,nid->btnd', branches, w_branch[l])
        gates = jax.nn.sigmoid(m_g.reshape(B, T, N_BRANCHES, D_MODEL))
        x = x + jnp.sum(gates * proj, axis=2) @ w_out[l]

        x = x + 0.5 * swiglu(rms_norm(x, ffn2_norm[l]), ffn2_w_in[l], ffn2_w_out[l])
    return x
```

```python
import functools
import math

import jax
import jax.numpy as jnp
from jax import lax
from jax.experimental import pallas as pl
from jax.experimental.pallas import tpu as pltpu

F32 = jnp.float32
BF16 = jnp.bfloat16

HEAD_DIM = 64
N_HEADS = 4
W_MIX = N_HEADS * HEAD_DIM
DIFF_QK_DIM = HEAD_DIM // 2
CMP_BLOCK = 32
CMP_STRIDE = 16
SLC_BLOCK = 64
TOPK = 16
WINDOW = 512
FORCE_SCORE = 1.0e4
GMLP_CHUNK = 128
RMS_EPS = 1e-6
NEG = -1.0e30

VMEM_LIMIT_BYTES = 56 * 1024 * 1024

FFN_TM = 512
FFN_TF = 256
PROJ_TM = 512
MERGE_TM = 512
ATTN_ROWS = 1024
ATTN_TK = 512
NSA_TQ = 128
NSA_TK = 512


def _dot(a, b, precision=None):
    return jnp.dot(a, b, preferred_element_type=F32, precision=precision)


def _dot_nt(a, b, precision=None):
    return lax.dot_general(a, b, (((1,), (1,)), ((), ())),
                           preferred_element_type=F32, precision=precision)


def _rms_rows(x):
    ms = jnp.mean(x * x, axis=-1, keepdims=True)
    return x * lax.rsqrt(ms + RMS_EPS)


def _group_norm(z, gmat, gain):
    ms = _dot(z * z, gmat, precision=lax.Precision.HIGHEST)
    return z * lax.rsqrt(ms + RMS_EPS) * gain


def _cparams(*sem):
    return pltpu.CompilerParams(dimension_semantics=sem, vmem_limit_bytes=VMEM_LIMIT_BYTES)


def _ffn_body(x_ref, g_ref, wa_ref, wb_ref, wo_ref, o_ref, *, n_chunks, tf):
    x = x_ref[...]
    h = (_rms_rows(x) * g_ref[...]).astype(BF16)
    acc = jnp.zeros_like(x)
    for c in range(n_chunks):
        a = _dot(h, wa_ref[:, c * tf:(c + 1) * tf])
        b = _dot(h, wb_ref[:, c * tf:(c + 1) * tf])
        g = (a * jax.nn.sigmoid(a) * b).astype(BF16)
        acc = acc + _dot(g, wo_ref[c * tf:(c + 1) * tf, :])
    o_ref[...] = x + 0.5 * acc


def _ffn(x2, gain, w_in, w_out):
    n, d = x2.shape
    d_ff = w_out.shape[0]
    wa = w_in[:, :d_ff].astype(BF16)
    wb = w_in[:, d_ff:].astype(BF16)
    wo = w_out.astype(BF16)
    tm, tf = FFN_TM, FFN_TF
    assert n % tm == 0 and d_ff % tf == 0
    const = lambda i: (0, 0)
    return pl.pallas_call(
        functools.partial(_ffn_body, n_chunks=d_ff // tf, tf=tf),
        grid=(n // tm,),
        in_specs=[pl.BlockSpec((tm, d), lambda i: (i, 0)),
                  pl.BlockSpec((1, d), const),
                  pl.BlockSpec((d, d_ff), const),
                  pl.BlockSpec((d, d_ff), const),
                  pl.BlockSpec((d_ff, d), const)],
        out_specs=pl.BlockSpec((tm, d), lambda i: (i, 0)),
        out_shape=jax.ShapeDtypeStruct((n, d), F32),
        compiler_params=_cparams("arbitrary"),
        name="ffn",
    )(x2, gain.reshape(1, d).astype(F32), wa, wb, wo)


_C_AQ, _C_AK, _C_AV = 0, 256, 512
_C_FQ, _C_FK, _C_FV = 768, 1024, 1280
_C_NQ = 1536
_C_KSW = 1792
_C_VS2 = 1920
_C_VW2 = 2048
_C_KVC = 2176
_C_UV = 2304
_C_SMALL = 2816
_C_TOTAL = 2944


def _proj_body(x_ref, g_ref, w_ref, g32_ref, g64_ref, gains_ref, fbias_ref,
               aq_ref, akT_ref, av_ref, fq_ref, fkT_ref, fv_ref, nq_ref, kswT_ref,
               vs2_ref, vw2_ref, kvc_ref, u_ref, vn_ref, c_ref, cT_ref, ng_ref,
               carry_ref, *, tiles_per_seq):
    i = pl.program_id(0)
    x = x_ref[...]
    tm = x.shape[0]
    h = (_rms_rows(x) * g_ref[...]).astype(BF16)
    z = _dot(h, w_ref[...])
    g32 = g32_ref[...]
    g64 = g64_ref[...]
    gains = gains_ref[...]

    def sec(c0, w=256):
        return z[:, c0:c0 + w]

    aq_ref[...] = (_group_norm(sec(_C_AQ), g32, gains[0:1]) * (DIFF_QK_DIM ** -0.5)).astype(BF16)
    akT_ref[0] = _group_norm(sec(_C_AK), g32, gains[1:2]).T.astype(BF16)
    av_ref[...] = sec(_C_AV).astype(BF16)
    fq_ref[...] = (_group_norm(sec(_C_FQ), g64, gains[2:3]) * (HEAD_DIM ** -0.5)).astype(BF16)
    fkT_ref[0] = _group_norm(sec(_C_FK), g64, gains[3:4]).T.astype(BF16)
    fv_ref[...] = sec(_C_FV).astype(BF16)
    nq = (_group_norm(sec(_C_NQ), g64, gains[4:5]) * (HEAD_DIM ** -0.5)).astype(BF16)
    for hd in range(N_HEADS):
        nq_ref[0, hd] = nq[:, hd * HEAD_DIM:(hd + 1) * HEAD_DIM]
    ksw = _group_norm(sec(_C_KSW, 128), g64[:128, :128], gains[5:6, :128])
    kswT_ref[0] = ksw.T.astype(BF16)
    vs2_ref[...] = sec(_C_VS2, 128).astype(BF16)
    vw2_ref[...] = sec(_C_VW2, 128).astype(BF16)
    kvc_ref[...] = sec(_C_KVC, 128)
    guv = jax.nn.gelu(sec(_C_UV, 512))
    u_ref[...] = guv[:, :256]
    vn_ref[...] = _group_norm(guv[:, 256:], g64, gains[6:7]).astype(BF16)
    small = sec(_C_SMALL, 128)
    ng_ref[...] = jax.nn.sigmoid(small)
    t = small + fbias_ref[...]
    logf = jnp.minimum(t, 0.0) - jnp.log(1.0 + jnp.exp(-jnp.abs(t)))

    @pl.when(i % tiles_per_seq == 0)
    def _():
        carry_ref[...] = jnp.zeros_like(carry_ref)

    row = lax.broadcasted_iota(jnp.int32, (tm, tm), 0)
    col = lax.broadcasted_iota(jnp.int32, (tm, tm), 1)
    tril = jnp.where(row >= col, 1.0, 0.0).astype(F32)
    csum = _dot(tril, logf, precision=lax.Precision.HIGHEST) + carry_ref[0:1, :]
    c_ref[...] = csum
    cT_ref[0] = csum.T[0:8, :]
    carry_ref[...] = jnp.broadcast_to(csum[tm - 1:tm, :], carry_ref.shape)


def _proj(x2, gain, w_cat, g32, g64, gains, fbias, batch, seq):
    n, d = x2.shape
    tm = PROJ_TM
    assert seq % tm == 0
    tps = seq // tm
    nt = n // tm
    const = lambda i: (0, 0)
    row = lambda i: (i, 0)
    trn = lambda i: (i // tps, 0, i % tps)
    outs = [
        (jax.ShapeDtypeStruct((n, 256), BF16), pl.BlockSpec((tm, 256), row)),
        (jax.ShapeDtypeStruct((batch, 256, seq), BF16), pl.BlockSpec((1, 256, tm), trn)),
        (jax.ShapeDtypeStruct((n, 256), BF16), pl.BlockSpec((tm, 256), row)),
        (jax.ShapeDtypeStruct((n, 256), BF16), pl.BlockSpec((tm, 256), row)),
        (jax.ShapeDtypeStruct((batch, 256, seq), BF16), pl.BlockSpec((1, 256, tm), trn)),
        (jax.ShapeDtypeStruct((n, 256), BF16), pl.BlockSpec((tm, 256), row)),
        (jax.ShapeDtypeStruct((batch, N_HEADS, seq, HEAD_DIM), BF16),
         pl.BlockSpec((1, N_HEADS, tm, HEAD_DIM), lambda i: (i // tps, 0, i % tps, 0))),
        (jax.ShapeDtypeStruct((batch, 128, seq), BF16), pl.BlockSpec((1, 128, tm), trn)),
        (jax.ShapeDtypeStruct((n, 128), BF16), pl.BlockSpec((tm, 128), row)),
        (jax.ShapeDtypeStruct((n, 128), BF16), pl.BlockSpec((tm, 128), row)),
        (jax.ShapeDtypeStruct((n, 128), F32), pl.BlockSpec((tm, 128), row)),
        (jax.ShapeDtypeStruct((n, 256), F32), pl.BlockSpec((tm, 256), row)),
        (jax.ShapeDtypeStruct((n, 256), BF16), pl.BlockSpec((tm, 256), row)),
        (jax.ShapeDtypeStruct((n, 128), F32), pl.BlockSpec((tm, 128), row)),
        (jax.ShapeDtypeStruct((batch, 8, seq), F32), pl.BlockSpec((1, 8, tm), trn)),
        (jax.ShapeDtypeStruct((n, 128), F32), pl.BlockSpec((tm, 128), row)),
    ]
    return pl.pallas_call(
        functools.partial(_proj_body, tiles_per_seq=tps),
        grid=(nt,),
        in_specs=[pl.BlockSpec((tm, d), row),
                  pl.BlockSpec((1, d), const),
                  pl.BlockSpec((d, _C_TOTAL), const),
                  pl.BlockSpec((256, 256), const),
                  pl.BlockSpec((256, 256), const),
                  pl.BlockSpec((8, 256), const),
                  pl.BlockSpec((1, 128), const)],
        out_specs=[o[1] for o in outs],
        out_shape=[o[0] for o in outs],
        scratch_shapes=[pltpu.VMEM((8, 128), F32)],
        compiler_params=_cparams("arbitrary"),
        name="proj",
    )(x2, gain.reshape(1, d).astype(F32), w_cat, g32, g64, gains, fbias)


def _attn_body(*refs, mode, tq, tk, lam_init):
    if mode == "fox":
        q_ref, kT_ref, v_ref, cq_ref, cT_ref, o_ref, qs_ref, m_ref, l_ref, acc_ref = refs
    else:
        q_ref, kT_ref, v_ref, lam_ref, g64_ref, o_ref, qs_ref, m_ref, l_ref, acc_ref = refs
    nv = 2 if mode == "fox" else 4
    wv = 128 // nv
    rows = nv * tq
    hp = pl.program_id(1)
    qi = pl.program_id(2)
    q0 = qi * tq

    q = q_ref[0]
    lane = lax.broadcasted_iota(jnp.int32, (tq, 128), 1)
    zero = jnp.zeros_like(q)
    for vh in range(nv):
        keep = (lane >= vh * wv) & (lane < (vh + 1) * wv)
        qs_ref[vh * tq:(vh + 1) * tq, :] = jnp.where(keep, q, zero)
    m_ref[...] = jnp.full_like(m_ref, NEG)
    l_ref[...] = jnp.zeros_like(l_ref)
    acc_ref[...] = jnp.zeros_like(acc_ref)

    if mode == "fox":
        cq = cq_ref[0]
        cq0 = jnp.where(hp == 0, cq[:, 0:1], cq[:, 2:3])
        cq1 = jnp.where(hp == 0, cq[:, 1:2], cq[:, 3:4])

    def step(ki, masked):
        k0 = pl.multiple_of(ki * tk, tk)
        kt = kT_ref[0, :, pl.ds(k0, tk)]
        vv = v_ref[0, pl.ds(k0, tk), :]
        s = _dot(qs_ref[...], kt)
        if mode == "fox":
            ck0 = cT_ref[0, pl.ds(2 * hp, 1), pl.ds(k0, tk)]
            ck1 = cT_ref[0, pl.ds(2 * hp + 1, 1), pl.ds(k0, tk)]
            bias = jnp.concatenate([cq0 - ck0, cq1 - ck1], axis=0)
            s = s + bias
        if masked:
            r = lax.broadcasted_iota(jnp.int32, (rows, tk), 0)
            c = lax.broadcasted_iota(jnp.int32, (rows, tk), 1)
            qpos = q0 + (r & (tq - 1))
            s = jnp.where(k0 + c <= qpos, s, NEG)
        m_old = m_ref[...]
        m_new = jnp.maximum(m_old, jnp.max(s, axis=-1, keepdims=True))
        alpha = jnp.exp(m_old - m_new)
        p = jnp.exp(s - m_new)
        l_ref[...] = alpha * l_ref[...] + jnp.sum(p, axis=-1, keepdims=True)
        acc_ref[...] = alpha * acc_ref[...] + _dot(p.astype(BF16), vv)
        m_ref[...] = m_new

    n_full = q0 // tk

    def loop_body(ki, carry):
        step(ki, False)
        return carry

    lax.fori_loop(0, n_full, loop_body, 0)
    step(n_full, True)

    o = acc_ref[...] / jnp.maximum(l_ref[...], 1e-30)
    lo = lane < HEAD_DIM
    if mode == "fox":
        o_ref[0] = jnp.where(lo, o[0:tq], o[tq:2 * tq]).astype(o_ref.dtype)
    else:
        lp = lam_ref[...]
        lam = (jnp.exp(jnp.sum(lp[0:1] * lp[1:2], axis=-1, keepdims=True))
               - jnp.exp(jnp.sum(lp[2:3] * lp[3:4], axis=-1, keepdims=True)) + lam_init)
        d0 = o[0:tq] - lam * o[tq:2 * tq]
        d1 = o[2 * tq:3 * tq] - lam * o[3 * tq:4 * tq]
        dd = jnp.where(lo, d0, d1)
        ms = _dot(dd * dd, g64_ref[...], precision=lax.Precision.HIGHEST)
        o_ref[0] = (dd * lax.rsqrt(ms + RMS_EPS) * (1.0 - lam_init)).astype(o_ref.dtype)


def _attn(mode, q, kT, v, extra, batch, seq, lam_init=0.0):
    nv = 2 if mode == "fox" else 4
    tq = ATTN_ROWS // nv
    tk = ATTN_TK
    assert tk % tq == 0 and seq % tk == 0 and (tq & (tq - 1)) == 0
    q3 = q.reshape(batch, seq, 256)
    v3 = v.reshape(batch, seq, 256)
    in_specs = [pl.BlockSpec((1, tq, 128), lambda b, hp, qi: (b, qi, hp)),
                pl.BlockSpec((1, 128, seq), lambda b, hp, qi: (b, hp, 0)),
                pl.BlockSpec((1, seq, 128), lambda b, hp, qi: (b, 0, hp))]
    if mode == "fox":
        c, cT = extra
        args = (q3, kT, v3, c.reshape(batch, seq, 128), cT)
        in_specs += [pl.BlockSpec((1, tq, 128), lambda b, hp, qi: (b, qi, 0)),
                     pl.BlockSpec((1, 8, seq), lambda b, hp, qi: (b, 0, 0))]
    else:
        lam_p, g64s = extra
        args = (q3, kT, v3, lam_p, g64s)
        in_specs += [pl.BlockSpec((4, DIFF_QK_DIM), lambda b, hp, qi: (0, 0)),
                     pl.BlockSpec((128, 128), lambda b, hp, qi: (0, 0))]
    rows = nv * tq
    out = pl.pallas_call(
        functools.partial(_attn_body, mode=mode, tq=tq, tk=tk, lam_init=lam_init),
        grid=(batch, 2, seq // tq),
        in_specs=in_specs,
        out_specs=pl.BlockSpec((1, tq, 128), lambda b, hp, qi: (b, qi, hp)),
        out_shape=jax.ShapeDtypeStruct((batch, seq, 256), BF16),
        scratch_shapes=[pltpu.VMEM((rows, 128), BF16),
                        pltpu.VMEM((rows, 1), F32),
                        pltpu.VMEM((rows, 1), F32),
                        pltpu.VMEM((rows, 128), F32)],
        compiler_params=_cparams("arbitrary", "arbitrary", "arbitrary"),
        name="attn_" + mode,
    )(*args)
    return out.reshape(batch * seq, 256)


def _compress_body(x_ref, pe_ref, w1a_ref, w1b_ref, b1_ref, w2_ref, b2_ref, g64_ref, kg_ref,
                   kcT_ref, vc2_ref):
    x = x_ref[0]
    nblk = x.shape[0]
    a = _dot((x + pe_ref[0:1, :]).astype(BF16), w1a_ref[...])
    b = _dot((x + pe_ref[1:2, :]).astype(BF16), w1b_ref[...])
    pre = a + pltpu.roll(b, nblk - 1, 0) + b1_ref[...]
    hid = jax.nn.gelu(pre).astype(BF16)
    out = _dot(hid, w2_ref[...]) + b2_ref[...]
    kc = _group_norm(out[:, 0:128], g64_ref[...], kg_ref[...])
    kcT_ref[0] = kc.T[0:HEAD_DIM, :].astype(BF16)
    vc2_ref[0] = out[:, 128:256].astype(BF16)


def _compress(kvc, pe_ext, w1a, w1b, b1, w2, b2, g64s, kgain, batch, seq):
    nblk = seq // CMP_STRIDE
    x = kvc.reshape(batch, nblk, CMP_STRIDE * 128)
    const = lambda b: (0, 0)
    return pl.pallas_call(
        _compress_body,
        grid=(batch,),
        in_specs=[pl.BlockSpec((1, nblk, CMP_STRIDE * 128), lambda b: (b, 0, 0)),
                  pl.BlockSpec((2, CMP_STRIDE * 128), const),
                  pl.BlockSpec(w1a.shape, const),
                  pl.BlockSpec(w1b.shape, const),
                  pl.BlockSpec(b1.shape, const),
                  pl.BlockSpec(w2.shape, const),
                  pl.BlockSpec(b2.shape, const),
                  pl.BlockSpec((128, 128), const),
                  pl.BlockSpec((1, 128), const)],
        out_specs=[pl.BlockSpec((1, HEAD_DIM, nblk), lambda b: (b, 0, 0)),
                   pl.BlockSpec((1, nblk, 128), lambda b: (b, 0, 0))],
        out_shape=[jax.ShapeDtypeStruct((batch, HEAD_DIM, nblk), BF16),
                   jax.ShapeDtypeStruct((batch, nblk, 128), BF16)],
        compiler_params=_cparams("arbitrary"),
        name="nsa_compress",
    )(x, pe_ext, w1a, w1b, b1, w2, b2, g64s, kgain)


def _nsa_body(nq_ref, kcT_ref, vc2_ref, kswT_ref, vs2_ref, vw2_ref, ng_ref, o_ref,
              imp_ref, m_ref, l_ref, acc_ref, *, tq, tk, seq):
    qi = pl.program_id(1)
    q0 = qi * tq
    rows = N_HEADS * tq
    nblk = kcT_ref.shape[2]
    nsel = seq // SLC_BLOCK

    qs = nq_ref[0].reshape(rows, HEAD_DIM)
    qpos_r = q0 + (lax.broadcasted_iota(jnp.int32, (rows, 1), 0) & (tq - 1))

    s_c = _dot(qs, kcT_ref[0])
    cend = lax.broadcasted_iota(jnp.int32, (rows, nblk), 1) * CMP_STRIDE + (CMP_BLOCK - 1)
    s_c = jnp.where(cend <= qpos_r, s_c, -jnp.inf)
    m_c = jnp.max(s_c, axis=-1, keepdims=True)
    m_c = jnp.where(m_c == -jnp.inf, 0.0, m_c)
    e_c = jnp.exp(s_c - m_c)
    p_c = e_c / jnp.maximum(jnp.sum(e_c, axis=-1, keepdims=True), 1e-30)
    o_cmp = _dot(p_c.astype(BF16), vc2_ref[0])

    p_sum = p_c[0:tq]
    for hd in range(1, N_HEADS):
        p_sum = p_sum + p_c[hd * tq:(hd + 1) * tq]
    jj = lax.broadcasted_iota(jnp.int32, (nsel, nblk), 0)
    cc = lax.broadcasted_iota(jnp.int32, (nsel, nblk), 1)
    ov_t = jnp.where((cc * CMP_STRIDE < (jj + 1) * SLC_BLOCK)
                     & (cc * CMP_STRIDE + CMP_BLOCK > jj * SLC_BLOCK), 1.0, 0.0).astype(F32)
    imp = _dot_nt(ov_t, p_sum, precision=lax.Precision.HIGHEST)
    sj = lax.broadcasted_iota(jnp.int32, (nsel, tq), 0)
    qpos_l = q0 + lax.broadcasted_iota(jnp.int32, (nsel, tq), 1)
    cur = qpos_l >> 6
    forced = (sj == 0) | (sj == cur) | (sj == cur - 1)
    valid = sj * SLC_BLOCK <= qpos_l
    imp = jnp.where(forced, FORCE_SCORE, jnp.where(valid, imp, -1.0))
    imp_ref[...] = imp

    def rank_body(i, rank):
        vi = imp_ref[pl.ds(i, 1), :]
        beats = (vi > imp) | ((vi == imp) & (sj > i))
        return rank + jnp.where(beats, 1.0, 0.0)

    rank = lax.fori_loop(0, nsel, rank_body, jnp.zeros((nsel, tq), F32))
    sel_t = jnp.where(rank < float(min(TOPK, nsel)), 1.0, 0.0).astype(F32)
    sel = sel_t.T.astype(BF16)

    m_ref[...] = jnp.full_like(m_ref, NEG)
    l_ref[...] = jnp.zeros_like(l_ref)
    acc_ref[...] = jnp.zeros_like(acc_ref)

    def sel_body(kj, carry):
        k0 = pl.multiple_of(kj * tk, tk)
        ej = lax.broadcasted_iota(jnp.int32, (nsel, tk), 0)
        et = lax.broadcasted_iota(jnp.int32, (nsel, tk), 1)
        expand = jnp.where(((k0 + et) >> 6) == ej, 1.0, 0.0).astype(BF16)
        sel_e = _dot(sel, expand)
        sel_e = jnp.concatenate([sel_e] * N_HEADS, axis=0)
        kpos = k0 + lax.broadcasted_iota(jnp.int32, (rows, tk), 1)
        keep = (sel_e > 0.5) & (kpos <= qpos_r)
        s = _dot(qs, kswT_ref[0, 0:HEAD_DIM, pl.ds(k0, tk)])
        s = jnp.where(keep, s, NEG)
        m_old = m_ref[...]
        m_new = jnp.maximum(m_old, jnp.max(s, axis=-1, keepdims=True))
        alpha = jnp.exp(m_old - m_new)
        p = jnp.exp(s - m_new)
        l_ref[...] = alpha * l_ref[...] + jnp.sum(p, axis=-1, keepdims=True)
        acc_ref[...] = alpha * acc_ref[...] + _dot(p.astype(BF16), vs2_ref[0, pl.ds(k0, tk), :])
        m_ref[...] = m_new
        return carry

    lax.fori_loop(0, (q0 + tq - 1) // tk + 1, sel_body, 0)
    o_sel = acc_ref[...] / jnp.maximum(l_ref[...], 1e-30)

    wlen = WINDOW + tq
    w0 = pl.multiple_of(jnp.maximum(q0 - WINDOW, 0), tq)
    kpos_w = w0 + lax.broadcasted_iota(jnp.int32, (rows, wlen), 1)
    dist = qpos_r - kpos_w
    s_w = _dot(qs, kswT_ref[0, HEAD_DIM:2 * HEAD_DIM, pl.ds(w0, wlen)])
    s_w = jnp.where((dist >= 0) & (dist < WINDOW), s_w, -jnp.inf)
    e_w = jnp.exp(s_w - jnp.max(s_w, axis=-1, keepdims=True))
    l_w = jnp.maximum(jnp.sum(e_w, axis=-1, keepdims=True), 1e-30)
    o_win = _dot(e_w.astype(BF16), vw2_ref[0, pl.ds(w0, wlen), :]) / l_w

    gates = ng_ref[0]
    lo = lax.broadcasted_iota(jnp.int32, (tq, 128), 1) < HEAD_DIM
    comb = []
    for hd in range(N_HEADS):
        r0, r1 = hd * tq, (hd + 1) * tq
        g0 = 4 + 3 * hd
        comb.append(gates[:, g0:g0 + 1] * o_cmp[r0:r1]
                    + gates[:, g0 + 1:g0 + 2] * o_sel[r0:r1]
                    + gates[:, g0 + 2:g0 + 3] * o_win[r0:r1])
    o_ref[0] = jnp.concatenate([jnp.where(lo, comb[0], comb[1]),
                                jnp.where(lo, comb[2], comb[3])], axis=-1).astype(o_ref.dtype)


def _nsa(nq, kcT, vc2, kswT, vs2, vw2, ng, batch, seq):
    tq, tk = NSA_TQ, NSA_TK
    assert seq % tk == 0 and tk % tq == 0 and WINDOW % tq == 0 and seq >= WINDOW + tq
    nblk = seq // CMP_STRIDE
    nsel = seq // SLC_BLOCK
    rows = N_HEADS * tq
    full = lambda b, qi: (b, 0, 0)
    out = pl.pallas_call(
        functools.partial(_nsa_body, tq=tq, tk=tk, seq=seq),
        grid=(batch, seq // tq),
        in_specs=[pl.BlockSpec((1, N_HEADS, tq, HEAD_DIM), lambda b, qi: (b, 0, qi, 0)),
                  pl.BlockSpec((1, HEAD_DIM, nblk), full),
                  pl.BlockSpec((1, nblk, 128), full),
                  pl.BlockSpec((1, 128, seq), full),
                  pl.BlockSpec((1, seq, 128), full),
                  pl.BlockSpec((1, seq, 128), full),
                  pl.BlockSpec((1, tq, 128), lambda b, qi: (b, qi, 0))],
        out_specs=pl.BlockSpec((1, tq, 256), lambda b, qi: (b, qi, 0)),
        out_shape=jax.ShapeDtypeStruct((batch, seq, 256), BF16),
        scratch_shapes=[pltpu.VMEM((nsel, tq), F32),
                        pltpu.VMEM((rows, 1), F32),
                        pltpu.VMEM((rows, 1), F32),
                        pltpu.VMEM((rows, 128), F32)],
        compiler_params=_cparams("arbitrary", "arbitrary"),
        name="nsa",
    )(nq, kcT, vc2, kswT, vs2.reshape(batch, seq, 128), vw2.reshape(batch, seq, 128),
      ng.reshape(batch, seq, 128))
    return out.reshape(batch * seq, 256)


def _merge_body(x_ref, g_ref, oa_ref, ob_ref, oc_ref, u_ref, vn_ref, ws_ref, bt_ref,
                wg_ref, wbr_ref, wout_ref, o_ref):
    x = x_ref[...]
    tm = x.shape[0]
    h = (_rms_rows(x) * g_ref[...]).astype(BF16)

    r = lax.broadcasted_iota(jnp.int32, (GMLP_CHUNK, GMLP_CHUNK), 0)
    c = lax.broadcasted_iota(jnp.int32, (GMLP_CHUNK, GMLP_CHUNK), 1)
    lane_grp = lax.broadcasted_iota(jnp.int32, (GMLP_CHUNK, W_MIX), 1) // HEAD_DIM
    w_tril = [jnp.where(r >= c, ws_ref[g], 0.0).astype(BF16) for g in range(N_HEADS)]
    od = []
    for ch in range(tm // GMLP_CHUNK):
        sl = slice(ch * GMLP_CHUNK, (ch + 1) * GMLP_CHUNK)
        vch = vn_ref[sl, :]
        sv = jnp.zeros((GMLP_CHUNK, W_MIX), F32)
        for g in range(N_HEADS):
            sv = jnp.where(lane_grp == g, _dot(w_tril[g], vch), sv)
        od.append(u_ref[sl, :] * (sv + bt_ref[...]))
    o_d = jnp.concatenate(od, axis=0).astype(BF16)

    branches = (oa_ref[...], ob_ref[...], oc_ref[...], o_d)
    acc = jnp.zeros_like(x)
    for n in range(4):
        gate = jax.nn.sigmoid(_dot(h, wg_ref[n]))
        acc = acc + gate * _dot(branches[n], wbr_ref[n])
    o_ref[...] = x + _dot(acc.astype(BF16), wout_ref[...])


def _merge(x2, gain, oa, ob, oc, u, vn, w_s, bt, wg, wbr, wout):
    n, d = x2.shape
    tm = MERGE_TM
    row = lambda i: (i, 0)
    c2 = lambda i: (0, 0)
    c3 = lambda i: (0, 0, 0)
    return pl.pallas_call(
        _merge_body,
        grid=(n // tm,),
        in_specs=[pl.BlockSpec((tm, d), row),
                  pl.BlockSpec((1, d), c2),
                  pl.BlockSpec((tm, 256), row),
                  pl.BlockSpec((tm, 256), row),
                  pl.BlockSpec((tm, 256), row),
                  pl.BlockSpec((tm, 256), row),
                  pl.BlockSpec((tm, 256), row),
                  pl.BlockSpec((N_HEADS, GMLP_CHUNK, GMLP_CHUNK), c3),
                  pl.BlockSpec((GMLP_CHUNK, W_MIX), c2),
                  pl.BlockSpec((4, d, d), c3),
                  pl.BlockSpec((4, W_MIX, d), c3),
                  pl.BlockSpec((d, d), c2)],
        out_specs=pl.BlockSpec((tm, d), row),
        out_shape=jax.ShapeDtypeStruct((n, d), F32),
        compiler_params=_cparams("arbitrary"),
        name="merge",
    )(x2, gain.reshape(1, d).astype(F32), oa, ob, oc, u, vn, w_s, bt, wg, wbr, wout)


def _block_diag_mean(width, group):
    idx = jnp.arange(width) // group
    return (idx[:, None] == idx[None, :]).astype(F32) / float(group)


def _stage_proj_weight(w_in_l):
    nkv = 1796
    cols = [w_in_l[:, 0:1536],
            w_in_l[:, 1540:1796],
            w_in_l[:, nkv + 128:nkv + 192],
            w_in_l[:, nkv + 256:nkv + 320],
            w_in_l[:, nkv + 192:nkv + 256],
            w_in_l[:, nkv + 192:nkv + 256],
            w_in_l[:, nkv + 320:nkv + 384],
            w_in_l[:, nkv + 320:nkv + 384],
            w_in_l[:, nkv:nkv + 128],
            w_in_l[:, 2192:2704],
            w_in_l[:, 1536:1540],
            w_in_l[:, 2180:2192],
            jnp.zeros((w_in_l.shape[0], 112), w_in_l.dtype)]
    w = jnp.concatenate(cols, axis=1)
    assert w.shape[1] == _C_TOTAL
    return w.astype(BF16), w_in_l[:, 2704:]


def _stage_compress(pe, w1, b1, w2, b2):
    hid = w1.shape[-1]
    half = CMP_STRIDE
    pe_ext = jnp.concatenate([pe[0], pe[1]], axis=-1)
    pe_ext = jnp.stack([pe_ext[:half].reshape(-1), pe_ext[half:].reshape(-1)])
    w1k = w1[0].reshape(CMP_BLOCK, HEAD_DIM, hid)
    w1v = w1[1].reshape(CMP_BLOCK, HEAD_DIM, hid)
    zk = jnp.zeros_like(w1k)
    top = jnp.concatenate([w1k, zk], axis=-1)
    bot = jnp.concatenate([zk, w1v], axis=-1)
    w1e = jnp.concatenate([top, bot], axis=1)
    w1a = w1e[:half].reshape(half * 128, 2 * hid).astype(BF16)
    w1b = w1e[half:].reshape(half * 128, 2 * hid).astype(BF16)
    b1e = jnp.concatenate([b1[0], b1[1]]).reshape(1, 2 * hid)
    z = jnp.zeros((hid, HEAD_DIM), F32)
    w2e = jnp.concatenate([jnp.concatenate([w2[0], z, z, z], axis=1),
                           jnp.concatenate([z, z, w2[1], w2[1]], axis=1)], axis=0).astype(BF16)
    zb = jnp.zeros((HEAD_DIM,), F32)
    b2e = jnp.concatenate([b2[0], zb, b2[1], b2[1]]).reshape(1, 256)
    return pe_ext, w1a, w1b, b1e, w2e, b2e


def kernel(x, ffn1_norm, ffn1_w_in, ffn1_w_out, mix_norm, w_in, diff_q_gain, diff_k_gain, diff_lambda, fox_q_gain, fox_k_gain, fox_f_bias, nsa_q_gain, nsa_k_gain, nsa_cmp_pe, nsa_phi_w1, nsa_phi_b1, nsa_phi_w2, nsa_phi_b2, gmlp_v_gain, gmlp_w_s, gmlp_b_s, w_branch, w_out, ffn2_norm, ffn2_w_in, ffn2_w_out):
    batch, seq, d = x.shape
    depth = w_in.shape[0]
    n = batch * seq
    g32 = _block_diag_mean(256, DIFF_QK_DIM)
    g64 = _block_diag_mean(256, HEAD_DIM)
    g64s = g64[:128, :128]
    x2 = x.reshape(n, d).astype(F32)

    for l in range(depth):
        x2 = _ffn(x2, ffn1_norm[l], ffn1_w_in[l], ffn1_w_out[l])

        w_cat, w_gate = _stage_proj_weight(w_in[l])
        gains = jnp.stack([
            jnp.tile(diff_q_gain[l], 8), jnp.tile(diff_k_gain[l], 8),
            jnp.tile(fox_q_gain[l], 4), jnp.tile(fox_k_gain[l], 4),
            jnp.tile(nsa_q_gain[l], 4), jnp.tile(nsa_k_gain[l], 4),
            gmlp_v_gain[l], jnp.zeros((256,), F32)]).astype(F32)
        fbias = jnp.concatenate([fox_f_bias[l].astype(F32), jnp.zeros((124,), F32)]).reshape(1, 128)
        (aq, akT, av, fq, fkT, fv, nq, kswT, vs2, vw2, kvc, u, vn, c, cT, ng) = _proj(
            x2, mix_norm[l], w_cat, g32, g64, gains, fbias, batch, seq)

        lam_init = 0.8 - 0.6 * math.exp(-0.3 * l)
        o_a = _attn("diff", aq, akT, av, (diff_lambda[l].astype(F32), g64s), batch, seq, lam_init)
        o_b = _attn("fox", fq, fkT, fv, (c, cT), batch, seq)

        pe_ext, w1a, w1b, b1e, w2e, b2e = _stage_compress(
            nsa_cmp_pe[l], nsa_phi_w1[l], nsa_phi_b1[l], nsa_phi_w2[l], nsa_phi_b2[l])
        kgain = jnp.tile(nsa_k_gain[l], 2).reshape(1, 128).astype(F32)
        kcT, vc2 = _compress(kvc, pe_ext, w1a, w1b, b1e, w2e, b2e, g64s, kgain, batch, seq)
        o_c = _nsa(nq, kcT, vc2, kswT, vs2, vw2, ng, batch, seq)

        bt = jnp.repeat(gmlp_b_s[l].T, HEAD_DIM, axis=1).astype(F32)
        wg = jnp.stack([w_gate[:, i * d:(i + 1) * d] for i in range(4)]).astype(BF16)
        x2 = _merge(x2, mix_norm[l], o_a, o_b, o_c, u, vn, gmlp_w_s[l].astype(F32), bt,
                    wg, w_branch[l].astype(BF16), w_out[l].astype(BF16))

        x2 = _ffn(x2, ffn2_norm[l], ffn2_w_in[l], ffn2_w_out[l])
    return x2.reshape(batch, seq, d).astype(x.dtype)
```

```python
import functools
import math

import jax
import jax.numpy as jnp
from jax import lax
from jax.experimental import pallas as pl
from jax.experimental.pallas import tpu as pltpu

F32 = jnp.float32
BF16 = jnp.bfloat16

HEAD_DIM = 64
N_HEADS = 4
W_MIX = N_HEADS * HEAD_DIM
DIFF_QK_DIM = HEAD_DIM // 2
CMP_BLOCK = 32
CMP_STRIDE = 16
SLC_BLOCK = 64
TOPK = 16
WINDOW = 512
FORCE_SCORE = 1.0e4
GMLP_CHUNK = 128
RMS_EPS = 1e-6
NEG = -1.0e30
LOG2E = math.log2(math.e)

VMEM_LIMIT_BYTES = 56 * 1024 * 1024

FFN_TM = 512
FFN_TF = 256
PROJ_TM = 512
MERGE_TM = 512
ATTN_COLS = 1024
ATTN_CB = 512
ATTN_TK = 512
NSA_TQ = 128
NSA_TK = 512


def _dot(a, b, precision=None):
    return jnp.dot(a, b, preferred_element_type=F32, precision=precision)


def _dot_nt(a, b, precision=None):
    return lax.dot_general(a, b, (((1,), (1,)), ((), ())),
                           preferred_element_type=F32, precision=precision)


def _rms_rows(x):
    ms = jnp.mean(x * x, axis=-1, keepdims=True)
    return x * lax.rsqrt(ms + RMS_EPS)


def _group_norm(z, gmat, gain):
    ms = _dot(z * z, gmat, precision=lax.Precision.HIGHEST)
    return z * lax.rsqrt(ms + RMS_EPS) * gain


def _split3(c):
    hi = c.astype(BF16)
    r1 = c - hi.astype(F32)
    mid = r1.astype(BF16)
    lo = (r1 - mid.astype(F32)).astype(BF16)
    return hi, mid, lo


def _cparams(*sem):
    return pltpu.CompilerParams(dimension_semantics=sem, vmem_limit_bytes=VMEM_LIMIT_BYTES)


def _ffn_body(x_ref, g_ref, wa_ref, wb_ref, wo_ref, o_ref, *, n_chunks, tf):
    x = x_ref[...]
    h = (_rms_rows(x) * g_ref[...]).astype(BF16)
    acc = jnp.zeros_like(x)
    for c in range(n_chunks):
        a = _dot(h, wa_ref[:, c * tf:(c + 1) * tf])
        b = _dot(h, wb_ref[:, c * tf:(c + 1) * tf])
        g = (a * jax.nn.sigmoid(a) * b).astype(BF16)
        acc = acc + _dot(g, wo_ref[c * tf:(c + 1) * tf, :])
    o_ref[...] = x + 0.5 * acc


def _ffn(x2, gain, w_in, w_out):
    n, d = x2.shape
    d_ff = w_out.shape[0]
    wa = w_in[:, :d_ff].astype(BF16)
    wb = w_in[:, d_ff:].astype(BF16)
    wo = w_out.astype(BF16)
    tm, tf = FFN_TM, FFN_TF
    assert n % tm == 0 and d_ff % tf == 0
    const = lambda i: (0, 0)
    return pl.pallas_call(
        functools.partial(_ffn_body, n_chunks=d_ff // tf, tf=tf),
        grid=(n // tm,),
        in_specs=[pl.BlockSpec((tm, d), lambda i: (i, 0)),
                  pl.BlockSpec((1, d), const),
                  pl.BlockSpec((d, d_ff), const),
                  pl.BlockSpec((d, d_ff), const),
                  pl.BlockSpec((d_ff, d), const)],
        out_specs=pl.BlockSpec((tm, d), lambda i: (i, 0)),
        out_shape=jax.ShapeDtypeStruct((n, d), F32),
        compiler_params=_cparams("arbitrary"),
        name="ffn",
    )(x2, gain.reshape(1, d).astype(F32), wa, wb, wo)


_C_AQ, _C_AK, _C_AV = 0, 256, 512
_C_FQ, _C_FK, _C_FV = 768, 1024, 1280
_C_NQ = 1536
_C_KSW = 1792
_C_VS2 = 1920
_C_VW2 = 2048
_C_KVC = 2176
_C_UV = 2304
_C_SMALL = 2816
_C_TOTAL = 2944
_GATE_COL = 16
_AUG_HI, _AUG_MID, _AUG_LO, _AUG_ONE = 0, 4, 8, 12


def _proj_body(x_ref, g_ref, w_ref, g32_ref, g64_ref, gains_ref, fbias_ref,
               aqT_ref, ak_ref, avT_ref, fqT_ref, fk_ref, fvT_ref, faug_ref, cT_ref,
               nq_ref, kswT_ref, vs2_ref, vw2_ref, kvc_ref, u_ref, vn_ref, ng_ref,
               carry_ref, *, tiles_per_seq):
    i = pl.program_id(0)
    x = x_ref[...]
    tm = x.shape[0]
    h = (_rms_rows(x) * g_ref[...]).astype(BF16)
    z = _dot(h, w_ref[...])
    g32 = g32_ref[...]
    g64 = g64_ref[...]
    gains = gains_ref[...]

    def sec(c0, w=256):
        return z[:, c0:c0 + w]

    aq = _group_norm(sec(_C_AQ), g32, gains[0:1]) * (DIFF_QK_DIM ** -0.5 * LOG2E)
    aqT_ref[0] = aq.T.astype(BF16)
    ak_ref[...] = _group_norm(sec(_C_AK), g32, gains[1:2]).astype(BF16)
    avT_ref[0] = sec(_C_AV).T.astype(BF16)
    fq = _group_norm(sec(_C_FQ), g64, gains[2:3]) * (HEAD_DIM ** -0.5 * LOG2E)
    fqT_ref[0] = fq.T.astype(BF16)
    fk_ref[...] = _group_norm(sec(_C_FK), g64, gains[3:4]).astype(BF16)
    fvT_ref[0] = sec(_C_FV).T.astype(BF16)
    nq = (_group_norm(sec(_C_NQ), g64, gains[4:5]) * (HEAD_DIM ** -0.5)).astype(BF16)
    for hd in range(N_HEADS):
        nq_ref[0, hd] = nq[:, hd * HEAD_DIM:(hd + 1) * HEAD_DIM]
    ksw = _group_norm(sec(_C_KSW, 128), g64[:128, :128], gains[5:6, :128])
    kswT_ref[0] = ksw.T.astype(BF16)
    vs2_ref[...] = sec(_C_VS2, 128).astype(BF16)
    vw2_ref[...] = sec(_C_VW2, 128).astype(BF16)
    kvc_ref[...] = sec(_C_KVC, 128)
    guv = jax.nn.gelu(sec(_C_UV, 512))
    u_ref[...] = guv[:, :256]
    vn_ref[...] = _group_norm(guv[:, 256:], g64, gains[6:7]).astype(BF16)
    small = sec(_C_SMALL, 128)
    ng_ref[...] = jax.nn.sigmoid(small)
    t = small + fbias_ref[...]
    logf = jnp.minimum(t, 0.0) - jnp.log(1.0 + jnp.exp(-jnp.abs(t)))

    @pl.when(i % tiles_per_seq == 0)
    def _():
        carry_ref[...] = jnp.zeros_like(carry_ref)

    row = lax.broadcasted_iota(jnp.int32, (tm, tm), 0)
    col = lax.broadcasted_iota(jnp.int32, (tm, tm), 1)
    tril = jnp.where(row >= col, 1.0, 0.0).astype(F32)
    csum = _dot(tril, logf, precision=lax.Precision.HIGHEST) + carry_ref[0:1, :]
    carry_ref[...] = jnp.broadcast_to(csum[tm - 1:tm, :], carry_ref.shape)
    c2 = csum * LOG2E
    cT_ref[0] = c2.T[0:8, :]
    hi, mid, lo = _split3(c2)
    lane = lax.broadcasted_iota(jnp.int32, (tm, 128), 1)
    one = jnp.where(lane < _AUG_ONE + 3, 1.0, 0.0).astype(BF16)
    faug_ref[...] = jnp.where(lane < _AUG_MID, hi,
                              jnp.where(lane < _AUG_LO, mid, jnp.where(lane < _AUG_ONE, lo, one)))


def _proj(x2, gain, w_cat, g32, g64, gains, fbias, batch, seq):
    n, d = x2.shape
    tm = PROJ_TM
    assert seq % tm == 0
    tps = seq // tm
    nt = n // tm
    const = lambda i: (0, 0)
    row = lambda i: (i, 0)
    trn = lambda i: (i // tps, 0, i % tps)
    nat = lambda w, dt: (jax.ShapeDtypeStruct((n, w), dt), pl.BlockSpec((tm, w), row))
    tr = lambda r, dt: (jax.ShapeDtypeStruct((batch, r, seq), dt), pl.BlockSpec((1, r, tm), trn))
    outs = [
        tr(256, BF16), nat(256, BF16), tr(256, BF16),
        tr(256, BF16), nat(256, BF16), tr(256, BF16),
        nat(128, BF16), tr(8, F32),
        (jax.ShapeDtypeStruct((batch, N_HEADS, seq, HEAD_DIM), BF16),
         pl.BlockSpec((1, N_HEADS, tm, HEAD_DIM), lambda i: (i // tps, 0, i % tps, 0))),
        tr(128, BF16),
        nat(128, BF16), nat(128, BF16),
        nat(128, F32), nat(256, F32), nat(256, BF16), nat(128, F32),
    ]
    return pl.pallas_call(
        functools.partial(_proj_body, tiles_per_seq=tps),
        grid=(nt,),
        in_specs=[pl.BlockSpec((tm, d), row),
                  pl.BlockSpec((1, d), const),
                  pl.BlockSpec((d, _C_TOTAL), const),
                  pl.BlockSpec((256, 256), const),
                  pl.BlockSpec((256, 256), const),
                  pl.BlockSpec((8, 256), const),
                  pl.BlockSpec((1, 128), const)],
        out_specs=[o[1] for o in outs],
        out_shape=[o[0] for o in outs],
        scratch_shapes=[pltpu.VMEM((8, 128), F32)],
        compiler_params=_cparams("arbitrary"),
        name="proj",
    )(x2, gain.reshape(1, d).astype(F32), w_cat, g32, g64, gains, fbias)


def _attn_body(*refs, mode, tq, tk, cb, lam_init):
    if mode == "fox":
        qT_ref, k_ref, vT_ref, aug_ref, cT_ref, o_ref, qs_ref, m_ref, l_ref, acc_ref, s_ref = refs
    else:
        qT_ref, k_ref, vT_ref, lam_ref, g64_ref, o_ref, qs_ref, m_ref, l_ref, acc_ref, s_ref = refs
    nv = 2 if mode == "fox" else 4
    wv = 128 // nv
    cols = nv * tq
    hp = pl.program_id(1)
    qi = pl.program_id(2)
    q0 = qi * tq

    qT = qT_ref[0]
    rowi = lax.broadcasted_iota(jnp.int32, (128, tq), 0)
    zero = jnp.zeros_like(qT)
    for vh in range(nv):
        keep = (rowi >= vh * wv) & (rowi < (vh + 1) * wv)
        qs_ref[0:128, vh * tq:(vh + 1) * tq] = jnp.where(keep, qT, zero)
        if mode == "fox":
            hd = 2 * hp + vh
            hi, mid, lo = _split3(cT_ref[0, pl.ds(hd, 1), :])
            neg = (rowi == _AUG_HI + hd) | (rowi == _AUG_MID + hd) | (rowi == _AUG_LO + hd)
            aug = jnp.where(neg, -1.0, 0.0).astype(BF16)
            aug = jnp.where(rowi == _AUG_ONE, hi, aug)
            aug = jnp.where(rowi == _AUG_ONE + 1, mid, aug)
            aug = jnp.where(rowi == _AUG_ONE + 2, lo, aug)
            qs_ref[128:256, vh * tq:(vh + 1) * tq] = aug
    m_ref[...] = jnp.full_like(m_ref, NEG)
    l_ref[...] = jnp.zeros_like(l_ref)
    acc_ref[...] = jnp.zeros_like(acc_ref)

    nb = cols // cb
    n_full = q0 // tk
    krow = lax.broadcasted_iota(jnp.int32, (tk, cb), 0)
    kcol = lax.broadcasted_iota(jnp.int32, (tk, cb), 1)
    row_minus_q = [krow - ((j * cb + kcol) & (tq - 1)) for j in range(nb)]
    no_mask = jnp.int32(2 ** 30)

    def key_tile(k0):
        kk = k_ref[0, pl.ds(k0, tk), :]
        if mode == "fox":
            kk = jnp.concatenate([kk, aug_ref[0, pl.ds(k0, tk), :]], axis=1)
        return kk

    def scores(kk, j, thr):
        s = _dot(kk, qs_ref[:, j * cb:(j + 1) * cb])
        if thr is not None:
            s = jnp.where(row_minus_q[j] <= thr, s, NEG)
        return s

    s_ref[...] = scores(key_tile(0), 0, jnp.where(n_full == 0, q0, no_mask))

    def step(ki, masked, final):
        k0 = pl.multiple_of(ki * tk, tk)
        kk = key_tile(k0)
        vt = vT_ref[0, :, pl.ds(k0, tk)]
        s = s_ref[...]
        for j in range(nb):
            cs = slice(j * cb, (j + 1) * cb)
            s_next = None
            if j + 1 < nb:
                s_next = scores(kk, j + 1, q0 - k0 if masked else None)
            elif not final:
                k1 = pl.multiple_of(k0 + tk, tk)
                thr = q0 - k1 if masked else jnp.where(ki + 1 == n_full, q0 - k1, no_mask)
                s_ref[...] = scores(key_tile(k1), 0, thr)
            m_old = m_ref[:, cs]
            m_new = jnp.maximum(m_old, jnp.max(s, axis=0, keepdims=True))
            alpha = jnp.exp2(m_old - m_new)
            p = jnp.exp2(s - m_new)
            l_ref[:, cs] = alpha * l_ref[:, cs] + jnp.sum(p, axis=0, keepdims=True)
            acc_ref[:, cs] = alpha * acc_ref[:, cs] + _dot(vt, p.astype(BF16))
            m_ref[:, cs] = m_new
            s = s_next

    def loop_body(ki, carry):
        step(ki, False, False)
        return carry

    lax.fori_loop(0, n_full, loop_body, 0)
    n_diag = max(tq // tk, 1)
    for dt in range(n_diag):
        step(n_full + dt, True, dt == n_diag - 1)

    o = acc_ref[...] / jnp.maximum(l_ref[...], 1e-30)
    lo_rows = rowi < HEAD_DIM
    if mode == "fox":
        out_t = jnp.where(lo_rows, o[:, 0:tq], o[:, tq:2 * tq])
    else:
        lp = lam_ref[...]
        lam = (jnp.exp(jnp.sum(lp[0:1] * lp[1:2], axis=-1, keepdims=True))
               - jnp.exp(jnp.sum(lp[2:3] * lp[3:4], axis=-1, keepdims=True)) + lam_init)
        d0 = o[:, 0:tq] - lam * o[:, tq:2 * tq]
        d1 = o[:, 2 * tq:3 * tq] - lam * o[:, 3 * tq:4 * tq]
        dd = jnp.where(lo_rows, d0, d1)
        ms = _dot(g64_ref[...], dd * dd, precision=lax.Precision.HIGHEST)
        out_t = dd * lax.rsqrt(ms + RMS_EPS) * (1.0 - lam_init)
    o_ref[0] = out_t.T.astype(o_ref.dtype)


def _attn(mode, qT, k, vT, extra, batch, seq, lam_init=0.0):
    nv = 2 if mode == "fox" else 4
    tq = ATTN_COLS // nv
    tk, cb = ATTN_TK, ATTN_CB
    assert (tk % tq == 0 or tq % tk == 0) and seq % tk == 0 and (tq % cb == 0 or cb % tq == 0)
    assert (tq & (tq - 1)) == 0
    k3 = k.reshape(batch, seq, 256)
    in_specs = [pl.BlockSpec((1, 128, tq), lambda b, hp, qi: (b, hp, qi)),
                pl.BlockSpec((1, seq, 128), lambda b, hp, qi: (b, 0, hp)),
                pl.BlockSpec((1, 128, seq), lambda b, hp, qi: (b, hp, 0))]
    if mode == "fox":
        faug, cT = extra
        args = (qT, k3, vT, faug.reshape(batch, seq, 128), cT)
        in_specs += [pl.BlockSpec((1, seq, 128), lambda b, hp, qi: (b, 0, 0)),
                     pl.BlockSpec((1, 8, tq), lambda b, hp, qi: (b, 0, qi))]
        depth = 256
    else:
        lam_p, g64s = extra
        args = (qT, k3, vT, lam_p, g64s)
        in_specs += [pl.BlockSpec((4, DIFF_QK_DIM), lambda b, hp, qi: (0, 0)),
                     pl.BlockSpec((128, 128), lambda b, hp, qi: (0, 0))]
        depth = 128
    cols = nv * tq
    out = pl.pallas_call(
        functools.partial(_attn_body, mode=mode, tq=tq, tk=tk, cb=cb, lam_init=lam_init),
        grid=(batch, 2, seq // tq),
        in_specs=in_specs,
        out_specs=pl.BlockSpec((1, tq, 128), lambda b, hp, qi: (b, qi, hp)),
        out_shape=jax.ShapeDtypeStruct((batch, seq, 256), BF16),
        scratch_shapes=[pltpu.VMEM((depth, cols), BF16),
                        pltpu.VMEM((1, cols), F32),
                        pltpu.VMEM((1, cols), F32),
                        pltpu.VMEM((128, cols), F32),
                        pltpu.VMEM((tk, cb), F32)],
        compiler_params=_cparams("arbitrary", "arbitrary", "arbitrary"),
        name="attn_" + mode,
    )(*args)
    return out.reshape(batch * seq, 256)


def _compress_body(x_ref, pe_ref, w1a_ref, w1b_ref, b1_ref, w2_ref, b2_ref, g64_ref, kg_ref,
                   kcT_ref, vc2_ref):
    x = x_ref[0]
    nblk = x.shape[0]
    a = _dot((x + pe_ref[0:1, :]).astype(BF16), w1a_ref[...])
    b = _dot((x + pe_ref[1:2, :]).astype(BF16), w1b_ref[...])
    pre = a + pltpu.roll(b, nblk - 1, 0) + b1_ref[...]
    hid = jax.nn.gelu(pre).astype(BF16)
    out = _dot(hid, w2_ref[...]) + b2_ref[...]
    kc = _group_norm(out[:, 0:128], g64_ref[...], kg_ref[...])
    kcT_ref[0] = kc.T[0:HEAD_DIM, :].astype(BF16)
    vc2_ref[0] = out[:, 128:256].astype(BF16)


def _compress(kvc, pe_ext, w1a, w1b, b1, w2, b2, g64s, kgain, batch, seq):
    nblk = seq // CMP_STRIDE
    x = kvc.reshape(batch, nblk, CMP_STRIDE * 128)
    const = lambda b: (0, 0)
    return pl.pallas_call(
        _compress_body,
        grid=(batch,),
        in_specs=[pl.BlockSpec((1, nblk, CMP_STRIDE * 128), lambda b: (b, 0, 0)),
                  pl.BlockSpec((2, CMP_STRIDE * 128), const),
                  pl.BlockSpec(w1a.shape, const),
                  pl.BlockSpec(w1b.shape, const),
                  pl.BlockSpec(b1.shape, const),
                  pl.BlockSpec(w2.shape, const),
                  pl.BlockSpec(b2.shape, const),
                  pl.BlockSpec((128, 128), const),
                  pl.BlockSpec((1, 128), const)],
        out_specs=[pl.BlockSpec((1, HEAD_DIM, nblk), lambda b: (b, 0, 0)),
                   pl.BlockSpec((1, nblk, 128), lambda b: (b, 0, 0))],
        out_shape=[jax.ShapeDtypeStruct((batch, HEAD_DIM, nblk), BF16),
                   jax.ShapeDtypeStruct((batch, nblk, 128), BF16)],
        compiler_params=_cparams("arbitrary"),
        name="nsa_compress",
    )(x, pe_ext, w1a, w1b, b1, w2, b2, g64s, kgain)


def _nsa_body(nq_ref, kcT_ref, vc2_ref, kswT_ref, vs2_ref, vw2_ref, ng_ref, o_ref,
              imp_ref, m_ref, l_ref, acc_ref, *, tq, tk, seq):
    qi = pl.program_id(1)
    q0 = qi * tq
    rows = N_HEADS * tq
    nblk = kcT_ref.shape[2]
    nsel = seq // SLC_BLOCK

    qs = nq_ref[0].reshape(rows, HEAD_DIM)
    qpos_r = q0 + (lax.broadcasted_iota(jnp.int32, (rows, 1), 0) & (tq - 1))

    s_c = _dot(qs, kcT_ref[0])
    cend = lax.broadcasted_iota(jnp.int32, (rows, nblk), 1) * CMP_STRIDE + (CMP_BLOCK - 1)
    s_c = jnp.where(cend <= qpos_r, s_c, -jnp.inf)
    m_c = jnp.max(s_c, axis=-1, keepdims=True)
    m_c = jnp.where(m_c == -jnp.inf, 0.0, m_c)
    e_c = jnp.exp(s_c - m_c)
    p_c = e_c / jnp.maximum(jnp.sum(e_c, axis=-1, keepdims=True), 1e-30)
    o_cmp = _dot(p_c.astype(BF16), vc2_ref[0])

    p_sum = p_c[0:tq]
    for hd in range(1, N_HEADS):
        p_sum = p_sum + p_c[hd * tq:(hd + 1) * tq]
    jj = lax.broadcasted_iota(jnp.int32, (nsel, nblk), 0)
    cc = lax.broadcasted_iota(jnp.int32, (nsel, nblk), 1)
    ov_t = jnp.where((cc * CMP_STRIDE < (jj + 1) * SLC_BLOCK)
                     & (cc * CMP_STRIDE + CMP_BLOCK > jj * SLC_BLOCK), 1.0, 0.0).astype(F32)
    imp = _dot_nt(ov_t, p_sum, precision=lax.Precision.HIGHEST)
    sj = lax.broadcasted_iota(jnp.int32, (nsel, tq), 0)
    qpos_l = q0 + lax.broadcasted_iota(jnp.int32, (nsel, tq), 1)
    cur = qpos_l >> 6
    forced = (sj == 0) | (sj == cur) | (sj == cur - 1)
    valid = sj * SLC_BLOCK <= qpos_l
    imp = jnp.where(forced, FORCE_SCORE, jnp.where(valid, imp, -1.0))
    imp_ref[...] = imp

    def rank_body(i, rank):
        vi = imp_ref[pl.ds(i, 1), :]
        beats = (vi > imp) | ((vi == imp) & (sj > i))
        return rank + jnp.where(beats, 1.0, 0.0)

    rank = lax.fori_loop(0, nsel, rank_body, jnp.zeros((nsel, tq), F32))
    sel_t = jnp.where(rank < float(min(TOPK, nsel)), 1.0, 0.0).astype(F32)
    sel = sel_t.T.astype(BF16)

    m_ref[...] = jnp.full_like(m_ref, NEG)
    l_ref[...] = jnp.zeros_like(l_ref)
    acc_ref[...] = jnp.zeros_like(acc_ref)

    def sel_body(kj, carry):
        k0 = pl.multiple_of(kj * tk, tk)
        ej = lax.broadcasted_iota(jnp.int32, (nsel, tk), 0)
        et = lax.broadcasted_iota(jnp.int32, (nsel, tk), 1)
        expand = jnp.where(((k0 + et) >> 6) == ej, 1.0, 0.0).astype(BF16)
        sel_e = _dot(sel, expand)
        sel_e = jnp.concatenate([sel_e] * N_HEADS, axis=0)
        kpos = k0 + lax.broadcasted_iota(jnp.int32, (rows, tk), 1)
        keep = (sel_e > 0.5) & (kpos <= qpos_r)
        s = _dot(qs, kswT_ref[0, 0:HEAD_DIM, pl.ds(k0, tk)])
        s = jnp.where(keep, s, NEG)
        m_old = m_ref[...]
        m_new = jnp.maximum(m_old, jnp.max(s, axis=-1, keepdims=True))
        alpha = jnp.exp(m_old - m_new)
        p = jnp.exp(s - m_new)
        l_ref[...] = alpha * l_ref[...] + jnp.sum(p, axis=-1, keepdims=True)
        acc_ref[...] = alpha * acc_ref[...] + _dot(p.astype(BF16), vs2_ref[0, pl.ds(k0, tk), :])
        m_ref[...] = m_new
        return carry

    lax.fori_loop(0, (q0 + tq - 1) // tk + 1, sel_body, 0)
    o_sel = acc_ref[...] / jnp.maximum(l_ref[...], 1e-30)

    wlen = WINDOW + tq
    w0 = pl.multiple_of(jnp.maximum(q0 - WINDOW, 0), tq)
    kpos_w = w0 + lax.broadcasted_iota(jnp.int32, (rows, wlen), 1)
    dist = qpos_r - kpos_w
    s_w = _dot(qs, kswT_ref[0, HEAD_DIM:2 * HEAD_DIM, pl.ds(w0, wlen)])
    s_w = jnp.where((dist >= 0) & (dist < WINDOW), s_w, -jnp.inf)
    e_w = jnp.exp(s_w - jnp.max(s_w, axis=-1, keepdims=True))
    l_w = jnp.maximum(jnp.sum(e_w, axis=-1, keepdims=True), 1e-30)
    o_win = _dot(e_w.astype(BF16), vw2_ref[0, pl.ds(w0, wlen), :]) / l_w

    gates = ng_ref[0]
    lo = lax.broadcasted_iota(jnp.int32, (tq, 128), 1) < HEAD_DIM
    comb = []
    for hd in range(N_HEADS):
        r0, r1 = hd * tq, (hd + 1) * tq
        g0 = _GATE_COL + 3 * hd
        comb.append(gates[:, g0:g0 + 1] * o_cmp[r0:r1]
                    + gates[:, g0 + 1:g0 + 2] * o_sel[r0:r1]
                    + gates[:, g0 + 2:g0 + 3] * o_win[r0:r1])
    o_ref[0] = jnp.concatenate([jnp.where(lo, comb[0], comb[1]),
                                jnp.where(lo, comb[2], comb[3])], axis=-1).astype(o_ref.dtype)


def _nsa(nq, kcT, vc2, kswT, vs2, vw2, ng, batch, seq):
    tq, tk = NSA_TQ, NSA_TK
    assert seq % tk == 0 and tk % tq == 0 and WINDOW % tq == 0 and seq >= WINDOW + tq
    nblk = seq // CMP_STRIDE
    nsel = seq // SLC_BLOCK
    rows = N_HEADS * tq
    full = lambda b, qi: (b, 0, 0)
    out = pl.pallas_call(
        functools.partial(_nsa_body, tq=tq, tk=tk, seq=seq),
        grid=(batch, seq // tq),
        in_specs=[pl.BlockSpec((1, N_HEADS, tq, HEAD_DIM), lambda b, qi: (b, 0, qi, 0)),
                  pl.BlockSpec((1, HEAD_DIM, nblk), full),
                  pl.BlockSpec((1, nblk, 128), full),
                  pl.BlockSpec((1, 128, seq), full),
                  pl.BlockSpec((1, seq, 128), full),
                  pl.BlockSpec((1, seq, 128), full),
                  pl.BlockSpec((1, tq, 128), lambda b, qi: (b, qi, 0))],
        out_specs=pl.BlockSpec((1, tq, 256), lambda b, qi: (b, qi, 0)),
        out_shape=jax.ShapeDtypeStruct((batch, seq, 256), BF16),
        scratch_shapes=[pltpu.VMEM((nsel, tq), F32),
                        pltpu.VMEM((rows, 1), F32),
                        pltpu.VMEM((rows, 1), F32),
                        pltpu.VMEM((rows, 128), F32)],
        compiler_params=_cparams("arbitrary", "arbitrary"),
        name="nsa",
    )(nq, kcT, vc2, kswT, vs2.reshape(batch, seq, 128), vw2.reshape(batch, seq, 128),
      ng.reshape(batch, seq, 128))
    return out.reshape(batch * seq, 256)


def _merge_body(x_ref, g_ref, oa_ref, ob_ref, oc_ref, u_ref, vn_ref, ws_ref, bt_ref,
                wg_ref, wbr_ref, wout_ref, o_ref):
    x = x_ref[...]
    tm = x.shape[0]
    h = (_rms_rows(x) * g_ref[...]).astype(BF16)

    r = lax.broadcasted_iota(jnp.int32, (GMLP_CHUNK, GMLP_CHUNK), 0)
    c = lax.broadcasted_iota(jnp.int32, (GMLP_CHUNK, GMLP_CHUNK), 1)
    lane_grp = lax.broadcasted_iota(jnp.int32, (GMLP_CHUNK, W_MIX), 1) // HEAD_DIM
    w_tril = [jnp.where(r >= c, ws_ref[g], 0.0).astype(BF16) for g in range(N_HEADS)]
    od = []
    for ch in range(tm // GMLP_CHUNK):
        sl = slice(ch * GMLP_CHUNK, (ch + 1) * GMLP_CHUNK)
        vch = vn_ref[sl, :]
        sv = jnp.zeros((GMLP_CHUNK, W_MIX), F32)
        for g in range(N_HEADS):
            sv = jnp.where(lane_grp == g, _dot(w_tril[g], vch), sv)
        od.append(u_ref[sl, :] * (sv + bt_ref[...]))
    o_d = jnp.concatenate(od, axis=0).astype(BF16)

    branches = (oa_ref[...], ob_ref[...], oc_ref[...], o_d)
    acc = jnp.zeros_like(x)
    for n in range(4):
        gate = jax.nn.sigmoid(_dot(h, wg_ref[n]))
        acc = acc + gate * _dot(branches[n], wbr_ref[n])
    o_ref[...] = x + _dot(acc.astype(BF16), wout_ref[...])


def _merge(x2, gain, oa, ob, oc, u, vn, w_s, bt, wg, wbr, wout):
    n, d = x2.shape
    tm = MERGE_TM
    row = lambda i: (i, 0)
    c2 = lambda i: (0, 0)
    c3 = lambda i: (0, 0, 0)
    return pl.pallas_call(
        _merge_body,
        grid=(n // tm,),
        in_specs=[pl.BlockSpec((tm, d), row),
                  pl.BlockSpec((1, d), c2),
                  pl.BlockSpec((tm, 256), row),
                  pl.BlockSpec((tm, 256), row),
                  pl.BlockSpec((tm, 256), row),
                  pl.BlockSpec((tm, 256), row),
                  pl.BlockSpec((tm, 256), row),
                  pl.BlockSpec((N_HEADS, GMLP_CHUNK, GMLP_CHUNK), c3),
                  pl.BlockSpec((GMLP_CHUNK, W_MIX), c2),
                  pl.BlockSpec((4, d, d), c3),
                  pl.BlockSpec((4, W_MIX, d), c3),
                  pl.BlockSpec((d, d), c2)],
        out_specs=pl.BlockSpec((tm, d), row),
        out_shape=jax.ShapeDtypeStruct((n, d), F32),
        compiler_params=_cparams("arbitrary"),
        name="merge",
    )(x2, gain.reshape(1, d).astype(F32), oa, ob, oc, u, vn, w_s, bt, wg, wbr, wout)


def _block_diag_mean(width, group):
    idx = jnp.arange(width) // group
    return (idx[:, None] == idx[None, :]).astype(F32) / float(group)


def _stage_proj_weight(w_in_l):
    nkv = 1796
    d = w_in_l.shape[0]
    flog = w_in_l[:, 1536:1540]
    cols = [w_in_l[:, 0:1536],
            w_in_l[:, 1540:1796],
            w_in_l[:, nkv + 128:nkv + 192],
            w_in_l[:, nkv + 256:nkv + 320],
            w_in_l[:, nkv + 192:nkv + 256],
            w_in_l[:, nkv + 192:nkv + 256],
            w_in_l[:, nkv + 320:nkv + 384],
            w_in_l[:, nkv + 320:nkv + 384],
            w_in_l[:, nkv:nkv + 128],
            w_in_l[:, 2192:2704],
            flog, flog, flog,
            jnp.zeros((d, _GATE_COL - 12), w_in_l.dtype),
            w_in_l[:, 2180:2192],
            jnp.zeros((d, 128 - _GATE_COL - 12), w_in_l.dtype)]
    w = jnp.concatenate(cols, axis=1)
    assert w.shape[1] == _C_TOTAL
    return w.astype(BF16), w_in_l[:, 2704:]


def _stage_compress(pe, w1, b1, w2, b2):
    hid = w1.shape[-1]
    half = CMP_STRIDE
    pe_ext = jnp.concatenate([pe[0], pe[1]], axis=-1)
    pe_ext = jnp.stack([pe_ext[:half].reshape(-1), pe_ext[half:].reshape(-1)])
    w1k = w1[0].reshape(CMP_BLOCK, HEAD_DIM, hid)
    w1v = w1[1].reshape(CMP_BLOCK, HEAD_DIM, hid)
    zk = jnp.zeros_like(w1k)
    top = jnp.concatenate([w1k, zk], axis=-1)
    bot = jnp.concatenate([zk, w1v], axis=-1)
    w1e = jnp.concatenate([top, bot], axis=1)
    w1a = w1e[:half].reshape(half * 128, 2 * hid).astype(BF16)
    w1b = w1e[half:].reshape(half * 128, 2 * hid).astype(BF16)
    b1e = jnp.concatenate([b1[0], b1[1]]).reshape(1, 2 * hid)
    z = jnp.zeros((hid, HEAD_DIM), F32)
    w2e = jnp.concatenate([jnp.concatenate([w2[0], z, z, z], axis=1),
                           jnp.concatenate([z, z, w2[1], w2[1]], axis=1)], axis=0).astype(BF16)
    zb = jnp.zeros((HEAD_DIM,), F32)
    b2e = jnp.concatenate([b2[0], zb, b2[1], b2[1]]).reshape(1, 256)
    return pe_ext, w1a, w1b, b1e, w2e, b2e


def kernel(x, ffn1_norm, ffn1_w_in, ffn1_w_out, mix_norm, w_in, diff_q_gain, diff_k_gain, diff_lambda, fox_q_gain, fox_k_gain, fox_f_bias, nsa_q_gain, nsa_k_gain, nsa_cmp_pe, nsa_phi_w1, nsa_phi_b1, nsa_phi_w2, nsa_phi_b2, gmlp_v_gain, gmlp_w_s, gmlp_b_s, w_branch, w_out, ffn2_norm, ffn2_w_in, ffn2_w_out):
    batch, seq, d = x.shape
    depth = w_in.shape[0]
    n = batch * seq
    g32 = _block_diag_mean(256, DIFF_QK_DIM)
    g64 = _block_diag_mean(256, HEAD_DIM)
    g64s = g64[:128, :128]
    x2 = x.reshape(n, d).astype(F32)

    for l in range(depth):
        x2 = _ffn(x2, ffn1_norm[l], ffn1_w_in[l], ffn1_w_out[l])

        w_cat, w_gate = _stage_proj_weight(w_in[l])
        gains = jnp.stack([
            jnp.tile(diff_q_gain[l], 8), jnp.tile(diff_k_gain[l], 8),
            jnp.tile(fox_q_gain[l], 4), jnp.tile(fox_k_gain[l], 4),
            jnp.tile(nsa_q_gain[l], 4), jnp.tile(nsa_k_gain[l], 4),
            gmlp_v_gain[l], jnp.zeros((256,), F32)]).astype(F32)
        fb = fox_f_bias[l].astype(F32)
        fbias = jnp.concatenate([fb, fb, fb, jnp.zeros((116,), F32)]).reshape(1, 128)
        (aqT, ak, avT, fqT, fk, fvT, faug, cT, nq, kswT, vs2, vw2, kvc, u, vn, ng) = _proj(
            x2, mix_norm[l], w_cat, g32, g64, gains, fbias, batch, seq)

        lam_init = 0.8 - 0.6 * math.exp(-0.3 * l)
        o_a = _attn("diff", aqT, ak, avT, (diff_lambda[l].astype(F32), g64s), batch, seq, lam_init)
        o_b = _attn("fox", fqT, fk, fvT, (faug, cT), batch, seq)

        pe_ext, w1a, w1b, b1e, w2e, b2e = _stage_compress(
            nsa_cmp_pe[l], nsa_phi_w1[l], nsa_phi_b1[l], nsa_phi_w2[l], nsa_phi_b2[l])
        kgain = jnp.tile(nsa_k_gain[l], 2).reshape(1, 128).astype(F32)
        kcT, vc2 = _compress(kvc, pe_ext, w1a, w1b, b1e, w2e, b2e, g64s, kgain, batch, seq)
        o_c = _nsa(nq, kcT, vc2, kswT, vs2, vw2, ng, batch, seq)

        bt = jnp.repeat(gmlp_b_s[l].T, HEAD_DIM, axis=1).astype(F32)
        wg = jnp.stack([w_gate[:, i * d:(i + 1) * d] for i in range(4)]).astype(BF16)
        x2 = _merge(x2, mix_norm[l], o_a, o_b, o_c, u, vn, gmlp_w_s[l].astype(F32), bt,
                    wg, w_branch[l].astype(BF16), w_out[l].astype(BF16))

        x2 = _ffn(x2, ffn2_norm[l], ffn2_w_in[l], ffn2_w_out[l])
    return x2.reshape(batch, seq, d).astype(x.dtype)
```

```python
import functools
import math

import jax
import jax.numpy as jnp
from jax import lax
from jax.experimental import pallas as pl
from jax.experimental.pallas import tpu as pltpu

F32 = jnp.float32
BF16 = jnp.bfloat16

HEAD_DIM = 64
N_HEADS = 4
W_MIX = N_HEADS * HEAD_DIM
DIFF_QK_DIM = HEAD_DIM // 2
CMP_BLOCK = 32
CMP_STRIDE = 16
SLC_BLOCK = 64
TOPK = 16
WINDOW = 512
FORCE_SCORE = 1.0e4
GMLP_CHUNK = 128
RMS_EPS = 1e-6
NEG = -1.0e30
LOG2E = math.log2(math.e)
INT_MIN = -2 ** 31

VMEM_LIMIT_BYTES = 56 * 1024 * 1024

FFN_TM = 512
FFN_TF = 256
PROJ_TM = 512
MERGE_TM = 512
ATTN_COLS = 1024
ATTN_CB = 512
ATTN_TK = 512
NSA_TQ = 256
MAX_SEL_BLOCKS = 128


def _dot(a, b, precision=None):
    return jnp.dot(a, b, preferred_element_type=F32, precision=precision)


def _rms_rows(x):
    ms = jnp.mean(x * x, axis=-1, keepdims=True)
    return x * lax.rsqrt(ms + RMS_EPS)


def _group_norm(z, gmat, gain):
    ms = _dot(z * z, gmat, precision=lax.Precision.HIGHEST)
    return z * lax.rsqrt(ms + RMS_EPS) * gain


def _split3(c):
    hi = c.astype(BF16)
    r1 = c - hi.astype(F32)
    mid = r1.astype(BF16)
    lo = (r1 - mid.astype(F32)).astype(BF16)
    return hi, mid, lo


def _cparams(*sem):
    return pltpu.CompilerParams(dimension_semantics=sem, vmem_limit_bytes=VMEM_LIMIT_BYTES)


def _ffn_body(x_ref, g_ref, wa_ref, wb_ref, wo_ref, o_ref, *, n_chunks, tf):
    x = x_ref[...]
    h = (_rms_rows(x) * g_ref[...]).astype(BF16)
    acc = jnp.zeros_like(x)
    for c in range(n_chunks):
        a = _dot(h, wa_ref[:, c * tf:(c + 1) * tf])
        b = _dot(h, wb_ref[:, c * tf:(c + 1) * tf])
        g = (a * jax.nn.sigmoid(a) * b).astype(BF16)
        acc = acc + _dot(g, wo_ref[c * tf:(c + 1) * tf, :])
    o_ref[...] = x + 0.5 * acc


def _ffn(x2, gain, w_in, w_out):
    n, d = x2.shape
    d_ff = w_out.shape[0]
    wa = w_in[:, :d_ff].astype(BF16)
    wb = w_in[:, d_ff:].astype(BF16)
    wo = w_out.astype(BF16)
    tm, tf = FFN_TM, FFN_TF
    assert n % tm == 0 and d_ff % tf == 0
    const = lambda i: (0, 0)
    return pl.pallas_call(
        functools.partial(_ffn_body, n_chunks=d_ff // tf, tf=tf),
        grid=(n // tm,),
        in_specs=[pl.BlockSpec((tm, d), lambda i: (i, 0)),
                  pl.BlockSpec((1, d), const),
                  pl.BlockSpec((d, d_ff), const),
                  pl.BlockSpec((d, d_ff), const),
                  pl.BlockSpec((d_ff, d), const)],
        out_specs=pl.BlockSpec((tm, d), lambda i: (i, 0)),
        out_shape=jax.ShapeDtypeStruct((n, d), F32),
        compiler_params=_cparams("arbitrary"),
        name="ffn",
    )(x2, gain.reshape(1, d).astype(F32), wa, wb, wo)


_C_AQ, _C_AK, _C_AV = 0, 256, 512
_C_FQ, _C_FK, _C_FV = 768, 1024, 1280
_C_NQ = 1536
_C_KSW = 1792
_C_VSW = 1920
_C_KVC = 2048
_C_UV = 2176
_C_SMALL = 2688
_C_TOTAL = 2816
_GATE_COL = 16
_AUG_HI, _AUG_MID, _AUG_LO, _AUG_ONE = 0, 4, 8, 12


def _proj_body(x_ref, g_ref, w_ref, g32_ref, g64_ref, gains_ref, fbias_ref,
               aqT_ref, ak_ref, avT_ref, fqT_ref, fk_ref, fvT_ref, faug_ref, cT_ref,
               nqT_ref, ksw_ref, vswT_ref, kvc_ref, u_ref, vn_ref, ngT_ref,
               carry_ref, *, tiles_per_seq):
    i = pl.program_id(0)
    x = x_ref[...]
    tm = x.shape[0]
    h = (_rms_rows(x) * g_ref[...]).astype(BF16)
    z = _dot(h, w_ref[...])
    g32 = g32_ref[...]
    g64 = g64_ref[...]
    gains = gains_ref[...]

    def sec(c0, w=256):
        return z[:, c0:c0 + w]

    aq = _group_norm(sec(_C_AQ), g32, gains[0:1]) * (DIFF_QK_DIM ** -0.5 * LOG2E)
    aqT_ref[0] = aq.T.astype(BF16)
    ak_ref[...] = _group_norm(sec(_C_AK), g32, gains[1:2]).astype(BF16)
    avT_ref[0] = sec(_C_AV).T.astype(BF16)
    fq = _group_norm(sec(_C_FQ), g64, gains[2:3]) * (HEAD_DIM ** -0.5 * LOG2E)
    fqT_ref[0] = fq.T.astype(BF16)
    fk_ref[...] = _group_norm(sec(_C_FK), g64, gains[3:4]).astype(BF16)
    fvT_ref[0] = sec(_C_FV).T.astype(BF16)
    nq = _group_norm(sec(_C_NQ), g64, gains[4:5]) * (HEAD_DIM ** -0.5 * LOG2E)
    nqT_ref[0] = nq.T.astype(BF16)
    ksw_ref[...] = _group_norm(sec(_C_KSW, 128), g64[:128, :128], gains[5:6, :128]).astype(BF16)
    vswT_ref[0] = sec(_C_VSW, 128).T.astype(BF16)
    kvc_ref[...] = sec(_C_KVC, 128)
    guv = jax.nn.gelu(sec(_C_UV, 512))
    u_ref[...] = guv[:, :256]
    vn_ref[...] = _group_norm(guv[:, 256:], g64, gains[6:7]).astype(BF16)
    small = sec(_C_SMALL, 128)
    ngT_ref[0] = jax.nn.sigmoid(small).T[_GATE_COL:_GATE_COL + 16, :]
    t = small + fbias_ref[...]
    logf = jnp.minimum(t, 0.0) - jnp.log(1.0 + jnp.exp(-jnp.abs(t)))

    @pl.when(i % tiles_per_seq == 0)
    def _():
        carry_ref[...] = jnp.zeros_like(carry_ref)

    row = lax.broadcasted_iota(jnp.int32, (tm, tm), 0)
    col = lax.broadcasted_iota(jnp.int32, (tm, tm), 1)
    tril = jnp.where(row >= col, 1.0, 0.0).astype(F32)
    csum = _dot(tril, logf, precision=lax.Precision.HIGHEST) + carry_ref[0:1, :]
    carry_ref[...] = jnp.broadcast_to(csum[tm - 1:tm, :], carry_ref.shape)
    c2 = csum * LOG2E
    cT_ref[0] = c2.T[0:8, :]
    hi, mid, lo = _split3(c2)
    lane = lax.broadcasted_iota(jnp.int32, (tm, 128), 1)
    one = jnp.where(lane < _AUG_ONE + 3, 1.0, 0.0).astype(BF16)
    faug_ref[...] = jnp.where(lane < _AUG_MID, hi,
                              jnp.where(lane < _AUG_LO, mid, jnp.where(lane < _AUG_ONE, lo, one)))


def _proj(x2, gain, w_cat, g32, g64, gains, fbias, batch, seq):
    n, d = x2.shape
    tm = PROJ_TM
    assert seq % tm == 0
    tps = seq // tm
    nt = n // tm
    const = lambda i: (0, 0)
    row = lambda i: (i, 0)
    trn = lambda i: (i // tps, 0, i % tps)
    nat = lambda w, dt: (jax.ShapeDtypeStruct((n, w), dt), pl.BlockSpec((tm, w), row))
    tr = lambda r, dt: (jax.ShapeDtypeStruct((batch, r, seq), dt), pl.BlockSpec((1, r, tm), trn))
    outs = [
        tr(256, BF16), nat(256, BF16), tr(256, BF16),
        tr(256, BF16), nat(256, BF16), tr(256, BF16),
        nat(128, BF16), tr(8, F32),
        tr(256, BF16), nat(128, BF16), tr(128, BF16),
        nat(128, F32), nat(256, F32), nat(256, BF16), tr(16, F32),
    ]
    return pl.pallas_call(
        functools.partial(_proj_body, tiles_per_seq=tps),
        grid=(nt,),
        in_specs=[pl.BlockSpec((tm, d), row),
                  pl.BlockSpec((1, d), const),
                  pl.BlockSpec((d, _C_TOTAL), const),
                  pl.BlockSpec((256, 256), const),
                  pl.BlockSpec((256, 256), const),
                  pl.BlockSpec((8, 256), const),
                  pl.BlockSpec((1, 128), const)],
        out_specs=[o[1] for o in outs],
        out_shape=[o[0] for o in outs],
        scratch_shapes=[pltpu.VMEM((8, 128), F32)],
        compiler_params=_cparams("arbitrary"),
        name="proj",
    )(x2, gain.reshape(1, d).astype(F32), w_cat, g32, g64, gains, fbias)


def _flash_causal(key_tile, val_tile, qs_ref, m_ref, l_ref, acc_ref, s_ref, *, q0, tq, tk, cb):
    cols = qs_ref.shape[1]
    nb = cols // cb
    n_full = q0 // tk
    m_ref[...] = jnp.full_like(m_ref, NEG)
    l_ref[...] = jnp.zeros_like(l_ref)
    acc_ref[...] = jnp.zeros_like(acc_ref)
    krow = lax.broadcasted_iota(jnp.int32, (tk, cb), 0)
    kcol = lax.broadcasted_iota(jnp.int32, (tk, cb), 1)
    row_minus_q = [krow - ((j * cb + kcol) & (tq - 1)) for j in range(nb)]
    no_mask = jnp.int32(2 ** 30)

    def scores(kk, j, thr):
        s = _dot(kk, qs_ref[:, j * cb:(j + 1) * cb])
        if thr is not None:
            s = jnp.where(row_minus_q[j] <= thr, s, NEG)
        return s

    s_ref[...] = scores(key_tile(0), 0, jnp.where(n_full == 0, q0, no_mask))

    def step(ki, masked, final):
        k0 = pl.multiple_of(ki * tk, tk)
        kk = key_tile(k0)
        vt = val_tile(k0)
        s = s_ref[...]
        for j in range(nb):
            cs = slice(j * cb, (j + 1) * cb)
            s_next = None
            if j + 1 < nb:
                s_next = scores(kk, j + 1, q0 - k0 if masked else None)
            elif not final:
                k1 = pl.multiple_of(k0 + tk, tk)
                thr = q0 - k1 if masked else jnp.where(ki + 1 == n_full, q0 - k1, no_mask)
                s_ref[...] = scores(key_tile(k1), 0, thr)
            m_old = m_ref[:, cs]
            m_new = jnp.maximum(m_old, jnp.max(s, axis=0, keepdims=True))
            alpha = jnp.exp2(m_old - m_new)
            p = jnp.exp2(s - m_new)
            l_ref[:, cs] = alpha * l_ref[:, cs] + jnp.sum(p, axis=0, keepdims=True)
            acc_ref[:, cs] = alpha * acc_ref[:, cs] + _dot(vt, p.astype(BF16))
            m_ref[:, cs] = m_new
            s = s_next

    def loop_body(ki, carry):
        step(ki, False, False)
        return carry

    lax.fori_loop(0, n_full, loop_body, 0)
    n_diag = max(tq // tk, 1)
    for dt in range(n_diag):
        step(n_full + dt, True, dt == n_diag - 1)
    return acc_ref[...] / jnp.maximum(l_ref[...], 1e-30)


def _attn_body(*refs, mode, tq, tk, cb, lam_init):
    if mode == "fox":
        qT_ref, k_ref, vT_ref, aug_ref, cT_ref, o_ref, qs_ref, m_ref, l_ref, acc_ref, s_ref = refs
    else:
        qT_ref, k_ref, vT_ref, lam_ref, g64_ref, o_ref, qs_ref, m_ref, l_ref, acc_ref, s_ref = refs
    nv = 2 if mode == "fox" else 4
    wv = 128 // nv
    hp = pl.program_id(1)
    qi = pl.program_id(2)
    q0 = qi * tq

    qT = qT_ref[0]
    rowi = lax.broadcasted_iota(jnp.int32, (128, tq), 0)
    zero = jnp.zeros_like(qT)
    for vh in range(nv):
        keep = (rowi >= vh * wv) & (rowi < (vh + 1) * wv)
        qs_ref[0:128, vh * tq:(vh + 1) * tq] = jnp.where(keep, qT, zero)
        if mode == "fox":
            hd = 2 * hp + vh
            hi, mid, lo = _split3(cT_ref[0, pl.ds(hd, 1), :])
            neg = (rowi == _AUG_HI + hd) | (rowi == _AUG_MID + hd) | (rowi == _AUG_LO + hd)
            aug = jnp.where(neg, -1.0, 0.0).astype(BF16)
            aug = jnp.where(rowi == _AUG_ONE, hi, aug)
            aug = jnp.where(rowi == _AUG_ONE + 1, mid, aug)
            aug = jnp.where(rowi == _AUG_ONE + 2, lo, aug)
            qs_ref[128:256, vh * tq:(vh + 1) * tq] = aug

    def key_tile(k0):
        kk = k_ref[0, pl.ds(k0, tk), :]
        if mode == "fox":
            kk = jnp.concatenate([kk, aug_ref[0, pl.ds(k0, tk), :]], axis=1)
        return kk

    def val_tile(k0):
        return vT_ref[0, :, pl.ds(k0, tk)]

    o = _flash_causal(key_tile, val_tile, qs_ref, m_ref, l_ref, acc_ref, s_ref,
                      q0=q0, tq=tq, tk=tk, cb=cb)
    lo_rows = rowi < HEAD_DIM
    if mode == "fox":
        out_t = jnp.where(lo_rows, o[:, 0:tq], o[:, tq:2 * tq])
    else:
        lp = lam_ref[...]
        lam = (jnp.exp(jnp.sum(lp[0:1] * lp[1:2], axis=-1, keepdims=True))
               - jnp.exp(jnp.sum(lp[2:3] * lp[3:4], axis=-1, keepdims=True)) + lam_init)
        d0 = o[:, 0:tq] - lam * o[:, tq:2 * tq]
        d1 = o[:, 2 * tq:3 * tq] - lam * o[:, 3 * tq:4 * tq]
        dd = jnp.where(lo_rows, d0, d1)
        ms = _dot(g64_ref[...], dd * dd, precision=lax.Precision.HIGHEST)
        out_t = dd * lax.rsqrt(ms + RMS_EPS) * (1.0 - lam_init)
    o_ref[0] = out_t.T.astype(o_ref.dtype)


def _attn(mode, qT, k, vT, extra, batch, seq, lam_init=0.0):
    nv = 2 if mode == "fox" else 4
    tq = ATTN_COLS // nv
    tk, cb = ATTN_TK, ATTN_CB
    assert (tk % tq == 0 or tq % tk == 0) and seq % tk == 0 and (tq % cb == 0 or cb % tq == 0)
    assert (tq & (tq - 1)) == 0
    k3 = k.reshape(batch, seq, 256)
    in_specs = [pl.BlockSpec((1, 128, tq), lambda b, hp, qi: (b, hp, qi)),
                pl.BlockSpec((1, seq, 128), lambda b, hp, qi: (b, 0, hp)),
                pl.BlockSpec((1, 128, seq), lambda b, hp, qi: (b, hp, 0))]
    if mode == "fox":
        faug, cT = extra
        args = (qT, k3, vT, faug.reshape(batch, seq, 128), cT)
        in_specs += [pl.BlockSpec((1, seq, 128), lambda b, hp, qi: (b, 0, 0)),
                     pl.BlockSpec((1, 8, tq), lambda b, hp, qi: (b, 0, qi))]
        depth = 256
    else:
        lam_p, g64s = extra
        args = (qT, k3, vT, lam_p, g64s)
        in_specs += [pl.BlockSpec((4, DIFF_QK_DIM), lambda b, hp, qi: (0, 0)),
                     pl.BlockSpec((128, 128), lambda b, hp, qi: (0, 0))]
        depth = 128
    cols = nv * tq
    out = pl.pallas_call(
        functools.partial(_attn_body, mode=mode, tq=tq, tk=tk, cb=cb, lam_init=lam_init),
        grid=(batch, 2, seq // tq),
        in_specs=in_specs,
        out_specs=pl.BlockSpec((1, tq, 128), lambda b, hp, qi: (b, qi, hp)),
        out_shape=jax.ShapeDtypeStruct((batch, seq, 256), BF16),
        scratch_shapes=[pltpu.VMEM((depth, cols), BF16),
                        pltpu.VMEM((1, cols), F32),
                        pltpu.VMEM((1, cols), F32),
                        pltpu.VMEM((128, cols), F32),
                        pltpu.VMEM((tk, cb), F32)],
        compiler_params=_cparams("arbitrary", "arbitrary", "arbitrary"),
        name="attn_" + mode,
    )(*args)
    return out.reshape(batch * seq, 256)


def _compress_body(x_ref, pe_ref, w1a_ref, w1b_ref, b1_ref, w2_ref, b2_ref, g64_ref, kg_ref,
                   kc_ref, vcT_ref):
    x = x_ref[0]
    nblk = x.shape[0]
    a = _dot((x + pe_ref[0:1, :]).astype(BF16), w1a_ref[...])
    b = _dot((x + pe_ref[1:2, :]).astype(BF16), w1b_ref[...])
    pre = a + pltpu.roll(b, nblk - 1, 0) + b1_ref[...]
    hid = jax.nn.gelu(pre).astype(BF16)
    out = _dot(hid, w2_ref[...]) + b2_ref[...]
    kc_ref[0] = _group_norm(out[:, 0:128], g64_ref[...], kg_ref[...]).astype(BF16)
    vcT_ref[0] = out[:, 128:256].T[0:HEAD_DIM, :].astype(BF16)


def _compress(kvc, pe_ext, w1a, w1b, b1, w2, b2, g64s, kgain, batch, seq):
    nblk = seq // CMP_STRIDE
    x = kvc.reshape(batch, nblk, CMP_STRIDE * 128)
    const = lambda b: (0, 0)
    return pl.pallas_call(
        _compress_body,
        grid=(batch,),
        in_specs=[pl.BlockSpec((1, nblk, CMP_STRIDE * 128), lambda b: (b, 0, 0)),
                  pl.BlockSpec((2, CMP_STRIDE * 128), const),
                  pl.BlockSpec(w1a.shape, const),
                  pl.BlockSpec(w1b.shape, const),
                  pl.BlockSpec(b1.shape, const),
                  pl.BlockSpec(w2.shape, const),
                  pl.BlockSpec(b2.shape, const),
                  pl.BlockSpec((128, 128), const),
                  pl.BlockSpec((1, 128), const)],
        out_specs=[pl.BlockSpec((1, nblk, 128), lambda b: (b, 0, 0)),
                   pl.BlockSpec((1, HEAD_DIM, nblk), lambda b: (b, 0, 0))],
        out_shape=[jax.ShapeDtypeStruct((batch, nblk, 128), BF16),
                   jax.ShapeDtypeStruct((batch, HEAD_DIM, nblk), BF16)],
        compiler_params=_cparams("arbitrary"),
        name="nsa_compress",
    )(x, pe_ext, w1a, w1b, b1, w2, b2, g64s, kgain)


def _topk_select(imp, k):
    nsel, tq = imp.shape
    key = lax.bitcast_convert_type(imp, jnp.int32)

    def count(mask):
        return jnp.sum(jnp.where(mask, 1.0, 0.0), axis=0, keepdims=True)

    kf = float(k)
    t = jnp.where(count(key >= 0) >= kf, 0, INT_MIN).astype(jnp.int32)
    for bit in range(30, -1, -1):
        cand = t + (1 << bit)
        t = jnp.where(count(key >= cand) >= kf, cand, t)
    gt = key > t
    eq = key == t
    need = kf - count(gt)
    r = lax.broadcasted_iota(jnp.int32, (nsel, nsel), 0)
    c = lax.broadcasted_iota(jnp.int32, (nsel, nsel), 1)
    tril = jnp.where(r >= c, 1.0, 0.0).astype(BF16)
    prefix_eq = _dot(tril, jnp.where(eq, 1.0, 0.0).astype(BF16))
    return gt | (eq & (prefix_eq <= need))


def _nsa_body(nqT_ref, kc_ref, vcT_ref, ksw_ref, vswT_ref, hot_ref, ngT_ref, o_ref,
              qs_ref, qw_ref, m_ref, l_ref, acc_ref, s_ref, *, tq, tk, cb, seq):
    qi = pl.program_id(1)
    q0 = qi * tq
    cols = N_HEADS * tq
    nblk = kc_ref.shape[1]
    nsel = seq // SLC_BLOCK
    nb = cols // cb

    zero = jnp.zeros((HEAD_DIM, tq), BF16)
    for hd in range(N_HEADS):
        qh = nqT_ref[0, hd * HEAD_DIM:(hd + 1) * HEAD_DIM, :]
        cs = slice(hd * tq, (hd + 1) * tq)
        qs_ref[0:HEAD_DIM, cs] = qh
        qs_ref[HEAD_DIM:128, cs] = zero
        qw_ref[0:HEAD_DIM, cs] = zero
        qw_ref[HEAD_DIM:128, cs] = qh

    def qpos_of(shape, j):
        return q0 + ((j * cb + lax.broadcasted_iota(jnp.int32, shape, 1)) & (tq - 1))

    kc = kc_ref[0]
    vcT = vcT_ref[0]
    cend = lax.broadcasted_iota(jnp.int32, (nblk, cb), 0) * CMP_STRIDE + (CMP_BLOCK - 1)
    o_cmp, p_sum = [], None
    for j in range(nb):
        s = _dot(kc, qs_ref[0:128, j * cb:(j + 1) * cb])
        s = jnp.where(cend <= qpos_of((nblk, cb), j), s, -jnp.inf)
        m = jnp.max(s, axis=0, keepdims=True)
        m = jnp.where(m == -jnp.inf, 0.0, m)
        e = jnp.exp2(s - m)
        p = e / jnp.maximum(jnp.sum(e, axis=0, keepdims=True), 1e-30)
        o_cmp.append(_dot(vcT, p.astype(BF16)))
        for c0 in range(0, cb, tq):
            ph = p[:, c0:c0 + tq]
            p_sum = ph if p_sum is None else p_sum + ph
    o_cmp = jnp.concatenate(o_cmp, axis=1)

    jj = lax.broadcasted_iota(jnp.int32, (nsel, nblk), 0)
    cc = lax.broadcasted_iota(jnp.int32, (nsel, nblk), 1)
    ov_t = jnp.where((cc * CMP_STRIDE < (jj + 1) * SLC_BLOCK)
                     & (cc * CMP_STRIDE + CMP_BLOCK > jj * SLC_BLOCK), 1.0, 0.0).astype(F32)
    imp = _dot(ov_t, p_sum, precision=lax.Precision.HIGHEST)
    sj = lax.broadcasted_iota(jnp.int32, (nsel, tq), 0)
    qpos_l = q0 + lax.broadcasted_iota(jnp.int32, (nsel, tq), 1)
    cur = qpos_l >> 6
    forced = (sj == 0) | (sj == cur) | (sj == cur - 1)
    valid = sj * SLC_BLOCK <= qpos_l
    imp = jnp.where(forced, FORCE_SCORE, jnp.where(valid, imp, -1.0))
    sel = _topk_select(imp, min(TOPK, nsel))
    bias = jnp.where(sel, 0.0, -(2.0 ** 30)).astype(BF16)
    if nsel < MAX_SEL_BLOCKS:
        bias = jnp.concatenate([bias, jnp.zeros((MAX_SEL_BLOCKS - nsel, tq), BF16)], axis=0)
    for hd in range(N_HEADS):
        qs_ref[128:256, hd * tq:(hd + 1) * tq] = bias

    wlen = WINDOW + tq
    w0 = pl.multiple_of(jnp.maximum(q0 - WINDOW, 0), tq)
    kw = ksw_ref[0, pl.ds(w0, wlen), :]
    vw = vswT_ref[0, :, pl.ds(w0, wlen)]
    kpos_w = w0 + lax.broadcasted_iota(jnp.int32, (wlen, cb), 0)
    o_win = []
    for j in range(nb):
        s = _dot(kw, qw_ref[:, j * cb:(j + 1) * cb])
        dist = qpos_of((wlen, cb), j) - kpos_w
        s = jnp.where((dist >= 0) & (dist < WINDOW), s, -jnp.inf)
        e = jnp.exp2(s - jnp.max(s, axis=0, keepdims=True))
        l_w = jnp.maximum(jnp.sum(e, axis=0, keepdims=True), 1e-30)
        o_win.append(_dot(vw, e.astype(BF16)) / l_w)
    o_win = jnp.concatenate(o_win, axis=1)[HEAD_DIM:128, :]

    def key_tile(k0):
        return jnp.concatenate([ksw_ref[0, pl.ds(k0, tk), :], hot_ref[pl.ds(k0, tk), :]], axis=1)

    def val_tile(k0):
        return vswT_ref[0, :, pl.ds(k0, tk)]

    o_sel = _flash_causal(key_tile, val_tile, qs_ref, m_ref, l_ref, acc_ref, s_ref,
                          q0=q0, tq=tq, tk=tk, cb=cb)[0:HEAD_DIM, :]

    gates = ngT_ref[0]
    out_t = []
    for hd in range(N_HEADS):
        cs = slice(hd * tq, (hd + 1) * tq)
        out_t.append(gates[3 * hd:3 * hd + 1] * o_cmp[:, cs]
                     + gates[3 * hd + 1:3 * hd + 2] * o_sel[:, cs]
                     + gates[3 * hd + 2:3 * hd + 3] * o_win[:, cs])
    o_ref[0] = jnp.concatenate(out_t, axis=0).T.astype(o_ref.dtype)


def _nsa(nqT, kc, vcT, ksw, vswT, hot, ngT, batch, seq):
    tq, tk, cb = NSA_TQ, ATTN_TK, ATTN_CB
    cols = N_HEADS * tq
    nblk = seq // CMP_STRIDE
    assert seq % tk == 0 and (tk % tq == 0 or tq % tk == 0) and cb % tq == 0 and cols % cb == 0
    assert WINDOW % tq == 0 and seq >= WINDOW + tq and (tq & (tq - 1)) == 0
    assert seq // SLC_BLOCK <= MAX_SEL_BLOCKS
    full = lambda b, qi: (b, 0, 0)
    out = pl.pallas_call(
        functools.partial(_nsa_body, tq=tq, tk=tk, cb=cb, seq=seq),
        grid=(batch, seq // tq),
        in_specs=[pl.BlockSpec((1, 256, tq), lambda b, qi: (b, 0, qi)),
                  pl.BlockSpec((1, nblk, 128), full),
                  pl.BlockSpec((1, HEAD_DIM, nblk), full),
                  pl.BlockSpec((1, seq, 128), full),
                  pl.BlockSpec((1, 128, seq), full),
                  pl.BlockSpec((seq, MAX_SEL_BLOCKS), lambda b, qi: (0, 0)),
                  pl.BlockSpec((1, 16, tq), lambda b, qi: (b, 0, qi))],
        out_specs=pl.BlockSpec((1, tq, 256), lambda b, qi: (b, qi, 0)),
        out_shape=jax.ShapeDtypeStruct((batch, seq, 256), BF16),
        scratch_shapes=[pltpu.VMEM((256, cols), BF16),
                        pltpu.VMEM((128, cols), BF16),
                        pltpu.VMEM((1, cols), F32),
                        pltpu.VMEM((1, cols), F32),
                        pltpu.VMEM((128, cols), F32),
                        pltpu.VMEM((tk, cb), F32)],
        compiler_params=_cparams("arbitrary", "arbitrary"),
        name="nsa",
    )(nqT, kc, vcT, ksw.reshape(batch, seq, 128), vswT, hot, ngT)
    return out.reshape(batch * seq, 256)


def _merge_body(x_ref, g_ref, oa_ref, ob_ref, oc_ref, u_ref, vn_ref, ws_ref, bt_ref,
                wg_ref, wbr_ref, wout_ref, o_ref):
    x = x_ref[...]
    tm = x.shape[0]
    h = (_rms_rows(x) * g_ref[...]).astype(BF16)

    r = lax.broadcasted_iota(jnp.int32, (GMLP_CHUNK, GMLP_CHUNK), 0)
    c = lax.broadcasted_iota(jnp.int32, (GMLP_CHUNK, GMLP_CHUNK), 1)
    lane_grp = lax.broadcasted_iota(jnp.int32, (GMLP_CHUNK, W_MIX), 1) // HEAD_DIM
    w_tril = [jnp.where(r >= c, ws_ref[g], 0.0).astype(BF16) for g in range(N_HEADS)]
    od = []
    for ch in range(tm // GMLP_CHUNK):
        sl = slice(ch * GMLP_CHUNK, (ch + 1) * GMLP_CHUNK)
        vch = vn_ref[sl, :]
        sv = jnp.zeros((GMLP_CHUNK, W_MIX), F32)
        for g in range(N_HEADS):
            sv = jnp.where(lane_grp == g, _dot(w_tril[g], vch), sv)
        od.append(u_ref[sl, :] * (sv + bt_ref[...]))
    o_d = jnp.concatenate(od, axis=0).astype(BF16)

    branches = (oa_ref[...], ob_ref[...], oc_ref[...], o_d)
    acc = jnp.zeros_like(x)
    for n in range(4):
        gate = jax.nn.sigmoid(_dot(h, wg_ref[n]))
        acc = acc + gate * _dot(branches[n], wbr_ref[n])
    o_ref[...] = x + _dot(acc.astype(BF16), wout_ref[...])


def _merge(x2, gain, oa, ob, oc, u, vn, w_s, bt, wg, wbr, wout):
    n, d = x2.shape
    tm = MERGE_TM
    row = lambda i: (i, 0)
    c2 = lambda i: (0, 0)
    c3 = lambda i: (0, 0, 0)
    return pl.pallas_call(
        _merge_body,
        grid=(n // tm,),
        in_specs=[pl.BlockSpec((tm, d), row),
                  pl.BlockSpec((1, d), c2),
                  pl.BlockSpec((tm, 256), row),
                  pl.BlockSpec((tm, 256), row),
                  pl.BlockSpec((tm, 256), row),
                  pl.BlockSpec((tm, 256), row),
                  pl.BlockSpec((tm, 256), row),
                  pl.BlockSpec((N_HEADS, GMLP_CHUNK, GMLP_CHUNK), c3),
                  pl.BlockSpec((GMLP_CHUNK, W_MIX), c2),
                  pl.BlockSpec((4, d, d), c3),
                  pl.BlockSpec((4, W_MIX, d), c3),
                  pl.BlockSpec((d, d), c2)],
        out_specs=pl.BlockSpec((tm, d), row),
        out_shape=jax.ShapeDtypeStruct((n, d), F32),
        compiler_params=_cparams("arbitrary"),
        name="merge",
    )(x2, gain.reshape(1, d).astype(F32), oa, ob, oc, u, vn, w_s, bt, wg, wbr, wout)


def _block_diag_mean(width, group):
    idx = jnp.arange(width) // group
    return (idx[:, None] == idx[None, :]).astype(F32) / float(group)


def _stage_proj_weight(w_in_l):
    nkv = 1796
    d = w_in_l.shape[0]
    flog = w_in_l[:, 1536:1540]
    cols = [w_in_l[:, 0:1536],
            w_in_l[:, 1540:1796],
            w_in_l[:, nkv + 128:nkv + 192],
            w_in_l[:, nkv + 256:nkv + 320],
            w_in_l[:, nkv + 192:nkv + 256],
            w_in_l[:, nkv + 320:nkv + 384],
            w_in_l[:, nkv:nkv + 128],
            w_in_l[:, 2192:2704],
            flog, flog, flog,
            jnp.zeros((d, _GATE_COL - 12), w_in_l.dtype),
            w_in_l[:, 2180:2192],
            jnp.zeros((d, 128 - _GATE_COL - 12), w_in_l.dtype)]
    w = jnp.concatenate(cols, axis=1)
    assert w.shape[1] == _C_TOTAL
    return w.astype(BF16), w_in_l[:, 2704:]


def _stage_compress(pe, w1, b1, w2, b2):
    hid = w1.shape[-1]
    half = CMP_STRIDE
    pe_ext = jnp.concatenate([pe[0], pe[1]], axis=-1)
    pe_ext = jnp.stack([pe_ext[:half].reshape(-1), pe_ext[half:].reshape(-1)])
    w1k = w1[0].reshape(CMP_BLOCK, HEAD_DIM, hid)
    w1v = w1[1].reshape(CMP_BLOCK, HEAD_DIM, hid)
    zk = jnp.zeros_like(w1k)
    top = jnp.concatenate([w1k, zk], axis=-1)
    bot = jnp.concatenate([zk, w1v], axis=-1)
    w1e = jnp.concatenate([top, bot], axis=1)
    w1a = w1e[:half].reshape(half * 128, 2 * hid).astype(BF16)
    w1b = w1e[half:].reshape(half * 128, 2 * hid).astype(BF16)
    b1e = jnp.concatenate([b1[0], b1[1]]).reshape(1, 2 * hid)
    z = jnp.zeros((hid, HEAD_DIM), F32)
    w2e = jnp.concatenate([jnp.concatenate([w2[0], z, z, z], axis=1),
                           jnp.concatenate([z, z, w2[1], z], axis=1)], axis=0).astype(BF16)
    zb = jnp.zeros((HEAD_DIM,), F32)
    b2e = jnp.concatenate([b2[0], zb, b2[1], zb]).reshape(1, 256)
    return pe_ext, w1a, w1b, b1e, w2e, b2e


def kernel(x, ffn1_norm, ffn1_w_in, ffn1_w_out, mix_norm, w_in, diff_q_gain, diff_k_gain, diff_lambda, fox_q_gain, fox_k_gain, fox_f_bias, nsa_q_gain, nsa_k_gain, nsa_cmp_pe, nsa_phi_w1, nsa_phi_b1, nsa_phi_w2, nsa_phi_b2, gmlp_v_gain, gmlp_w_s, gmlp_b_s, w_branch, w_out, ffn2_norm, ffn2_w_in, ffn2_w_out):
    batch, seq, d = x.shape
    depth = w_in.shape[0]
    n = batch * seq
    g32 = _block_diag_mean(256, DIFF_QK_DIM)
    g64 = _block_diag_mean(256, HEAD_DIM)
    g64s = g64[:128, :128]
    hot = (jnp.arange(seq)[:, None] // SLC_BLOCK == jnp.arange(MAX_SEL_BLOCKS)[None, :]).astype(BF16)
    x2 = x.reshape(n, d).astype(F32)

    for l in range(depth):
        x2 = _ffn(x2, ffn1_norm[l], ffn1_w_in[l], ffn1_w_out[l])

        w_cat, w_gate = _stage_proj_weight(w_in[l])
        gains = jnp.stack([
            jnp.tile(diff_q_gain[l], 8), jnp.tile(diff_k_gain[l], 8),
            jnp.tile(fox_q_gain[l], 4), jnp.tile(fox_k_gain[l], 4),
            jnp.tile(nsa_q_gain[l], 4), jnp.tile(nsa_k_gain[l], 4),
            gmlp_v_gain[l], jnp.zeros((256,), F32)]).astype(F32)
        fb = fox_f_bias[l].astype(F32)
        fbias = jnp.concatenate([fb, fb, fb, jnp.zeros((116,), F32)]).reshape(1, 128)
        (aqT, ak, avT, fqT, fk, fvT, faug, cT, nqT, ksw, vswT, kvc, u, vn, ngT) = _proj(
            x2, mix_norm[l], w_cat, g32, g64, gains, fbias, batch, seq)

        lam_init = 0.8 - 0.6 * math.exp(-0.3 * l)
        o_a = _attn("diff", aqT, ak, avT, (diff_lambda[l].astype(F32), g64s), batch, seq, lam_init)
        o_b = _attn("fox", fqT, fk, fvT, (faug, cT), batch, seq)

        pe_ext, w1a, w1b, b1e, w2e, b2e = _stage_compress(
            nsa_cmp_pe[l], nsa_phi_w1[l], nsa_phi_b1[l], nsa_phi_w2[l], nsa_phi_b2[l])
        kgain = jnp.tile(nsa_k_gain[l], 2).reshape(1, 128).astype(F32)
        kc, vcT = _compress(kvc, pe_ext, w1a, w1b, b1e, w2e, b2e, g64s, kgain, batch, seq)
        o_c = _nsa(nqT, kc, vcT, ksw, vswT, hot, ngT, batch, seq)

        bt = jnp.repeat(gmlp_b_s[l].T, HEAD_DIM, axis=1).astype(F32)
        wg = jnp.stack([w_gate[:, i * d:(i + 1) * d] for i in range(4)]).astype(BF16)
        x2 = _merge(x2, mix_norm[l], o_a, o_b, o_c, u, vn, gmlp_w_s[l].astype(F32), bt,
                    wg, w_branch[l].astype(BF16), w_out[l].astype(BF16))

        x2 = _ffn(x2, ffn2_norm[l], ffn2_w_in[l], ffn2_w_out[l])
    return x2.reshape(batch, seq, d).astype(x.dtype)
```

```python
import functools
import math

import jax
import jax.numpy as jnp
from jax import lax
from jax.experimental import pallas as pl
from jax.experimental.pallas import tpu as pltpu

F32 = jnp.float32
BF16 = jnp.bfloat16

HEAD_DIM = 64
N_HEADS = 4
W_MIX = N_HEADS * HEAD_DIM
DIFF_QK_DIM = HEAD_DIM // 2
CMP_BLOCK = 32
CMP_STRIDE = 16
SLC_BLOCK = 64
TOPK = 16
WINDOW = 512
FORCE_SCORE = 1.0e4
GMLP_CHUNK = 128
RMS_EPS = 1e-6
NEG = -1.0e30
LOG2E = math.log2(math.e)
INT_MIN = -2 ** 31

VMEM_LIMIT_BYTES = 56 * 1024 * 1024

FFN_TM = 512
FFN_TF = 256
PROJ_TM = 512
MERGE_TM = 512
ATTN_COLS = 2048
ATTN_CB = 512
ATTN_TK = 1024
NSA_TQ = 512
MAX_SEL_BLOCKS = 128


def _dot(a, b):
    return jnp.dot(a, b, preferred_element_type=F32)


def _split3(c):
    hi = c.astype(BF16)
    r1 = c - hi.astype(F32)
    mid = r1.astype(BF16)
    lo = (r1 - mid.astype(F32)).astype(BF16)
    return hi, mid, lo


def _dot_exact_lhs(a, x, terms):
    return sum(_dot(a, part) for part in _split3(x)[:terms])


def _dot_exact_rhs(x, a, terms):
    return sum(_dot(part, a) for part in _split3(x)[:terms])


def _rms_rows(x):
    ms = jnp.mean(x * x, axis=-1, keepdims=True)
    return x * lax.rsqrt(ms + RMS_EPS)


def _group_norm(z, gmat, gain):
    ms = _dot_exact_rhs(z * z, gmat, 2)
    return z * lax.rsqrt(ms + RMS_EPS) * gain


def _cparams(*sem):
    return pltpu.CompilerParams(dimension_semantics=sem, vmem_limit_bytes=VMEM_LIMIT_BYTES)


def _ffn_body(x_ref, g_ref, wa_ref, wb_ref, wo_ref, o_ref, *, n_chunks, tf):
    x = x_ref[...]
    h = (_rms_rows(x) * g_ref[...]).astype(BF16)
    acc = jnp.zeros_like(x)
    for c in range(n_chunks):
        a = _dot(h, wa_ref[:, c * tf:(c + 1) * tf])
        b = _dot(h, wb_ref[:, c * tf:(c + 1) * tf])
        g = (a * jax.nn.sigmoid(a) * b).astype(BF16)
        acc = acc + _dot(g, wo_ref[c * tf:(c + 1) * tf, :])
    o_ref[...] = x + 0.5 * acc


def _ffn(x2, gain, w_in, w_out):
    n, d = x2.shape
    d_ff = w_out.shape[0]
    wa = w_in[:, :d_ff].astype(BF16)
    wb = w_in[:, d_ff:].astype(BF16)
    wo = w_out.astype(BF16)
    tm, tf = FFN_TM, FFN_TF
    assert n % tm == 0 and d_ff % tf == 0
    const = lambda i: (0, 0)
    return pl.pallas_call(
        functools.partial(_ffn_body, n_chunks=d_ff // tf, tf=tf),
        grid=(n // tm,),
        in_specs=[pl.BlockSpec((tm, d), lambda i: (i, 0)),
                  pl.BlockSpec((1, d), const),
                  pl.BlockSpec((d, d_ff), const),
                  pl.BlockSpec((d, d_ff), const),
                  pl.BlockSpec((d_ff, d), const)],
        out_specs=pl.BlockSpec((tm, d), lambda i: (i, 0)),
        out_shape=jax.ShapeDtypeStruct((n, d), F32),
        compiler_params=_cparams("arbitrary"),
        name="ffn",
    )(x2, gain.reshape(1, d).astype(F32), wa, wb, wo)


_C_AQ, _C_AK, _C_AV = 0, 256, 512
_C_FQ, _C_FK, _C_FV = 768, 1024, 1280
_C_NQ = 1536
_C_KSW = 1792
_C_VSW = 1920
_C_KVC = 2048
_C_UV = 2176
_C_SMALL = 2688
_C_TOTAL = 2816
_GATE_COL = 16
_AUG_HI, _AUG_MID, _AUG_LO, _AUG_ONE = 0, 4, 8, 12


def _proj_body(x_ref, g_ref, w_ref, g32_ref, g64_ref, gains_ref, fbias_ref,
               aqT_ref, ak_ref, avT_ref, fqT_ref, fk_ref, fvT_ref, faug_ref, cT_ref,
               nqT_ref, ksw_ref, vswT_ref, kvc_ref, u_ref, vn_ref, ngT_ref,
               carry_ref, *, tiles_per_seq):
    i = pl.program_id(0)
    x = x_ref[...]
    tm = x.shape[0]
    h = (_rms_rows(x) * g_ref[...]).astype(BF16)
    z = _dot(h, w_ref[...])
    g32 = g32_ref[...]
    g64 = g64_ref[...]
    gains = gains_ref[...]

    def sec(c0, w=256):
        return z[:, c0:c0 + w]

    aq = _group_norm(sec(_C_AQ), g32, gains[0:1]) * (DIFF_QK_DIM ** -0.5 * LOG2E)
    aqT_ref[0] = aq.T.astype(BF16)
    ak_ref[...] = _group_norm(sec(_C_AK), g32, gains[1:2]).astype(BF16)
    avT_ref[0] = sec(_C_AV).T.astype(BF16)
    fq = _group_norm(sec(_C_FQ), g64, gains[2:3]) * (HEAD_DIM ** -0.5 * LOG2E)
    fqT_ref[0] = fq.T.astype(BF16)
    fk_ref[...] = _group_norm(sec(_C_FK), g64, gains[3:4]).astype(BF16)
    fvT_ref[0] = sec(_C_FV).T.astype(BF16)
    nq = _group_norm(sec(_C_NQ), g64, gains[4:5]) * (HEAD_DIM ** -0.5 * LOG2E)
    nqT_ref[0] = nq.T.astype(BF16)
    ksw_ref[...] = _group_norm(sec(_C_KSW, 128), g64[:128, :128], gains[5:6, :128]).astype(BF16)
    vswT_ref[0] = sec(_C_VSW, 128).T.astype(BF16)
    kvc_ref[...] = sec(_C_KVC, 128)
    guv = jax.nn.gelu(sec(_C_UV, 512))
    u_ref[...] = guv[:, :256]
    vn_ref[...] = _group_norm(guv[:, 256:], g64, gains[6:7]).astype(BF16)
    small = sec(_C_SMALL, 128)
    ngT_ref[0] = jax.nn.sigmoid(small).T[_GATE_COL:_GATE_COL + 16, :]
    t = small + fbias_ref[...]
    logf = jnp.minimum(t, 0.0) - jnp.log(1.0 + jnp.exp(-jnp.abs(t)))

    @pl.when(i % tiles_per_seq == 0)
    def _():
        carry_ref[...] = jnp.zeros_like(carry_ref)

    row = lax.broadcasted_iota(jnp.int32, (tm, tm), 0)
    col = lax.broadcasted_iota(jnp.int32, (tm, tm), 1)
    tril = jnp.where(row >= col, 1.0, 0.0).astype(BF16)
    csum = _dot_exact_lhs(tril, logf, 3) + carry_ref[0:1, :]
    carry_ref[...] = jnp.broadcast_to(csum[tm - 1:tm, :], carry_ref.shape)
    c2 = csum * LOG2E
    cT_ref[0] = c2.T[0:8, :]
    hi, mid, lo = _split3(c2)
    lane = lax.broadcasted_iota(jnp.int32, (tm, 128), 1)
    one = jnp.where(lane < _AUG_ONE + 3, 1.0, 0.0).astype(BF16)
    faug_ref[...] = jnp.where(lane < _AUG_MID, hi,
                              jnp.where(lane < _AUG_LO, mid, jnp.where(lane < _AUG_ONE, lo, one)))


def _proj(x2, gain, w_cat, g32, g64, gains, fbias, batch, seq):
    n, d = x2.shape
    tm = PROJ_TM
    assert seq % tm == 0
    tps = seq // tm
    nt = n // tm
    const = lambda i: (0, 0)
    row = lambda i: (i, 0)
    trn = lambda i: (i // tps, 0, i % tps)
    nat = lambda w, dt: (jax.ShapeDtypeStruct((n, w), dt), pl.BlockSpec((tm, w), row))
    tr = lambda r, dt: (jax.ShapeDtypeStruct((batch, r, seq), dt), pl.BlockSpec((1, r, tm), trn))
    outs = [
        tr(256, BF16), nat(256, BF16), tr(256, BF16),
        tr(256, BF16), nat(256, BF16), tr(256, BF16),
        nat(128, BF16), tr(8, F32),
        tr(256, BF16), nat(128, BF16), tr(128, BF16),
        nat(128, F32), nat(256, F32), nat(256, BF16), tr(16, F32),
    ]
    return pl.pallas_call(
        functools.partial(_proj_body, tiles_per_seq=tps),
        grid=(nt,),
        in_specs=[pl.BlockSpec((tm, d), row),
                  pl.BlockSpec((1, d), const),
                  pl.BlockSpec((d, _C_TOTAL), const),
                  pl.BlockSpec((256, 256), const),
                  pl.BlockSpec((256, 256), const),
                  pl.BlockSpec((8, 256), const),
                  pl.BlockSpec((1, 128), const)],
        out_specs=[o[1] for o in outs],
        out_shape=[o[0] for o in outs],
        scratch_shapes=[pltpu.VMEM((8, 128), F32)],
        compiler_params=_cparams("arbitrary"),
        name="proj",
    )(x2, gain.reshape(1, d).astype(F32), w_cat, g32, g64, gains, fbias)


def _flash_causal(key_tile, val_tile, qs_ref, m_ref, l_ref, acc_ref, s_ref, *, q0, tq, tk, cb):
    cols = qs_ref.shape[1]
    nb = cols // cb
    n_full = q0 // tk
    m_ref[...] = jnp.full_like(m_ref, NEG)
    l_ref[...] = jnp.zeros_like(l_ref)
    acc_ref[...] = jnp.zeros_like(acc_ref)
    krow = lax.broadcasted_iota(jnp.int32, (tk, cb), 0)
    kcol = lax.broadcasted_iota(jnp.int32, (tk, cb), 1)
    row_minus_q = [krow - ((j * cb + kcol) & (tq - 1)) for j in range(nb)]
    no_mask = jnp.int32(2 ** 30)

    def scores(kk, j, thr):
        s = _dot(kk, qs_ref[:, j * cb:(j + 1) * cb])
        if thr is not None:
            s = jnp.where(row_minus_q[j] <= thr, s, NEG)
        return s

    s_ref[...] = scores(key_tile(0), 0, jnp.where(n_full == 0, q0, no_mask))

    def step(ki, masked, final):
        k0 = pl.multiple_of(ki * tk, tk)
        kk = key_tile(k0)
        vt = val_tile(k0)
        s = s_ref[...]
        for j in range(nb):
            cs = slice(j * cb, (j + 1) * cb)
            s_next = None
            if j + 1 < nb:
                s_next = scores(kk, j + 1, q0 - k0 if masked else None)
            elif not final:
                k1 = pl.multiple_of(k0 + tk, tk)
                thr = q0 - k1 if masked else jnp.where(ki + 1 == n_full, q0 - k1, no_mask)
                s_ref[...] = scores(key_tile(k1), 0, thr)
            m_old = m_ref[:, cs]
            m_new = jnp.maximum(m_old, jnp.max(s, axis=0, keepdims=True))
            alpha = jnp.exp2(m_old - m_new)
            p = jnp.exp2(s - m_new)
            l_ref[:, cs] = alpha * l_ref[:, cs] + jnp.sum(p, axis=0, keepdims=True)
            acc_ref[:, cs] = alpha * acc_ref[:, cs] + _dot(vt, p.astype(BF16))
            m_ref[:, cs] = m_new
            s = s_next

    def loop_body(ki, carry):
        step(ki, False, False)
        return carry

    lax.fori_loop(0, n_full, loop_body, 0)
    n_diag = max(tq // tk, 1)
    for dt in range(n_diag):
        step(n_full + dt, True, dt == n_diag - 1)
    return acc_ref[...] / jnp.maximum(l_ref[...], 1e-30)


def _attn_body(*refs, mode, tq, tk, cb, lam_init):
    if mode == "fox":
        qT_ref, k_ref, vT_ref, aug_ref, cT_ref, o_ref, qs_ref, m_ref, l_ref, acc_ref, s_ref = refs
    else:
        qT_ref, k_ref, vT_ref, lam_ref, g64_ref, o_ref, qs_ref, m_ref, l_ref, acc_ref, s_ref = refs
    nv = 2 if mode == "fox" else 4
    wv = 128 // nv
    hp = pl.program_id(1)
    qi = pl.program_id(2)
    q0 = qi * tq

    qT = qT_ref[0]
    rowi = lax.broadcasted_iota(jnp.int32, (128, tq), 0)
    zero = jnp.zeros_like(qT)
    for vh in range(nv):
        keep = (rowi >= vh * wv) & (rowi < (vh + 1) * wv)
        qs_ref[0:128, vh * tq:(vh + 1) * tq] = jnp.where(keep, qT, zero)
        if mode == "fox":
            hd = 2 * hp + vh
            hi, mid, lo = _split3(cT_ref[0, pl.ds(hd, 1), :])
            neg = (rowi == _AUG_HI + hd) | (rowi == _AUG_MID + hd) | (rowi == _AUG_LO + hd)
            aug = jnp.where(neg, -1.0, 0.0).astype(BF16)
            aug = jnp.where(rowi == _AUG_ONE, hi, aug)
            aug = jnp.where(rowi == _AUG_ONE + 1, mid, aug)
            aug = jnp.where(rowi == _AUG_ONE + 2, lo, aug)
            qs_ref[128:256, vh * tq:(vh + 1) * tq] = aug

    def key_tile(k0):
        kk = k_ref[0, pl.ds(k0, tk), :]
        if mode == "fox":
            kk = jnp.concatenate([kk, aug_ref[0, pl.ds(k0, tk), :]], axis=1)
        return kk

    def val_tile(k0):
        return vT_ref[0, :, pl.ds(k0, tk)]

    o = _flash_causal(key_tile, val_tile, qs_ref, m_ref, l_ref, acc_ref, s_ref,
                      q0=q0, tq=tq, tk=tk, cb=cb)
    lo_rows = rowi < HEAD_DIM
    if mode == "fox":
        out_t = jnp.where(lo_rows, o[:, 0:tq], o[:, tq:2 * tq])
    else:
        lp = lam_ref[...]
        lam = (jnp.exp(jnp.sum(lp[0:1] * lp[1:2], axis=-1, keepdims=True))
               - jnp.exp(jnp.sum(lp[2:3] * lp[3:4], axis=-1, keepdims=True)) + lam_init)
        d0 = o[:, 0:tq] - lam * o[:, tq:2 * tq]
        d1 = o[:, 2 * tq:3 * tq] - lam * o[:, 3 * tq:4 * tq]
        dd = jnp.where(lo_rows, d0, d1)
        ms = _dot_exact_lhs(g64_ref[...], dd * dd, 2)
        out_t = dd * lax.rsqrt(ms + RMS_EPS) * (1.0 - lam_init)
    o_ref[0] = out_t.T.astype(o_ref.dtype)


def _attn(mode, qT, k, vT, extra, batch, seq, lam_init=0.0):
    nv = 2 if mode == "fox" else 4
    tq = ATTN_COLS // nv
    tk, cb = ATTN_TK, ATTN_CB
    assert (tk % tq == 0 or tq % tk == 0) and seq % tk == 0 and (tq % cb == 0 or cb % tq == 0)
    assert (tq & (tq - 1)) == 0
    k3 = k.reshape(batch, seq, 256)
    in_specs = [pl.BlockSpec((1, 128, tq), lambda b, hp, qi: (b, hp, qi)),
                pl.BlockSpec((1, seq, 128), lambda b, hp, qi: (b, 0, hp)),
                pl.BlockSpec((1, 128, seq), lambda b, hp, qi: (b, hp, 0))]
    if mode == "fox":
        faug, cT = extra
        args = (qT, k3, vT, faug.reshape(batch, seq, 128), cT)
        in_specs += [pl.BlockSpec((1, seq, 128), lambda b, hp, qi: (b, 0, 0)),
                     pl.BlockSpec((1, 8, tq), lambda b, hp, qi: (b, 0, qi))]
        depth = 256
    else:
        lam_p, g64s = extra
        args = (qT, k3, vT, lam_p, g64s)
        in_specs += [pl.BlockSpec((4, DIFF_QK_DIM), lambda b, hp, qi: (0, 0)),
                     pl.BlockSpec((128, 128), lambda b, hp, qi: (0, 0))]
        depth = 128
    cols = nv * tq
    out = pl.pallas_call(
        functools.partial(_attn_body, mode=mode, tq=tq, tk=tk, cb=cb, lam_init=lam_init),
        grid=(batch, 2, seq // tq),
        in_specs=in_specs,
        out_specs=pl.BlockSpec((1, tq, 128), lambda b, hp, qi: (b, qi, hp)),
        out_shape=jax.ShapeDtypeStruct((batch, seq, 256), BF16),
        scratch_shapes=[pltpu.VMEM((depth, cols), BF16),
                        pltpu.VMEM((1, cols), F32),
                        pltpu.VMEM((1, cols), F32),
                        pltpu.VMEM((128, cols), F32),
                        pltpu.VMEM((tk, cb), F32)],
        compiler_params=_cparams("arbitrary", "arbitrary", "arbitrary"),
        name="attn_" + mode,
    )(*args)
    return out.reshape(batch * seq, 256)


def _compress_body(x_ref, pe_ref, w1a_ref, w1b_ref, b1_ref, w2_ref, b2_ref, g64_ref, kg_ref,
                   kc_ref, vcT_ref):
    x = x_ref[0]
    nblk = x.shape[0]
    a = _dot((x + pe_ref[0:1, :]).astype(BF16), w1a_ref[...])
    b = _dot((x + pe_ref[1:2, :]).astype(BF16), w1b_ref[...])
    pre = a + pltpu.roll(b, nblk - 1, 0) + b1_ref[...]
    hid = jax.nn.gelu(pre).astype(BF16)
    out = _dot(hid, w2_ref[...]) + b2_ref[...]
    kc_ref[0] = _group_norm(out[:, 0:128], g64_ref[...], kg_ref[...]).astype(BF16)
    vcT_ref[0] = out[:, 128:256].T[0:HEAD_DIM, :].astype(BF16)


def _compress(kvc, pe_ext, w1a, w1b, b1, w2, b2, g64s, kgain, batch, seq):
    nblk = seq // CMP_STRIDE
    x = kvc.reshape(batch, nblk, CMP_STRIDE * 128)
    const = lambda b: (0, 0)
    return pl.pallas_call(
        _compress_body,
        grid=(batch,),
        in_specs=[pl.BlockSpec((1, nblk, CMP_STRIDE * 128), lambda b: (b, 0, 0)),
                  pl.BlockSpec((2, CMP_STRIDE * 128), const),
                  pl.BlockSpec(w1a.shape, const),
                  pl.BlockSpec(w1b.shape, const),
                  pl.BlockSpec(b1.shape, const),
                  pl.BlockSpec(w2.shape, const),
                  pl.BlockSpec(b2.shape, const),
                  pl.BlockSpec((128, 128), const),
                  pl.BlockSpec((1, 128), const)],
        out_specs=[pl.BlockSpec((1, nblk, 128), lambda b: (b, 0, 0)),
                   pl.BlockSpec((1, HEAD_DIM, nblk), lambda b: (b, 0, 0))],
        out_shape=[jax.ShapeDtypeStruct((batch, nblk, 128), BF16),
                   jax.ShapeDtypeStruct((batch, HEAD_DIM, nblk), BF16)],
        compiler_params=_cparams("arbitrary"),
        name="nsa_compress",
    )(x, pe_ext, w1a, w1b, b1, w2, b2, g64s, kgain)


def _topk_select(imp, k):
    nsel, tq = imp.shape

    def count(mask):
        return jnp.sum(jnp.where(mask, 1.0, 0.0), axis=0, keepdims=True)

    kf = float(k)
    bits = jnp.zeros((1, tq), jnp.int32)
    for bit in range(30, -1, -1):
        cand = bits + (1 << bit)
        ok = count(imp >= lax.bitcast_convert_type(cand, F32)) >= kf
        bits = jnp.where(ok, cand, bits)
    kth = jnp.where(count(imp >= 0.0) >= kf, lax.bitcast_convert_type(bits, F32), -1.0)
    gt = imp > kth
    eq = imp == kth
    need = kf - count(gt)
    r = lax.broadcasted_iota(jnp.int32, (nsel, nsel), 0)
    c = lax.broadcasted_iota(jnp.int32, (nsel, nsel), 1)
    tril = jnp.where(r >= c, 1.0, 0.0).astype(BF16)
    prefix_eq = _dot(tril, jnp.where(eq, 1.0, 0.0).astype(BF16))
    return gt | (eq & (prefix_eq <= need))


def _nsa_body(nqT_ref, kc_ref, vcT_ref, ksw_ref, vswT_ref, hot_ref, ngT_ref, o_ref,
              qs_ref, qw_ref, m_ref, l_ref, acc_ref, s_ref, *, tq, tk, cb, seq):
    qi = pl.program_id(1)
    q0 = qi * tq
    cols = N_HEADS * tq
    nblk = kc_ref.shape[1]
    nsel = seq // SLC_BLOCK
    nb = cols // cb

    zero = jnp.zeros((HEAD_DIM, tq), BF16)
    for hd in range(N_HEADS):
        qh = nqT_ref[0, hd * HEAD_DIM:(hd + 1) * HEAD_DIM, :]
        cs = slice(hd * tq, (hd + 1) * tq)
        qs_ref[0:HEAD_DIM, cs] = qh
        qs_ref[HEAD_DIM:128, cs] = zero
        qw_ref[0:HEAD_DIM, cs] = zero
        qw_ref[HEAD_DIM:128, cs] = qh

    def qpos_of(shape, j):
        return q0 + ((j * cb + lax.broadcasted_iota(jnp.int32, shape, 1)) & (tq - 1))

    kc = kc_ref[0]
    vcT = vcT_ref[0]
    cend = lax.broadcasted_iota(jnp.int32, (nblk, cb), 0) * CMP_STRIDE + (CMP_BLOCK - 1)
    o_cmp, p_sum = [], None
    for j in range(nb):
        s = _dot(kc, qs_ref[0:128, j * cb:(j + 1) * cb])
        s = jnp.where(cend <= qpos_of((nblk, cb), j), s, -jnp.inf)
        m = jnp.max(s, axis=0, keepdims=True)
        m = jnp.where(m == -jnp.inf, 0.0, m)
        e = jnp.exp2(s - m)
        p = e / jnp.maximum(jnp.sum(e, axis=0, keepdims=True), 1e-30)
        o_cmp.append(_dot(vcT, p.astype(BF16)))
        for c0 in range(0, cb, tq):
            ph = p[:, c0:c0 + tq]
            p_sum = ph if p_sum is None else p_sum + ph
    o_cmp = jnp.concatenate(o_cmp, axis=1)

    jj = lax.broadcasted_iota(jnp.int32, (nsel, nblk), 0)
    cc = lax.broadcasted_iota(jnp.int32, (nsel, nblk), 1)
    ov_t = jnp.where((cc * CMP_STRIDE < (jj + 1) * SLC_BLOCK)
                     & (cc * CMP_STRIDE + CMP_BLOCK > jj * SLC_BLOCK), 1.0, 0.0).astype(BF16)
    imp = _dot_exact_lhs(ov_t, p_sum, 3)
    sj = lax.broadcasted_iota(jnp.int32, (nsel, tq), 0)
    qpos_l = q0 + lax.broadcasted_iota(jnp.int32, (nsel, tq), 1)
    cur = qpos_l >> 6
    forced = (sj == 0) | (sj == cur) | (sj == cur - 1)
    valid = sj * SLC_BLOCK <= qpos_l
    imp = jnp.where(forced, FORCE_SCORE, jnp.where(valid, imp, -1.0))
    sel = _topk_select(imp, min(TOPK, nsel))
    bias = jnp.where(sel, 0.0, -(2.0 ** 30)).astype(BF16)
    if nsel < MAX_SEL_BLOCKS:
        bias = jnp.concatenate([bias, jnp.zeros((MAX_SEL_BLOCKS - nsel, tq), BF16)], axis=0)
    for hd in range(N_HEADS):
        qs_ref[128:256, hd * tq:(hd + 1) * tq] = bias

    wlen = WINDOW + tq
    w0 = pl.multiple_of(jnp.maximum(q0 - WINDOW, 0), tq)
    kw = ksw_ref[0, pl.ds(w0, wlen), :]
    vw = vswT_ref[0, :, pl.ds(w0, wlen)]
    kpos_w = w0 + lax.broadcasted_iota(jnp.int32, (wlen, cb), 0)
    o_win = []
    for j in range(nb):
        s = _dot(kw, qw_ref[:, j * cb:(j + 1) * cb])
        dist = qpos_of((wlen, cb), j) - kpos_w
        s = jnp.where((dist >= 0) & (dist < WINDOW), s, -jnp.inf)
        e = jnp.exp2(s - jnp.max(s, axis=0, keepdims=True))
        l_w = jnp.maximum(jnp.sum(e, axis=0, keepdims=True), 1e-30)
        o_win.append(_dot(vw, e.astype(BF16)) / l_w)
    o_win = jnp.concatenate(o_win, axis=1)[HEAD_DIM:128, :]

    def key_tile(k0):
        return jnp.concatenate([ksw_ref[0, pl.ds(k0, tk), :], hot_ref[pl.ds(k0, tk), :]], axis=1)

    def val_tile(k0):
        return vswT_ref[0, :, pl.ds(k0, tk)]

    o_sel = _flash_causal(key_tile, val_tile, qs_ref, m_ref, l_ref, acc_ref, s_ref,
                          q0=q0, tq=tq, tk=tk, cb=cb)[0:HEAD_DIM, :]

    gates = ngT_ref[0]
    out_t = []
    for hd in range(N_HEADS):
        cs = slice(hd * tq, (hd + 1) * tq)
        out_t.append(gates[3 * hd:3 * hd + 1] * o_cmp[:, cs]
                     + gates[3 * hd + 1:3 * hd + 2] * o_sel[:, cs]
                     + gates[3 * hd + 2:3 * hd + 3] * o_win[:, cs])
    o_ref[0] = jnp.concatenate(out_t, axis=0).T.astype(o_ref.dtype)


def _nsa(nqT, kc, vcT, ksw, vswT, hot, ngT, batch, seq):
    tq, tk, cb = NSA_TQ, ATTN_TK, ATTN_CB
    cols = N_HEADS * tq
    nblk = seq // CMP_STRIDE
    assert seq % tk == 0 and (tk % tq == 0 or tq % tk == 0) and cb % tq == 0 and cols % cb == 0
    assert WINDOW % tq == 0 and seq >= WINDOW + tq and (tq & (tq - 1)) == 0
    assert seq // SLC_BLOCK <= MAX_SEL_BLOCKS
    full = lambda b, qi: (b, 0, 0)
    out = pl.pallas_call(
        functools.partial(_nsa_body, tq=tq, tk=tk, cb=cb, seq=seq),
        grid=(batch, seq // tq),
        in_specs=[pl.BlockSpec((1, 256, tq), lambda b, qi: (b, 0, qi)),
                  pl.BlockSpec((1, nblk, 128), full),
                  pl.BlockSpec((1, HEAD_DIM, nblk), full),
                  pl.BlockSpec((1, seq, 128), full),
                  pl.BlockSpec((1, 128, seq), full),
                  pl.BlockSpec((seq, MAX_SEL_BLOCKS), lambda b, qi: (0, 0)),
                  pl.BlockSpec((1, 16, tq), lambda b, qi: (b, 0, qi))],
        out_specs=pl.BlockSpec((1, tq, 256), lambda b, qi: (b, qi, 0)),
        out_shape=jax.ShapeDtypeStruct((batch, seq, 256), BF16),
        scratch_shapes=[pltpu.VMEM((256, cols), BF16),
                        pltpu.VMEM((128, cols), BF16),
                        pltpu.VMEM((1, cols), F32),
                        pltpu.VMEM((1, cols), F32),
                        pltpu.VMEM((128, cols), F32),
                        pltpu.VMEM((tk, cb), F32)],
        compiler_params=_cparams("arbitrary", "arbitrary"),
        name="nsa",
    )(nqT, kc, vcT, ksw.reshape(batch, seq, 128), vswT, hot, ngT)
    return out.reshape(batch * seq, 256)


def _merge_body(x_ref, g_ref, oa_ref, ob_ref, oc_ref, u_ref, vn_ref, ws_ref, bt_ref,
                wg_ref, wbr_ref, wout_ref, o_ref):
    x = x_ref[...]
    tm = x.shape[0]
    h = (_rms_rows(x) * g_ref[...]).astype(BF16)

    r = lax.broadcasted_iota(jnp.int32, (GMLP_CHUNK, GMLP_CHUNK), 0)
    c = lax.broadcasted_iota(jnp.int32, (GMLP_CHUNK, GMLP_CHUNK), 1)
    lane_grp = lax.broadcasted_iota(jnp.int32, (GMLP_CHUNK, W_MIX), 1) // HEAD_DIM
    w_tril = [jnp.where(r >= c, ws_ref[g], 0.0).astype(BF16) for g in range(N_HEADS)]
    od = []
    for ch in range(tm // GMLP_CHUNK):
        sl = slice(ch * GMLP_CHUNK, (ch + 1) * GMLP_CHUNK)
        vch = vn_ref[sl, :]
        sv = jnp.zeros((GMLP_CHUNK, W_MIX), F32)
        for g in range(N_HEADS):
            sv = jnp.where(lane_grp == g, _dot(w_tril[g], vch), sv)
        od.append(u_ref[sl, :] * (sv + bt_ref[...]))
    o_d = jnp.concatenate(od, axis=0).astype(BF16)

    branches = (oa_ref[...], ob_ref[...], oc_ref[...], o_d)
    acc = jnp.zeros_like(x)
    for n in range(4):
        gate = jax.nn.sigmoid(_dot(h, wg_ref[n]))
        acc = acc + gate * _dot(branches[n], wbr_ref[n])
    o_ref[...] = x + _dot(acc.astype(BF16), wout_ref[...])


def _merge(x2, gain, oa, ob, oc, u, vn, w_s, bt, wg, wbr, wout):
    n, d = x2.shape
    tm = MERGE_TM
    row = lambda i: (i, 0)
    c2 = lambda i: (0, 0)
    c3 = lambda i: (0, 0, 0)
    return pl.pallas_call(
        _merge_body,
        grid=(n // tm,),
        in_specs=[pl.BlockSpec((tm, d), row),
                  pl.BlockSpec((1, d), c2),
                  pl.BlockSpec((tm, 256), row),
                  pl.BlockSpec((tm, 256), row),
                  pl.BlockSpec((tm, 256), row),
                  pl.BlockSpec((tm, 256), row),
                  pl.BlockSpec((tm, 256), row),
                  pl.BlockSpec((N_HEADS, GMLP_CHUNK, GMLP_CHUNK), c3),
                  pl.BlockSpec((GMLP_CHUNK, W_MIX), c2),
                  pl.BlockSpec((4, d, d), c3),
                  pl.BlockSpec((4, W_MIX, d), c3),
                  pl.BlockSpec((d, d), c2)],
        out_specs=pl.BlockSpec((tm, d), row),
        out_shape=jax.ShapeDtypeStruct((n, d), F32),
        compiler_params=_cparams("arbitrary"),
        name="merge",
    )(x2, gain.reshape(1, d).astype(F32), oa, ob, oc, u, vn, w_s, bt, wg, wbr, wout)


def _block_diag_mean(width, group):
    idx = jnp.arange(width) // group
    return ((idx[:, None] == idx[None, :]).astype(F32) / float(group)).astype(BF16)


def _stage_proj_weight(w_in_l):
    nkv = 1796
    d = w_in_l.shape[0]
    flog = w_in_l[:, 1536:1540]
    cols = [w_in_l[:, 0:1536],
            w_in_l[:, 1540:1796],
            w_in_l[:, nkv + 128:nkv + 192],
            w_in_l[:, nkv + 256:nkv + 320],
            w_in_l[:, nkv + 192:nkv + 256],
            w_in_l[:, nkv + 320:nkv + 384],
            w_in_l[:, nkv:nkv + 128],
            w_in_l[:, 2192:2704],
            flog, flog, flog,
            jnp.zeros((d, _GATE_COL - 12), w_in_l.dtype),
            w_in_l[:, 2180:2192],
            jnp.zeros((d, 128 - _GATE_COL - 12), w_in_l.dtype)]
    w = jnp.concatenate(cols, axis=1)
    assert w.shape[1] == _C_TOTAL
    return w.astype(BF16), w_in_l[:, 2704:]


def _stage_compress(pe, w1, b1, w2, b2):
    hid = w1.shape[-1]
    half = CMP_STRIDE
    pe_ext = jnp.concatenate([pe[0], pe[1]], axis=-1)
    pe_ext = jnp.stack([pe_ext[:half].reshape(-1), pe_ext[half:].reshape(-1)])
    w1k = w1[0].reshape(CMP_BLOCK, HEAD_DIM, hid)
    w1v = w1[1].reshape(CMP_BLOCK, HEAD_DIM, hid)
    zk = jnp.zeros_like(w1k)
    top = jnp.concatenate([w1k, zk], axis=-1)
    bot = jnp.concatenate([zk, w1v], axis=-1)
    w1e = jnp.concatenate([top, bot], axis=1)
    w1a = w1e[:half].reshape(half * 128, 2 * hid).astype(BF16)
    w1b = w1e[half:].reshape(half * 128, 2 * hid).astype(BF16)
    b1e = jnp.concatenate([b1[0], b1[1]]).reshape(1, 2 * hid)
    z = jnp.zeros((hid, HEAD_DIM), F32)
    w2e = jnp.concatenate([jnp.concatenate([w2[0], z, z, z], axis=1),
                           jnp.concatenate([z, z, w2[1], z], axis=1)], axis=0).astype(BF16)
    zb = jnp.zeros((HEAD_DIM,), F32)
    b2e = jnp.concatenate([b2[0], zb, b2[1], zb]).reshape(1, 256)
    return pe_ext, w1a, w1b, b1e, w2e, b2e


def kernel(x, ffn1_norm, ffn1_w_in, ffn1_w_out, mix_norm, w_in, diff_q_gain, diff_k_gain, diff_lambda, fox_q_gain, fox_k_gain, fox_f_bias, nsa_q_gain, nsa_k_gain, nsa_cmp_pe, nsa_phi_w1, nsa_phi_b1, nsa_phi_w2, nsa_phi_b2, gmlp_v_gain, gmlp_w_s, gmlp_b_s, w_branch, w_out, ffn2_norm, ffn2_w_in, ffn2_w_out):
    batch, seq, d = x.shape
    depth = w_in.shape[0]
    n = batch * seq
    g32 = _block_diag_mean(256, DIFF_QK_DIM)
    g64 = _block_diag_mean(256, HEAD_DIM)
    g64s = g64[:128, :128]
    hot = (jnp.arange(seq)[:, None] // SLC_BLOCK == jnp.arange(MAX_SEL_BLOCKS)[None, :]).astype(BF16)
    x2 = x.reshape(n, d).astype(F32)

    for l in range(depth):
        x2 = _ffn(x2, ffn1_norm[l], ffn1_w_in[l], ffn1_w_out[l])

        w_cat, w_gate = _stage_proj_weight(w_in[l])
        gains = jnp.stack([
            jnp.tile(diff_q_gain[l], 8), jnp.tile(diff_k_gain[l], 8),
            jnp.tile(fox_q_gain[l], 4), jnp.tile(fox_k_gain[l], 4),
            jnp.tile(nsa_q_gain[l], 4), jnp.tile(nsa_k_gain[l], 4),
            gmlp_v_gain[l], jnp.zeros((256,), F32)]).astype(F32)
        fb = fox_f_bias[l].astype(F32)
        fbias = jnp.concatenate([fb, fb, fb, jnp.zeros((116,), F32)]).reshape(1, 128)
        (aqT, ak, avT, fqT, fk, fvT, faug, cT, nqT, ksw, vswT, kvc, u, vn, ngT) = _proj(
            x2, mix_norm[l], w_cat, g32, g64, gains, fbias, batch, seq)

        lam_init = 0.8 - 0.6 * math.exp(-0.3 * l)
        o_a = _attn("diff", aqT, ak, avT, (diff_lambda[l].astype(F32), g64s), batch, seq, lam_init)
        o_b = _attn("fox", fqT, fk, fvT, (faug, cT), batch, seq)

        pe_ext, w1a, w1b, b1e, w2e, b2e = _stage_compress(
            nsa_cmp_pe[l], nsa_phi_w1[l], nsa_phi_b1[l], nsa_phi_w2[l], nsa_phi_b2[l])
        kgain = jnp.tile(nsa_k_gain[l], 2).reshape(1, 128).astype(F32)
        kc, vcT = _compress(kvc, pe_ext, w1a, w1b, b1e, w2e, b2e, g64s, kgain, batch, seq)
        o_c = _nsa(nqT, kc, vcT, ksw, vswT, hot, ngT, batch, seq)

        bt = jnp.repeat(gmlp_b_s[l].T, HEAD_DIM, axis=1).astype(F32)
        wg = jnp.stack([w_gate[:, i * d:(i + 1) * d] for i in range(4)]).astype(BF16)
        x2 = _merge(x2, mix_norm[l], o_a, o_b, o_c, u, vn, gmlp_w_s[l].astype(F32), bt,
                    wg, w_branch[l].astype(BF16), w_out[l].astype(BF16))

        x2 = _ffn(x2, ffn2_norm[l], ffn2_w_in[l], ffn2_w_out[l])
    return x2.reshape(batch, seq, d).astype(x.dtype)
```

```python
import functools
import math

import jax
import jax.numpy as jnp
from jax import lax
from jax.experimental import pallas as pl
from jax.experimental.pallas import tpu as pltpu

F32 = jnp.float32
BF16 = jnp.bfloat16

HEAD_DIM = 64
N_HEADS = 4
W_MIX = N_HEADS * HEAD_DIM
DIFF_QK_DIM = HEAD_DIM // 2
CMP_BLOCK = 32
CMP_STRIDE = 16
SLC_BLOCK = 64
TOPK = 16
WINDOW = 512
FORCE_SCORE = 1.0e4
GMLP_CHUNK = 128
RMS_EPS = 1e-6
NEG = -1.0e30
LOG2E = math.log2(math.e)
INT_MIN = -2 ** 31

VMEM_LIMIT_BYTES = 56 * 1024 * 1024

FFN_TM = 512
FFN_TF = 256
PROJ_TM = 512
MERGE_TM = 512
ATTN_COLS = 2048
ATTN_CB = 512
ATTN_TK = 1024
NSA_TQ = 512
MAX_SEL_BLOCKS = 128


def _dot(a, b):
    return jnp.dot(a, b, preferred_element_type=F32)


def _split3(c):
    hi = c.astype(BF16)
    r1 = c - hi.astype(F32)
    mid = r1.astype(BF16)
    lo = (r1 - mid.astype(F32)).astype(BF16)
    return hi, mid, lo


def _dot_exact_lhs(a, x, terms):
    return sum(_dot(a, part) for part in _split3(x)[:terms])


def _dot_exact_rhs(x, a, terms):
    return sum(_dot(part, a) for part in _split3(x)[:terms])


def _rms_rows(x):
    ms = jnp.mean(x * x, axis=-1, keepdims=True)
    return x * lax.rsqrt(ms + RMS_EPS)


def _group_norm(z, gmat, gain):
    ms = _dot_exact_rhs(z * z, gmat, 2)
    return z * lax.rsqrt(ms + RMS_EPS) * gain


def _cparams(*sem):
    return pltpu.CompilerParams(dimension_semantics=sem, vmem_limit_bytes=VMEM_LIMIT_BYTES)


def _ffn_body(x_ref, g_ref, wa_ref, wb_ref, wo_ref, o_ref, *, n_chunks, tf):
    x = x_ref[...]
    h = (_rms_rows(x) * g_ref[...]).astype(BF16)
    acc = jnp.zeros_like(x)
    for c in range(n_chunks):
        a = _dot(h, wa_ref[:, c * tf:(c + 1) * tf])
        b = _dot(h, wb_ref[:, c * tf:(c + 1) * tf])
        g = (a * jax.nn.sigmoid(a) * b).astype(BF16)
        acc = acc + _dot(g, wo_ref[c * tf:(c + 1) * tf, :])
    o_ref[...] = x + 0.5 * acc


def _layer(l, *block):
    return pl.BlockSpec((None,) + block, lambda *_: (l,) + (0,) * len(block))


def _ffn(x2, l, gain, w_in, w_out):
    n, d = x2.shape
    d_ff = w_out.shape[1]
    tm, tf = FFN_TM, FFN_TF
    assert n % tm == 0 and d_ff % tf == 0
    return pl.pallas_call(
        functools.partial(_ffn_body, n_chunks=d_ff // tf, tf=tf),
        grid=(n // tm,),
        in_specs=[pl.BlockSpec((tm, d), lambda i: (i, 0)),
                  _layer(l, 1, d),
                  pl.BlockSpec((None, d, d_ff), lambda i: (l, 0, 0)),
                  pl.BlockSpec((None, d, d_ff), lambda i: (l, 0, 1)),
                  _layer(l, d_ff, d)],
        out_specs=pl.BlockSpec((tm, d), lambda i: (i, 0)),
        out_shape=jax.ShapeDtypeStruct((n, d), F32),
        compiler_params=_cparams("arbitrary"),
        name="ffn",
    )(x2, gain, w_in, w_in, w_out)


_C_AQ, _C_AK, _C_AV = 0, 256, 512
_C_FQ, _C_FK, _C_FV = 768, 1024, 1280
_C_NQ = 1536
_C_KSW = 1792
_C_VSW = 1920
_C_KVC = 2048
_C_UV = 2176
_C_SMALL = 2688
_C_TOTAL = 2816
_GATE_COL = 16
_AUG_HI, _AUG_MID, _AUG_LO, _AUG_ONE = 0, 4, 8, 12


def _proj_body(x_ref, g_ref, w_ref, g32_ref, g64_ref, gains_ref, fbias_ref,
               aqT_ref, ak_ref, avT_ref, fqT_ref, fk_ref, fvT_ref, faug_ref, cT_ref,
               nqT_ref, ksw_ref, vswT_ref, kvc_ref, u_ref, vn_ref, ngT_ref,
               carry_ref, *, tiles_per_seq):
    i = pl.program_id(0)
    x = x_ref[...]
    tm = x.shape[0]
    h = (_rms_rows(x) * g_ref[...]).astype(BF16)
    z = _dot(h, w_ref[...])
    g32 = g32_ref[...]
    g64 = g64_ref[...]
    gains = gains_ref[...]

    def sec(c0, w=256):
        return z[:, c0:c0 + w]

    aq = _group_norm(sec(_C_AQ), g32, gains[0:1]) * (DIFF_QK_DIM ** -0.5 * LOG2E)
    aqT_ref[0] = aq.T.astype(BF16)
    ak_ref[...] = _group_norm(sec(_C_AK), g32, gains[1:2]).astype(BF16)
    avT_ref[0] = sec(_C_AV).T.astype(BF16)
    fq = _group_norm(sec(_C_FQ), g64, gains[2:3]) * (HEAD_DIM ** -0.5 * LOG2E)
    fqT_ref[0] = fq.T.astype(BF16)
    fk_ref[...] = _group_norm(sec(_C_FK), g64, gains[3:4]).astype(BF16)
    fvT_ref[0] = sec(_C_FV).T.astype(BF16)
    nq = _group_norm(sec(_C_NQ), g64, gains[4:5]) * (HEAD_DIM ** -0.5 * LOG2E)
    nqT_ref[0] = nq.T.astype(BF16)
    ksw_ref[...] = _group_norm(sec(_C_KSW, 128), g64[:128, :128], gains[5:6, :128]).astype(BF16)
    vswT_ref[0] = sec(_C_VSW, 128).T.astype(BF16)
    kvc_ref[...] = sec(_C_KVC, 128)
    guv = jax.nn.gelu(sec(_C_UV, 512))
    u_ref[...] = guv[:, :256]
    vn_ref[...] = _group_norm(guv[:, 256:], g64, gains[6:7]).astype(BF16)
    small = sec(_C_SMALL, 128)
    ngT_ref[0] = jax.nn.sigmoid(small).T[_GATE_COL:_GATE_COL + 16, :]
    t = small + fbias_ref[...]
    logf = jnp.minimum(t, 0.0) - jnp.log(1.0 + jnp.exp(-jnp.abs(t)))

    @pl.when(i % tiles_per_seq == 0)
    def _():
        carry_ref[...] = jnp.zeros_like(carry_ref)

    row = lax.broadcasted_iota(jnp.int32, (tm, tm), 0)
    col = lax.broadcasted_iota(jnp.int32, (tm, tm), 1)
    tril = jnp.where(row >= col, 1.0, 0.0).astype(BF16)
    csum = _dot_exact_lhs(tril, logf, 3) + carry_ref[0:1, :]
    carry_ref[...] = jnp.broadcast_to(csum[tm - 1:tm, :], carry_ref.shape)
    c2 = csum * LOG2E
    cT_ref[0] = c2.T[0:8, :]
    hi, mid, lo = _split3(c2)
    lane = lax.broadcasted_iota(jnp.int32, (tm, 128), 1)
    one = jnp.where(lane < _AUG_ONE + 3, 1.0, 0.0).astype(BF16)
    faug_ref[...] = jnp.where(lane < _AUG_MID, hi,
                              jnp.where(lane < _AUG_LO, mid, jnp.where(lane < _AUG_ONE, lo, one)))


def _proj(x2, l, gain, w_cat, g32, g64, gains, fbias, batch, seq):
    n, d = x2.shape
    tm = PROJ_TM
    assert seq % tm == 0
    tps = seq // tm
    nt = n // tm
    const = lambda i: (0, 0)
    row = lambda i: (i, 0)
    trn = lambda i: (i // tps, 0, i % tps)
    nat = lambda w, dt: (jax.ShapeDtypeStruct((n, w), dt), pl.BlockSpec((tm, w), row))
    tr = lambda r, dt: (jax.ShapeDtypeStruct((batch, r, seq), dt), pl.BlockSpec((1, r, tm), trn))
    outs = [
        tr(256, BF16), nat(256, BF16), tr(256, BF16),
        tr(256, BF16), nat(256, BF16), tr(256, BF16),
        nat(128, BF16), tr(8, F32),
        tr(256, BF16), nat(128, BF16), tr(128, BF16),
        nat(128, F32), nat(256, F32), nat(256, BF16), tr(16, F32),
    ]
    return pl.pallas_call(
        functools.partial(_proj_body, tiles_per_seq=tps),
        grid=(nt,),
        in_specs=[pl.BlockSpec((tm, d), row),
                  _layer(l, 1, d),
                  _layer(l, d, _C_TOTAL),
                  pl.BlockSpec((256, 256), const),
                  pl.BlockSpec((256, 256), const),
                  _layer(l, 8, 256),
                  _layer(l, 1, 128)],
        out_specs=[o[1] for o in outs],
        out_shape=[o[0] for o in outs],
        scratch_shapes=[pltpu.VMEM((8, 128), F32)],
        compiler_params=_cparams("arbitrary"),
        name="proj",
    )(x2, gain, w_cat, g32, g64, gains, fbias)


def _flash_causal(key_tile, val_tile, qs_ref, m_ref, l_ref, acc_ref, s_ref, *, q0, tq, tk, cb):
    cols = qs_ref.shape[1]
    nb = cols // cb
    n_full = q0 // tk
    m_ref[...] = jnp.full_like(m_ref, NEG)
    l_ref[...] = jnp.zeros_like(l_ref)
    acc_ref[...] = jnp.zeros_like(acc_ref)
    krow = lax.broadcasted_iota(jnp.int32, (tk, cb), 0)
    kcol = lax.broadcasted_iota(jnp.int32, (tk, cb), 1)
    row_minus_q = [krow - ((j * cb + kcol) & (tq - 1)) for j in range(nb)]
    no_mask = jnp.int32(2 ** 30)

    def scores(kk, j, thr):
        s = _dot(kk, qs_ref[:, j * cb:(j + 1) * cb])
        if thr is not None:
            s = jnp.where(row_minus_q[j] <= thr, s, NEG)
        return s

    s_ref[...] = scores(key_tile(0), 0, jnp.where(n_full == 0, q0, no_mask))

    def step(ki, masked, final):
        k0 = pl.multiple_of(ki * tk, tk)
        kk = key_tile(k0)
        vt = val_tile(k0)
        s = s_ref[...]
        for j in range(nb):
            cs = slice(j * cb, (j + 1) * cb)
            s_next = None
            if j + 1 < nb:
                s_next = scores(kk, j + 1, q0 - k0 if masked else None)
            elif not final:
                k1 = pl.multiple_of(k0 + tk, tk)
                thr = q0 - k1 if masked else jnp.where(ki + 1 == n_full, q0 - k1, no_mask)
                s_ref[...] = scores(key_tile(k1), 0, thr)
            m_old = m_ref[:, cs]
            m_new = jnp.maximum(m_old, jnp.max(s, axis=0, keepdims=True))
            alpha = jnp.exp2(m_old - m_new)
            p = jnp.exp2(s - m_new)
            l_ref[:, cs] = alpha * l_ref[:, cs] + jnp.sum(p, axis=0, keepdims=True)
            acc_ref[:, cs] = alpha * acc_ref[:, cs] + _dot(vt, p.astype(BF16))
            m_ref[:, cs] = m_new
            s = s_next

    def loop_body(ki, carry):
        step(ki, False, False)
        return carry

    lax.fori_loop(0, n_full, loop_body, 0)
    n_diag = max(tq // tk, 1)
    for dt in range(n_diag):
        step(n_full + dt, True, dt == n_diag - 1)
    return acc_ref[...] / jnp.maximum(l_ref[...], 1e-30)


def _attn_body(*refs, mode, tq, tk, cb, lam_init):
    if mode == "fox":
        qT_ref, k_ref, vT_ref, aug_ref, cT_ref, o_ref, qs_ref, m_ref, l_ref, acc_ref, s_ref = refs
    else:
        qT_ref, k_ref, vT_ref, lam_ref, g64_ref, o_ref, qs_ref, m_ref, l_ref, acc_ref, s_ref = refs
    nv = 2 if mode == "fox" else 4
    wv = 128 // nv
    hp = pl.program_id(1)
    qi = pl.program_id(2)
    q0 = qi * tq

    qT = qT_ref[0]
    rowi = lax.broadcasted_iota(jnp.int32, (128, tq), 0)
    zero = jnp.zeros_like(qT)
    for vh in range(nv):
        keep = (rowi >= vh * wv) & (rowi < (vh + 1) * wv)
        qs_ref[0:128, vh * tq:(vh + 1) * tq] = jnp.where(keep, qT, zero)
        if mode == "fox":
            hd = 2 * hp + vh
            hi, mid, lo = _split3(cT_ref[0, pl.ds(hd, 1), :])
            neg = (rowi == _AUG_HI + hd) | (rowi == _AUG_MID + hd) | (rowi == _AUG_LO + hd)
            aug = jnp.where(neg, -1.0, 0.0).astype(BF16)
            aug = jnp.where(rowi == _AUG_ONE, hi, aug)
            aug = jnp.where(rowi == _AUG_ONE + 1, mid, aug)
            aug = jnp.where(rowi == _AUG_ONE + 2, lo, aug)
            qs_ref[128:256, vh * tq:(vh + 1) * tq] = aug

    def key_tile(k0):
        kk = k_ref[0, pl.ds(k0, tk), :]
        if mode == "fox":
            kk = jnp.concatenate([kk, aug_ref[0, pl.ds(k0, tk), :]], axis=1)
        return kk

    def val_tile(k0):
        return vT_ref[0, :, pl.ds(k0, tk)]

    o = _flash_causal(key_tile, val_tile, qs_ref, m_ref, l_ref, acc_ref, s_ref,
                      q0=q0, tq=tq, tk=tk, cb=cb)
    lo_rows = rowi < HEAD_DIM
    if mode == "fox":
        out_t = jnp.where(lo_rows, o[:, 0:tq], o[:, tq:2 * tq])
    else:
        lp = lam_ref[...]
        lam = (jnp.exp(jnp.sum(lp[0:1] * lp[1:2], axis=-1, keepdims=True))
               - jnp.exp(jnp.sum(lp[2:3] * lp[3:4], axis=-1, keepdims=True)) + lam_init)
        d0 = o[:, 0:tq] - lam * o[:, tq:2 * tq]
        d1 = o[:, 2 * tq:3 * tq] - lam * o[:, 3 * tq:4 * tq]
        dd = jnp.where(lo_rows, d0, d1)
        ms = _dot_exact_lhs(g64_ref[...], dd * dd, 2)
        out_t = dd * lax.rsqrt(ms + RMS_EPS) * (1.0 - lam_init)
    o_ref[0] = out_t.T.astype(o_ref.dtype)


def _attn(mode, qT, k, vT, extra, batch, seq, lam_init=0.0):
    nv = 2 if mode == "fox" else 4
    tq = ATTN_COLS // nv
    tk, cb = ATTN_TK, ATTN_CB
    assert (tk % tq == 0 or tq % tk == 0) and seq % tk == 0 and (tq % cb == 0 or cb % tq == 0)
    assert (tq & (tq - 1)) == 0
    k3 = k.reshape(batch, seq, 256)
    in_specs = [pl.BlockSpec((1, 128, tq), lambda b, hp, qi: (b, hp, qi)),
                pl.BlockSpec((1, seq, 128), lambda b, hp, qi: (b, 0, hp)),
                pl.BlockSpec((1, 128, seq), lambda b, hp, qi: (b, hp, 0))]
    if mode == "fox":
        faug, cT = extra
        args = (qT, k3, vT, faug.reshape(batch, seq, 128), cT)
        in_specs += [pl.BlockSpec((1, seq, 128), lambda b, hp, qi: (b, 0, 0)),
                     pl.BlockSpec((1, 8, tq), lambda b, hp, qi: (b, 0, qi))]
        depth = 256
    else:
        l, lam_p, g64s = extra
        args = (qT, k3, vT, lam_p, g64s)
        in_specs += [_layer(l, 4, DIFF_QK_DIM),
                     pl.BlockSpec((128, 128), lambda b, hp, qi: (0, 0))]
        depth = 128
    cols = nv * tq
    out = pl.pallas_call(
        functools.partial(_attn_body, mode=mode, tq=tq, tk=tk, cb=cb, lam_init=lam_init),
        grid=(batch, 2, seq // tq),
        in_specs=in_specs,
        out_specs=pl.BlockSpec((1, tq, 128), lambda b, hp, qi: (b, qi, hp)),
        out_shape=jax.ShapeDtypeStruct((batch, seq, 256), BF16),
        scratch_shapes=[pltpu.VMEM((depth, cols), BF16),
                        pltpu.VMEM((1, cols), F32),
                        pltpu.VMEM((1, cols), F32),
                        pltpu.VMEM((128, cols), F32),
                        pltpu.VMEM((tk, cb), F32)],
        compiler_params=_cparams("arbitrary", "arbitrary", "arbitrary"),
        name="attn_" + mode,
    )(*args)
    return out.reshape(batch * seq, 256)


def _compress_body(x_ref, pe_ref, w1a_ref, w1b_ref, b1_ref, w2_ref, b2_ref, g64_ref, kg_ref,
                   kc_ref, vcT_ref):
    x = x_ref[0]
    nblk = x.shape[0]
    a = _dot((x + pe_ref[0:1, :]).astype(BF16), w1a_ref[...])
    b = _dot((x + pe_ref[1:2, :]).astype(BF16), w1b_ref[...])
    pre = a + pltpu.roll(b, nblk - 1, 0) + b1_ref[...]
    hid = jax.nn.gelu(pre).astype(BF16)
    out = _dot(hid, w2_ref[...]) + b2_ref[...]
    kc_ref[0] = _group_norm(out[:, 0:128], g64_ref[...], kg_ref[...]).astype(BF16)
    vcT_ref[0] = out[:, 128:256].T[0:HEAD_DIM, :].astype(BF16)


def _compress(kvc, l, pe_ext, w1a, w1b, b1, w2, b2, g64s, kgain, batch, seq):
    nblk = seq // CMP_STRIDE
    x = kvc.reshape(batch, nblk, CMP_STRIDE * 128)
    const = lambda b: (0, 0)
    return pl.pallas_call(
        _compress_body,
        grid=(batch,),
        in_specs=[pl.BlockSpec((1, nblk, CMP_STRIDE * 128), lambda b: (b, 0, 0)),
                  _layer(l, 2, CMP_STRIDE * 128),
                  _layer(l, *w1a.shape[1:]),
                  _layer(l, *w1b.shape[1:]),
                  _layer(l, *b1.shape[1:]),
                  _layer(l, *w2.shape[1:]),
                  _layer(l, *b2.shape[1:]),
                  pl.BlockSpec((128, 128), const),
                  _layer(l, 1, 128)],
        out_specs=[pl.BlockSpec((1, nblk, 128), lambda b: (b, 0, 0)),
                   pl.BlockSpec((1, HEAD_DIM, nblk), lambda b: (b, 0, 0))],
        out_shape=[jax.ShapeDtypeStruct((batch, nblk, 128), BF16),
                   jax.ShapeDtypeStruct((batch, HEAD_DIM, nblk), BF16)],
        compiler_params=_cparams("arbitrary"),
        name="nsa_compress",
    )(x, pe_ext, w1a, w1b, b1, w2, b2, g64s, kgain)


def _topk_select(imp, k):
    nsel, tq = imp.shape

    def count(mask):
        return jnp.sum(jnp.where(mask, 1.0, 0.0), axis=0, keepdims=True)

    kf = float(k)
    bits = jnp.zeros((1, tq), jnp.int32)
    for bit in range(30, -1, -1):
        cand = bits + (1 << bit)
        ok = count(imp >= lax.bitcast_convert_type(cand, F32)) >= kf
        bits = jnp.where(ok, cand, bits)
    kth = jnp.where(count(imp >= 0.0) >= kf, lax.bitcast_convert_type(bits, F32), -1.0)
    gt = imp > kth
    eq = imp == kth
    need = kf - count(gt)
    r = lax.broadcasted_iota(jnp.int32, (nsel, nsel), 0)
    c = lax.broadcasted_iota(jnp.int32, (nsel, nsel), 1)
    tril = jnp.where(r >= c, 1.0, 0.0).astype(BF16)
    prefix_eq = _dot(tril, jnp.where(eq, 1.0, 0.0).astype(BF16))
    return gt | (eq & (prefix_eq <= need))


def _nsa_body(nqT_ref, kc_ref, vcT_ref, ksw_ref, vswT_ref, hot_ref, ngT_ref, o_ref,
              qs_ref, qw_ref, m_ref, l_ref, acc_ref, s_ref, *, tq, tk, cb, seq):
    qi = pl.program_id(1)
    q0 = qi * tq
    cols = N_HEADS * tq
    nblk = kc_ref.shape[1]
    nsel = seq // SLC_BLOCK
    nb = cols // cb

    zero = jnp.zeros((HEAD_DIM, tq), BF16)
    for hd in range(N_HEADS):
        qh = nqT_ref[0, hd * HEAD_DIM:(hd + 1) * HEAD_DIM, :]
        cs = slice(hd * tq, (hd + 1) * tq)
        qs_ref[0:HEAD_DIM, cs] = qh
        qs_ref[HEAD_DIM:128, cs] = zero
        qw_ref[0:HEAD_DIM, cs] = zero
        qw_ref[HEAD_DIM:128, cs] = qh

    def qpos_of(shape, j):
        return q0 + ((j * cb + lax.broadcasted_iota(jnp.int32, shape, 1)) & (tq - 1))

    kc = kc_ref[0]
    vcT = vcT_ref[0]
    cend = lax.broadcasted_iota(jnp.int32, (nblk, cb), 0) * CMP_STRIDE + (CMP_BLOCK - 1)
    o_cmp, p_sum = [], None
    for j in range(nb):
        s = _dot(kc, qs_ref[0:128, j * cb:(j + 1) * cb])
        s = jnp.where(cend <= qpos_of((nblk, cb), j), s, -jnp.inf)
        m = jnp.max(s, axis=0, keepdims=True)
        m = jnp.where(m == -jnp.inf, 0.0, m)
        e = jnp.exp2(s - m)
        p = e / jnp.maximum(jnp.sum(e, axis=0, keepdims=True), 1e-30)
        o_cmp.append(_dot(vcT, p.astype(BF16)))
        for c0 in range(0, cb, tq):
            ph = p[:, c0:c0 + tq]
            p_sum = ph if p_sum is None else p_sum + ph
    o_cmp = jnp.concatenate(o_cmp, axis=1)

    jj = lax.broadcasted_iota(jnp.int32, (nsel, nblk), 0)
    cc = lax.broadcasted_iota(jnp.int32, (nsel, nblk), 1)
    ov_t = jnp.where((cc * CMP_STRIDE < (jj + 1) * SLC_BLOCK)
                     & (cc * CMP_STRIDE + CMP_BLOCK > jj * SLC_BLOCK), 1.0, 0.0).astype(BF16)
    imp = _dot_exact_lhs(ov_t, p_sum, 3)
    sj = lax.broadcasted_iota(jnp.int32, (nsel, tq), 0)
    qpos_l = q0 + lax.broadcasted_iota(jnp.int32, (nsel, tq), 1)
    cur = qpos_l >> 6
    forced = (sj == 0) | (sj == cur) | (sj == cur - 1)
    valid = sj * SLC_BLOCK <= qpos_l
    imp = jnp.where(forced, FORCE_SCORE, jnp.where(valid, imp, -1.0))
    sel = _topk_select(imp, min(TOPK, nsel))
    bias = jnp.where(sel, 0.0, -(2.0 ** 30)).astype(BF16)
    if nsel < MAX_SEL_BLOCKS:
        bias = jnp.concatenate([bias, jnp.zeros((MAX_SEL_BLOCKS - nsel, tq), BF16)], axis=0)
    for hd in range(N_HEADS):
        qs_ref[128:256, hd * tq:(hd + 1) * tq] = bias

    wlen = WINDOW + tq
    w0 = pl.multiple_of(jnp.maximum(q0 - WINDOW, 0), tq)
    kw = ksw_ref[0, pl.ds(w0, wlen), :]
    vw = vswT_ref[0, :, pl.ds(w0, wlen)]
    kpos_w = w0 + lax.broadcasted_iota(jnp.int32, (wlen, cb), 0)
    o_win = []
    for j in range(nb):
        s = _dot(kw, qw_ref[:, j * cb:(j + 1) * cb])
        dist = qpos_of((wlen, cb), j) - kpos_w
        s = jnp.where((dist >= 0) & (dist < WINDOW), s, -jnp.inf)
        e = jnp.exp2(s - jnp.max(s, axis=0, keepdims=True))
        l_w = jnp.maximum(jnp.sum(e, axis=0, keepdims=True), 1e-30)
        o_win.append(_dot(vw, e.astype(BF16)) / l_w)
    o_win = jnp.concatenate(o_win, axis=1)[HEAD_DIM:128, :]

    def key_tile(k0):
        return jnp.concatenate([ksw_ref[0, pl.ds(k0, tk), :], hot_ref[pl.ds(k0, tk), :]], axis=1)

    def val_tile(k0):
        return vswT_ref[0, :, pl.ds(k0, tk)]

    o_sel = _flash_causal(key_tile, val_tile, qs_ref, m_ref, l_ref, acc_ref, s_ref,
                          q0=q0, tq=tq, tk=tk, cb=cb)[0:HEAD_DIM, :]

    gates = ngT_ref[0]
    out_t = []
    for hd in range(N_HEADS):
        cs = slice(hd * tq, (hd + 1) * tq)
        out_t.append(gates[3 * hd:3 * hd + 1] * o_cmp[:, cs]
                     + gates[3 * hd + 1:3 * hd + 2] * o_sel[:, cs]
                     + gates[3 * hd + 2:3 * hd + 3] * o_win[:, cs])
    o_ref[0] = jnp.concatenate(out_t, axis=0).T.astype(o_ref.dtype)


def _nsa(nqT, kc, vcT, ksw, vswT, hot, ngT, batch, seq):
    tq, tk, cb = NSA_TQ, ATTN_TK, ATTN_CB
    cols = N_HEADS * tq
    nblk = seq // CMP_STRIDE
    assert seq % tk == 0 and (tk % tq == 0 or tq % tk == 0) and cb % tq == 0 and cols % cb == 0
    assert WINDOW % tq == 0 and seq >= WINDOW + tq and (tq & (tq - 1)) == 0
    assert seq // SLC_BLOCK <= MAX_SEL_BLOCKS
    full = lambda b, qi: (b, 0, 0)
    out = pl.pallas_call(
        functools.partial(_nsa_body, tq=tq, tk=tk, cb=cb, seq=seq),
        grid=(batch, seq // tq),
        in_specs=[pl.BlockSpec((1, 256, tq), lambda b, qi: (b, 0, qi)),
                  pl.BlockSpec((1, nblk, 128), full),
                  pl.BlockSpec((1, HEAD_DIM, nblk), full),
                  pl.BlockSpec((1, seq, 128), full),
                  pl.BlockSpec((1, 128, seq), full),
                  pl.BlockSpec((seq, MAX_SEL_BLOCKS), lambda b, qi: (0, 0)),
                  pl.BlockSpec((1, 16, tq), lambda b, qi: (b, 0, qi))],
        out_specs=pl.BlockSpec((1, tq, 256), lambda b, qi: (b, qi, 0)),
        out_shape=jax.ShapeDtypeStruct((batch, seq, 256), BF16),
        scratch_shapes=[pltpu.VMEM((256, cols), BF16),
                        pltpu.VMEM((128, cols), BF16),
                        pltpu.VMEM((1, cols), F32),
                        pltpu.VMEM((1, cols), F32),
                        pltpu.VMEM((128, cols), F32),
                        pltpu.VMEM((tk, cb), F32)],
        compiler_params=_cparams("arbitrary", "arbitrary"),
        name="nsa",
    )(nqT, kc, vcT, ksw.reshape(batch, seq, 128), vswT, hot, ngT)
    return out.reshape(batch * seq, 256)


def _merge_body(x_ref, g_ref, oa_ref, ob_ref, oc_ref, u_ref, vn_ref, ws_ref, bt_ref,
                wg_ref, wbr_ref, wout_ref, o_ref):
    x = x_ref[...]
    tm, d = x.shape
    h = (_rms_rows(x) * g_ref[...]).astype(BF16)

    r = lax.broadcasted_iota(jnp.int32, (GMLP_CHUNK, GMLP_CHUNK), 0)
    c = lax.broadcasted_iota(jnp.int32, (GMLP_CHUNK, GMLP_CHUNK), 1)
    lane_grp = lax.broadcasted_iota(jnp.int32, (GMLP_CHUNK, W_MIX), 1) // HEAD_DIM
    w_tril = [jnp.where(r >= c, ws_ref[g], 0.0).astype(BF16) for g in range(N_HEADS)]
    od = []
    for ch in range(tm // GMLP_CHUNK):
        sl = slice(ch * GMLP_CHUNK, (ch + 1) * GMLP_CHUNK)
        vch = vn_ref[sl, :]
        sv = jnp.zeros((GMLP_CHUNK, W_MIX), F32)
        for g in range(N_HEADS):
            sv = jnp.where(lane_grp == g, _dot(w_tril[g], vch), sv)
        od.append(u_ref[sl, :] * (sv + bt_ref[...]))
    o_d = jnp.concatenate(od, axis=0).astype(BF16)

    branches = (oa_ref[...], ob_ref[...], oc_ref[...], o_d)
    acc = jnp.zeros_like(x)
    for n in range(4):
        gate = jax.nn.sigmoid(_dot(h, wg_ref[:, n * d:(n + 1) * d]))
        acc = acc + gate * _dot(branches[n], wbr_ref[n])
    o_ref[...] = x + _dot(acc.astype(BF16), wout_ref[...])


def _merge(x2, l, gain, oa, ob, oc, u, vn, w_s, bt, wg, wbr, wout):
    n, d = x2.shape
    tm = MERGE_TM
    row = lambda i: (i, 0)
    return pl.pallas_call(
        _merge_body,
        grid=(n // tm,),
        in_specs=[pl.BlockSpec((tm, d), row),
                  _layer(l, 1, d),
                  pl.BlockSpec((tm, 256), row),
                  pl.BlockSpec((tm, 256), row),
                  pl.BlockSpec((tm, 256), row),
                  pl.BlockSpec((tm, 256), row),
                  pl.BlockSpec((tm, 256), row),
                  _layer(l, N_HEADS, GMLP_CHUNK, GMLP_CHUNK),
                  _layer(l, GMLP_CHUNK, W_MIX),
                  _layer(l, d, 4 * d),
                  _layer(l, 4, W_MIX, d),
                  _layer(l, d, d)],
        out_specs=pl.BlockSpec((tm, d), row),
        out_shape=jax.ShapeDtypeStruct((n, d), F32),
        compiler_params=_cparams("arbitrary"),
        name="merge",
    )(x2, gain, oa, ob, oc, u, vn, w_s, bt, wg, wbr, wout)


def _block_diag_mean(width, group):
    idx = jnp.arange(width) // group
    return ((idx[:, None] == idx[None, :]).astype(F32) / float(group)).astype(BF16)


def _stage_proj_weight(w_in):
    nkv = 1796
    lead = w_in.shape[:-1]
    flog = w_in[..., 1536:1540]
    cols = [w_in[..., 0:1536],
            w_in[..., 1540:1796],
            w_in[..., nkv + 128:nkv + 192],
            w_in[..., nkv + 256:nkv + 320],
            w_in[..., nkv + 192:nkv + 256],
            w_in[..., nkv + 320:nkv + 384],
            w_in[..., nkv:nkv + 128],
            w_in[..., 2192:2704],
            flog, flog, flog,
            jnp.zeros(lead + (_GATE_COL - 12,), w_in.dtype),
            w_in[..., 2180:2192],
            jnp.zeros(lead + (128 - _GATE_COL - 12,), w_in.dtype)]
    w = jnp.concatenate(cols, axis=-1)
    assert w.shape[-1] == _C_TOTAL
    return w.astype(BF16), w_in[..., 2704:].astype(BF16)


def _stage_compress(pe, w1, b1, w2, b2):
    hid = w1.shape[-1]
    half = CMP_STRIDE
    pe_ext = jnp.concatenate([pe[0], pe[1]], axis=-1)
    pe_ext = jnp.stack([pe_ext[:half].reshape(-1), pe_ext[half:].reshape(-1)])
    w1k = w1[0].reshape(CMP_BLOCK, HEAD_DIM, hid)
    w1v = w1[1].reshape(CMP_BLOCK, HEAD_DIM, hid)
    zk = jnp.zeros_like(w1k)
    top = jnp.concatenate([w1k, zk], axis=-1)
    bot = jnp.concatenate([zk, w1v], axis=-1)
    w1e = jnp.concatenate([top, bot], axis=1)
    w1a = w1e[:half].reshape(half * 128, 2 * hid).astype(BF16)
    w1b = w1e[half:].reshape(half * 128, 2 * hid).astype(BF16)
    b1e = jnp.concatenate([b1[0], b1[1]]).reshape(1, 2 * hid)
    z = jnp.zeros((hid, HEAD_DIM), F32)
    w2e = jnp.concatenate([jnp.concatenate([w2[0], z, z, z], axis=1),
                           jnp.concatenate([z, z, w2[1], z], axis=1)], axis=0).astype(BF16)
    zb = jnp.zeros((HEAD_DIM,), F32)
    b2e = jnp.concatenate([b2[0], zb, b2[1], zb]).reshape(1, 256)
    return pe_ext, w1a, w1b, b1e, w2e, b2e


def kernel(x, ffn1_norm, ffn1_w_in, ffn1_w_out, mix_norm, w_in, diff_q_gain, diff_k_gain, diff_lambda, fox_q_gain, fox_k_gain, fox_f_bias, nsa_q_gain, nsa_k_gain, nsa_cmp_pe, nsa_phi_w1, nsa_phi_b1, nsa_phi_w2, nsa_phi_b2, gmlp_v_gain, gmlp_w_s, gmlp_b_s, w_branch, w_out, ffn2_norm, ffn2_w_in, ffn2_w_out):
    batch, seq, d = x.shape
    depth = w_in.shape[0]
    n = batch * seq
    g32 = _block_diag_mean(256, DIFF_QK_DIM)
    g64 = _block_diag_mean(256, HEAD_DIM)
    g64s = g64[:128, :128]
    hot = (jnp.arange(seq)[:, None] // SLC_BLOCK == jnp.arange(MAX_SEL_BLOCKS)[None, :]).astype(BF16)

    norm3 = lambda g: g.reshape(depth, 1, d).astype(F32)
    ffn1_g, ffn2_g, mix_g = norm3(ffn1_norm), norm3(ffn2_norm), norm3(mix_norm)
    ffn1_wi, ffn1_wo = ffn1_w_in.astype(BF16), ffn1_w_out.astype(BF16)
    ffn2_wi, ffn2_wo = ffn2_w_in.astype(BF16), ffn2_w_out.astype(BF16)
    w_cat, w_gate = _stage_proj_weight(w_in)
    tile = lambda g, r: jnp.tile(g.astype(F32), (1, r))
    gains = jnp.stack([tile(diff_q_gain, 8), tile(diff_k_gain, 8), tile(fox_q_gain, 4),
                       tile(fox_k_gain, 4), tile(nsa_q_gain, 4), tile(nsa_k_gain, 4),
                       gmlp_v_gain.astype(F32), jnp.zeros((depth, 256), F32)], axis=1)
    fb = fox_f_bias.astype(F32)
    fbias = jnp.concatenate([fb, fb, fb, jnp.zeros((depth, 116), F32)], axis=1).reshape(depth, 1, 128)
    lam_p = diff_lambda.astype(F32)
    cmp_w = jax.vmap(_stage_compress)(nsa_cmp_pe, nsa_phi_w1, nsa_phi_b1, nsa_phi_w2, nsa_phi_b2)
    kgain = tile(nsa_k_gain, 2).reshape(depth, 1, 128)
    bt = jnp.repeat(jnp.swapaxes(gmlp_b_s, 1, 2), HEAD_DIM, axis=2).astype(F32)
    w_s = gmlp_w_s.astype(F32)
    w_br, w_o = w_branch.astype(BF16), w_out.astype(BF16)

    x2 = x.reshape(n, d).astype(F32)
    for l in range(depth):
        x2 = _ffn(x2, l, ffn1_g, ffn1_wi, ffn1_wo)
        (aqT, ak, avT, fqT, fk, fvT, faug, cT, nqT, ksw, vswT, kvc, u, vn, ngT) = _proj(
            x2, l, mix_g, w_cat, g32, g64, gains, fbias, batch, seq)

        lam_init = 0.8 - 0.6 * math.exp(-0.3 * l)
        o_a = _attn("diff", aqT, ak, avT, (l, lam_p, g64s), batch, seq, lam_init)
        o_b = _attn("fox", fqT, fk, fvT, (faug, cT), batch, seq)
        kc, vcT = _compress(kvc, l, *cmp_w, g64s, kgain, batch, seq)
        o_c = _nsa(nqT, kc, vcT, ksw, vswT, hot, ngT, batch, seq)

        x2 = _merge(x2, l, mix_g, o_a, o_b, o_c, u, vn, w_s, bt, w_gate, w_br, w_o)
        x2 = _ffn(x2, l, ffn2_g, ffn2_wi, ffn2_wo)
    return x2.reshape(batch, seq, d).astype(x.dtype)
```

```python
import functools
import math

import jax
import jax.numpy as jnp
from jax import lax
from jax.experimental import pallas as pl
from jax.experimental.pallas import tpu as pltpu

F32 = jnp.float32
BF16 = jnp.bfloat16

HEAD_DIM = 64
N_HEADS = 4
W_MIX = N_HEADS * HEAD_DIM
DIFF_QK_DIM = HEAD_DIM // 2
CMP_BLOCK = 32
CMP_STRIDE = 16
SLC_BLOCK = 64
TOPK = 16
WINDOW = 512
FORCE_SCORE = 1.0e4
GMLP_CHUNK = 128
RMS_EPS = 1e-6
NEG = -1.0e30
LOG2E = math.log2(math.e)
INT_MIN = -2 ** 31

VMEM_LIMIT_BYTES = 56 * 1024 * 1024

FFN_TM = 512
FFN_TF = 256
PROJ_TM = 512
MERGE_TM = 512
ATTN_COLS = 2048
ATTN_CB = 512
ATTN_TK = 1024
NSA_TQ = 512
MAX_SEL_BLOCKS = 128


def _dot(a, b):
    return jnp.dot(a, b, preferred_element_type=F32)


def _split3(c):
    hi = c.astype(BF16)
    r1 = c - hi.astype(F32)
    mid = r1.astype(BF16)
    lo = (r1 - mid.astype(F32)).astype(BF16)
    return hi, mid, lo


def _dot_exact_lhs(a, x, terms):
    return sum(_dot(a, part) for part in _split3(x)[:terms])


def _dot_exact_rhs(x, a, terms):
    return sum(_dot(part, a) for part in _split3(x)[:terms])


def _rms_rows(x):
    ms = jnp.mean(x * x, axis=-1, keepdims=True)
    return x * lax.rsqrt(ms + RMS_EPS)


def _group_norm(z, gmat, gain):
    ms = _dot_exact_rhs(z * z, gmat, 2)
    return z * lax.rsqrt(ms + RMS_EPS) * gain


def _cparams(*sem):
    return pltpu.CompilerParams(dimension_semantics=sem, vmem_limit_bytes=VMEM_LIMIT_BYTES)


def _ffn_body(x_ref, g_ref, wa_ref, wb_ref, wo_ref, o_ref, *, n_chunks, tf):
    x = x_ref[...]
    h = (_rms_rows(x) * g_ref[...]).astype(BF16)
    acc = jnp.zeros_like(x)
    for c in range(n_chunks):
        a = _dot(h, wa_ref[:, c * tf:(c + 1) * tf])
        b = _dot(h, wb_ref[:, c * tf:(c + 1) * tf])
        g = (a * jax.nn.sigmoid(a) * b).astype(BF16)
        acc = acc + _dot(g, wo_ref[c * tf:(c + 1) * tf, :])
    o_ref[...] = x + 0.5 * acc


def _layer(l, *block):
    return pl.BlockSpec((None,) + block, lambda *_: (l,) + (0,) * len(block))


def _ffn(x2, l, gain, w_in, w_out):
    n, d = x2.shape
    d_ff = w_out.shape[1]
    tm, tf = FFN_TM, FFN_TF
    assert n % tm == 0 and d_ff % tf == 0
    return pl.pallas_call(
        functools.partial(_ffn_body, n_chunks=d_ff // tf, tf=tf),
        grid=(n // tm,),
        in_specs=[pl.BlockSpec((tm, d), lambda i: (i, 0)),
                  _layer(l, 1, d),
                  pl.BlockSpec((None, d, d_ff), lambda i: (l, 0, 0)),
                  pl.BlockSpec((None, d, d_ff), lambda i: (l, 0, 1)),
                  _layer(l, d_ff, d)],
        out_specs=pl.BlockSpec((tm, d), lambda i: (i, 0)),
        out_shape=jax.ShapeDtypeStruct((n, d), F32),
        compiler_params=_cparams("arbitrary"),
        name="ffn",
    )(x2, gain, w_in, w_in, w_out)


_C_AQ, _C_AK, _C_AV = 0, 256, 512
_C_FQ, _C_FK, _C_FV = 768, 1024, 1280
_C_NQ = 1536
_C_KSW = 1792
_C_VSW = 1920
_C_KVC = 2048
_C_UV = 2176
_C_SMALL = 2688
_C_TOTAL = 2816
_GATE_COL = 16
_AUG_HI, _AUG_MID, _AUG_LO, _AUG_ONE = 0, 4, 8, 12


def _proj_body(x_ref, g_ref, w_ref, g32_ref, g64_ref, gains_ref, fbias_ref,
               aqT_ref, ak_ref, avT_ref, fqT_ref, fk_ref, fvT_ref, faug_ref, cT_ref,
               nqT_ref, ksw_ref, vswT_ref, kvc_ref, u_ref, vn_ref, ngT_ref,
               carry_ref, *, tiles_per_seq):
    i = pl.program_id(0)
    x = x_ref[...]
    tm = x.shape[0]
    h = (_rms_rows(x) * g_ref[...]).astype(BF16)
    z = _dot(h, w_ref[...])
    g32 = g32_ref[...]
    g64 = g64_ref[...]
    gains = gains_ref[...]

    def sec(c0, w=256):
        return z[:, c0:c0 + w]

    aq = _group_norm(sec(_C_AQ), g32, gains[0:1]) * (DIFF_QK_DIM ** -0.5 * LOG2E)
    aqT_ref[0] = aq.T.astype(BF16)
    ak_ref[...] = _group_norm(sec(_C_AK), g32, gains[1:2]).astype(BF16)
    avT_ref[0] = sec(_C_AV).T.astype(BF16)
    fq = _group_norm(sec(_C_FQ), g64, gains[2:3]) * (HEAD_DIM ** -0.5 * LOG2E)
    fqT_ref[0] = fq.T.astype(BF16)
    fk_ref[...] = _group_norm(sec(_C_FK), g64, gains[3:4]).astype(BF16)
    fvT_ref[0] = sec(_C_FV).T.astype(BF16)
    nq = _group_norm(sec(_C_NQ), g64, gains[4:5]) * (HEAD_DIM ** -0.5 * LOG2E)
    nqT_ref[0] = nq.T.astype(BF16)
    ksw_ref[...] = _group_norm(sec(_C_KSW, 128), g64[:128, :128], gains[5:6, :128]).astype(BF16)
    vswT_ref[0] = sec(_C_VSW, 128).T.astype(BF16)
    kvc_ref[...] = sec(_C_KVC, 128)
    guv = jax.nn.gelu(sec(_C_UV, 512))
    u_ref[...] = guv[:, :256]
    vn_ref[...] = _group_norm(guv[:, 256:], g64, gains[6:7]).astype(BF16)
    small = sec(_C_SMALL, 128)
    ngT_ref[0] = jax.nn.sigmoid(small).T[_GATE_COL:_GATE_COL + 16, :]
    t = small + fbias_ref[...]
    logf = jnp.minimum(t, 0.0) - jnp.log(1.0 + jnp.exp(-jnp.abs(t)))

    @pl.when(i % tiles_per_seq == 0)
    def _():
        carry_ref[...] = jnp.zeros_like(carry_ref)

    row = lax.broadcasted_iota(jnp.int32, (tm, tm), 0)
    col = lax.broadcasted_iota(jnp.int32, (tm, tm), 1)
    tril = jnp.where(row >= col, 1.0, 0.0).astype(BF16)
    csum = _dot_exact_lhs(tril, logf, 3) + carry_ref[0:1, :]
    carry_ref[...] = jnp.broadcast_to(csum[tm - 1:tm, :], carry_ref.shape)
    c2 = csum * LOG2E
    cT_ref[0] = c2.T[0:8, :]
    hi, mid, lo = _split3(c2)
    lane = lax.broadcasted_iota(jnp.int32, (tm, 128), 1)
    one = jnp.where(lane < _AUG_ONE + 3, 1.0, 0.0).astype(BF16)
    faug_ref[...] = jnp.where(lane < _AUG_MID, hi,
                              jnp.where(lane < _AUG_LO, mid, jnp.where(lane < _AUG_ONE, lo, one)))


def _proj(x2, l, gain, w_cat, g32, g64, gains, fbias, batch, seq):
    n, d = x2.shape
    tm = PROJ_TM
    assert seq % tm == 0
    tps = seq // tm
    nt = n // tm
    const = lambda i: (0, 0)
    row = lambda i: (i, 0)
    trn = lambda i: (i // tps, 0, i % tps)
    nat = lambda w, dt: (jax.ShapeDtypeStruct((n, w), dt), pl.BlockSpec((tm, w), row))
    tr = lambda r, dt: (jax.ShapeDtypeStruct((batch, r, seq), dt), pl.BlockSpec((1, r, tm), trn))
    outs = [
        tr(256, BF16), nat(256, BF16), tr(256, BF16),
        tr(256, BF16), nat(256, BF16), tr(256, BF16),
        nat(128, BF16), tr(8, F32),
        tr(256, BF16), nat(128, BF16), tr(128, BF16),
        nat(128, F32), nat(256, F32), nat(256, BF16), tr(16, F32),
    ]
    return pl.pallas_call(
        functools.partial(_proj_body, tiles_per_seq=tps),
        grid=(nt,),
        in_specs=[pl.BlockSpec((tm, d), row),
                  _layer(l, 1, d),
                  _layer(l, d, _C_TOTAL),
                  pl.BlockSpec((256, 256), const),
                  pl.BlockSpec((256, 256), const),
                  _layer(l, 8, 256),
                  _layer(l, 1, 128)],
        out_specs=[o[1] for o in outs],
        out_shape=[o[0] for o in outs],
        scratch_shapes=[pltpu.VMEM((8, 128), F32)],
        compiler_params=_cparams("arbitrary"),
        name="proj",
    )(x2, gain, w_cat, g32, g64, gains, fbias)


def _flash_causal(key_tile, val_tile, qs_ref, m_ref, l_ref, acc_ref, s_ref, *, q0, tq, tk, cb):
    cols = qs_ref.shape[1]
    nb = cols // cb
    n_full = q0 // tk
    m_ref[...] = jnp.full_like(m_ref, NEG)
    l_ref[...] = jnp.zeros_like(l_ref)
    acc_ref[...] = jnp.zeros_like(acc_ref)
    krow = lax.broadcasted_iota(jnp.int32, (tk, cb), 0)
    kcol = lax.broadcasted_iota(jnp.int32, (tk, cb), 1)
    row_minus_q = [krow - ((j * cb + kcol) & (tq - 1)) for j in range(nb)]
    no_mask = jnp.int32(2 ** 30)

    def scores(kk, j, thr):
        s = _dot(kk, qs_ref[:, j * cb:(j + 1) * cb])
        if thr is not None:
            s = jnp.where(row_minus_q[j] <= thr, s, NEG)
        return s

    s_ref[...] = scores(key_tile(0), 0, jnp.where(n_full == 0, q0, no_mask))

    def step(ki, masked, next_masked, final=False):
        k0 = pl.multiple_of(ki * tk, tk)
        kk = key_tile(k0)
        s = s_ref[...]
        for j in range(nb):
            cs = slice(j * cb, (j + 1) * cb)
            s_next = None
            if j + 1 < nb:
                s_next = scores(kk, j + 1, q0 - k0 if masked else None)
            elif not final:
                k1 = pl.multiple_of(k0 + tk, tk)
                s_ref[...] = scores(key_tile(k1), 0, q0 - k1 if next_masked else None)
            m_old = m_ref[:, cs]
            m_new = jnp.maximum(m_old, jnp.max(s, axis=0, keepdims=True))
            alpha = jnp.exp2(m_old - m_new)
            p = jnp.exp2(s - m_new)
            l_ref[:, cs] = alpha * l_ref[:, cs] + jnp.sum(p, axis=0, keepdims=True)
            acc_ref[:, cs] = alpha * acc_ref[:, cs] + _dot(val_tile(k0, j), p.astype(BF16))
            m_ref[:, cs] = m_new
            s = s_next

    def loop_body(ki, carry):
        step(ki, False, False)
        return carry

    lax.fori_loop(0, n_full - 1, loop_body, 0)

    @pl.when(n_full > 0)
    def _():
        step(n_full - 1, False, True)

    n_diag = max(tq // tk, 1)
    for dt in range(n_diag):
        step(n_full + dt, True, True, final=dt == n_diag - 1)
    return acc_ref[...] / jnp.maximum(l_ref[...], 1e-30)


def _attn_body(*refs, mode, tq, tk, cb, lam_init):
    if mode == "fox":
        qT_ref, k_ref, vT_ref, aug_ref, cT_ref, o_ref, qs_ref, m_ref, l_ref, acc_ref, s_ref = refs
    else:
        qT_ref, k_ref, vT_ref, lam_ref, g64_ref, o_ref, qs_ref, m_ref, l_ref, acc_ref, s_ref = refs
    nv = 2 if mode == "fox" else 4
    wv = 128 // nv
    hp = pl.program_id(1)
    qi = pl.program_id(2)
    q0 = qi * tq

    qT = qT_ref[0]
    rowi = lax.broadcasted_iota(jnp.int32, (128, tq), 0)
    zero = jnp.zeros_like(qT)
    for vh in range(nv):
        keep = (rowi >= vh * wv) & (rowi < (vh + 1) * wv)
        qs_ref[0:128, vh * tq:(vh + 1) * tq] = jnp.where(keep, qT, zero)
        if mode == "fox":
            hd = 2 * hp + vh
            hi, mid, lo = _split3(cT_ref[0, pl.ds(hd, 1), :])
            neg = (rowi == _AUG_HI + hd) | (rowi == _AUG_MID + hd) | (rowi == _AUG_LO + hd)
            aug = jnp.where(neg, -1.0, 0.0).astype(BF16)
            aug = jnp.where(rowi == _AUG_ONE, hi, aug)
            aug = jnp.where(rowi == _AUG_ONE + 1, mid, aug)
            aug = jnp.where(rowi == _AUG_ONE + 2, lo, aug)
            qs_ref[128:256, vh * tq:(vh + 1) * tq] = aug

    def key_tile(k0):
        kk = k_ref[0, pl.ds(k0, tk), :]
        if mode == "fox":
            kk = jnp.concatenate([kk, aug_ref[0, pl.ds(k0, tk), :]], axis=1)
        return kk

    def val_tile(k0, j):
        hd = ((j * cb) // tq) // (nv // 2)
        return vT_ref[0, hd * HEAD_DIM:(hd + 1) * HEAD_DIM, pl.ds(k0, tk)]

    o = _flash_causal(key_tile, val_tile, qs_ref, m_ref, l_ref, acc_ref, s_ref,
                      q0=q0, tq=tq, tk=tk, cb=cb)
    if mode == "fox":
        out_t = jnp.concatenate([o[:, 0:tq], o[:, tq:2 * tq]], axis=0)
    else:
        lp = lam_ref[...]
        lam = (jnp.exp(jnp.sum(lp[0:1] * lp[1:2], axis=-1, keepdims=True))
               - jnp.exp(jnp.sum(lp[2:3] * lp[3:4], axis=-1, keepdims=True)) + lam_init)
        d0 = o[:, 0:tq] - lam * o[:, tq:2 * tq]
        d1 = o[:, 2 * tq:3 * tq] - lam * o[:, 3 * tq:4 * tq]
        dd = jnp.concatenate([d0, d1], axis=0)
        ms = _dot_exact_lhs(g64_ref[...], dd * dd, 2)
        out_t = dd * lax.rsqrt(ms + RMS_EPS) * (1.0 - lam_init)
    o_ref[0] = out_t.T.astype(o_ref.dtype)


def _attn(mode, qT, k, vT, extra, batch, seq, lam_init=0.0):
    nv = 2 if mode == "fox" else 4
    tq = ATTN_COLS // nv
    tk, cb = ATTN_TK, ATTN_CB
    assert (tk % tq == 0 or tq % tk == 0) and seq % tk == 0 and (tq % cb == 0 or cb % tq == 0)
    assert (tq & (tq - 1)) == 0
    k3 = k.reshape(batch, seq, 256)
    in_specs = [pl.BlockSpec((1, 128, tq), lambda b, hp, qi: (b, hp, qi)),
                pl.BlockSpec((1, seq, 128), lambda b, hp, qi: (b, 0, hp)),
                pl.BlockSpec((1, 128, seq), lambda b, hp, qi: (b, hp, 0))]
    if mode == "fox":
        faug, cT = extra
        args = (qT, k3, vT, faug.reshape(batch, seq, 128), cT)
        in_specs += [pl.BlockSpec((1, seq, 128), lambda b, hp, qi: (b, 0, 0)),
                     pl.BlockSpec((1, 8, tq), lambda b, hp, qi: (b, 0, qi))]
        depth = 256
    else:
        l, lam_p, g64s = extra
        args = (qT, k3, vT, lam_p, g64s)
        in_specs += [_layer(l, 4, DIFF_QK_DIM),
                     pl.BlockSpec((128, 128), lambda b, hp, qi: (0, 0))]
        depth = 128
    cols = nv * tq
    out = pl.pallas_call(
        functools.partial(_attn_body, mode=mode, tq=tq, tk=tk, cb=cb, lam_init=lam_init),
        grid=(batch, 2, seq // tq),
        in_specs=in_specs,
        out_specs=pl.BlockSpec((1, tq, 128), lambda b, hp, qi: (b, qi, hp)),
        out_shape=jax.ShapeDtypeStruct((batch, seq, 256), BF16),
        scratch_shapes=[pltpu.VMEM((depth, cols), BF16),
                        pltpu.VMEM((1, cols), F32),
                        pltpu.VMEM((1, cols), F32),
                        pltpu.VMEM((HEAD_DIM, cols), F32),
                        pltpu.VMEM((tk, cb), F32)],
        compiler_params=_cparams("arbitrary", "arbitrary", "arbitrary"),
        name="attn_" + mode,
    )(*args)
    return out.reshape(batch * seq, 256)


def _compress_body(x_ref, pe_ref, w1a_ref, w1b_ref, b1_ref, w2_ref, b2_ref, g64_ref, kg_ref,
                   kc_ref, vcT_ref):
    x = x_ref[0]
    nblk = x.shape[0]
    a = _dot((x + pe_ref[0:1, :]).astype(BF16), w1a_ref[...])
    b = _dot((x + pe_ref[1:2, :]).astype(BF16), w1b_ref[...])
    pre = a + pltpu.roll(b, nblk - 1, 0) + b1_ref[...]
    hid = jax.nn.gelu(pre).astype(BF16)
    out = _dot(hid, w2_ref[...]) + b2_ref[...]
    kc_ref[0] = _group_norm(out[:, 0:128], g64_ref[...], kg_ref[...]).astype(BF16)
    vcT_ref[0] = out[:, 128:256].T[0:HEAD_DIM, :].astype(BF16)


def _compress(kvc, l, pe_ext, w1a, w1b, b1, w2, b2, g64s, kgain, batch, seq):
    nblk = seq // CMP_STRIDE
    x = kvc.reshape(batch, nblk, CMP_STRIDE * 128)
    const = lambda b: (0, 0)
    return pl.pallas_call(
        _compress_body,
        grid=(batch,),
        in_specs=[pl.BlockSpec((1, nblk, CMP_STRIDE * 128), lambda b: (b, 0, 0)),
                  _layer(l, 2, CMP_STRIDE * 128),
                  _layer(l, *w1a.shape[1:]),
                  _layer(l, *w1b.shape[1:]),
                  _layer(l, *b1.shape[1:]),
                  _layer(l, *w2.shape[1:]),
                  _layer(l, *b2.shape[1:]),
                  pl.BlockSpec((128, 128), const),
                  _layer(l, 1, 128)],
        out_specs=[pl.BlockSpec((1, nblk, 128), lambda b: (b, 0, 0)),
                   pl.BlockSpec((1, HEAD_DIM, nblk), lambda b: (b, 0, 0))],
        out_shape=[jax.ShapeDtypeStruct((batch, nblk, 128), BF16),
                   jax.ShapeDtypeStruct((batch, HEAD_DIM, nblk), BF16)],
        compiler_params=_cparams("arbitrary"),
        name="nsa_compress",
    )(x, pe_ext, w1a, w1b, b1, w2, b2, g64s, kgain)


def _topk_select(imp, k):
    nsel, tq = imp.shape

    def count(mask):
        return jnp.sum(jnp.where(mask, 1.0, 0.0), axis=0, keepdims=True)

    kf = float(k)
    bits = jnp.zeros((1, tq), jnp.int32)
    for bit in range(30, -1, -1):
        cand = bits + (1 << bit)
        ok = count(imp >= lax.bitcast_convert_type(cand, F32)) >= kf
        bits = jnp.where(ok, cand, bits)
    kth = jnp.where(count(imp >= 0.0) >= kf, lax.bitcast_convert_type(bits, F32), -1.0)
    gt = imp > kth
    eq = imp == kth
    need = kf - count(gt)
    r = lax.broadcasted_iota(jnp.int32, (nsel, nsel), 0)
    c = lax.broadcasted_iota(jnp.int32, (nsel, nsel), 1)
    tril = jnp.where(r >= c, 1.0, 0.0).astype(BF16)
    prefix_eq = _dot(tril, jnp.where(eq, 1.0, 0.0).astype(BF16))
    return gt | (eq & (prefix_eq <= need))


def _nsa_body(nqT_ref, kc_ref, vcT_ref, ksw_ref, vswT_ref, hot_ref, ngT_ref, o_ref,
              qs_ref, qw_ref, m_ref, l_ref, acc_ref, s_ref, *, tq, tk, cb, seq):
    qi = pl.program_id(1)
    q0 = qi * tq
    cols = N_HEADS * tq
    nblk = kc_ref.shape[1]
    nsel = seq // SLC_BLOCK
    nb = cols // cb

    zero = jnp.zeros((HEAD_DIM, tq), BF16)
    for hd in range(N_HEADS):
        qh = nqT_ref[0, hd * HEAD_DIM:(hd + 1) * HEAD_DIM, :]
        cs = slice(hd * tq, (hd + 1) * tq)
        qs_ref[0:HEAD_DIM, cs] = qh
        qs_ref[HEAD_DIM:128, cs] = zero
        qw_ref[0:HEAD_DIM, cs] = zero
        qw_ref[HEAD_DIM:128, cs] = qh

    def qpos_of(shape, j):
        return q0 + ((j * cb + lax.broadcasted_iota(jnp.int32, shape, 1)) & (tq - 1))

    kc = kc_ref[0]
    vcT = vcT_ref[0]
    cend = lax.broadcasted_iota(jnp.int32, (nblk, cb), 0) * CMP_STRIDE + (CMP_BLOCK - 1)
    o_cmp, p_sum = [], None
    for j in range(nb):
        s = _dot(kc, qs_ref[0:128, j * cb:(j + 1) * cb])
        s = jnp.where(cend <= qpos_of((nblk, cb), j), s, -jnp.inf)
        m = jnp.max(s, axis=0, keepdims=True)
        m = jnp.where(m == -jnp.inf, 0.0, m)
        e = jnp.exp2(s - m)
        p = e / jnp.maximum(jnp.sum(e, axis=0, keepdims=True), 1e-30)
        o_cmp.append(_dot(vcT, p.astype(BF16)))
        for c0 in range(0, cb, tq):
            ph = p[:, c0:c0 + tq]
            p_sum = ph if p_sum is None else p_sum + ph
    o_cmp = jnp.concatenate(o_cmp, axis=1)

    jj = lax.broadcasted_iota(jnp.int32, (nsel, nblk), 0)
    cc = lax.broadcasted_iota(jnp.int32, (nsel, nblk), 1)
    ov_t = jnp.where((cc * CMP_STRIDE < (jj + 1) * SLC_BLOCK)
                     & (cc * CMP_STRIDE + CMP_BLOCK > jj * SLC_BLOCK), 1.0, 0.0).astype(BF16)
    imp = _dot_exact_lhs(ov_t, p_sum, 3)
    sj = lax.broadcasted_iota(jnp.int32, (nsel, tq), 0)
    qpos_l = q0 + lax.broadcasted_iota(jnp.int32, (nsel, tq), 1)
    cur = qpos_l >> 6
    forced = (sj == 0) | (sj == cur) | (sj == cur - 1)
    valid = sj * SLC_BLOCK <= qpos_l
    imp = jnp.where(forced, FORCE_SCORE, jnp.where(valid, imp, -1.0))
    sel = _topk_select(imp, min(TOPK, nsel))
    bias = jnp.where(sel, 0.0, -(2.0 ** 30)).astype(BF16)
    if nsel < MAX_SEL_BLOCKS:
        bias = jnp.concatenate([bias, jnp.zeros((MAX_SEL_BLOCKS - nsel, tq), BF16)], axis=0)
    for hd in range(N_HEADS):
        qs_ref[128:256, hd * tq:(hd + 1) * tq] = bias

    wlen = WINDOW + tq
    w0 = pl.multiple_of(jnp.maximum(q0 - WINDOW, 0), tq)
    kw = ksw_ref[0, pl.ds(w0, wlen), :]
    vw = vswT_ref[0, :, pl.ds(w0, wlen)]
    kpos_w = w0 + lax.broadcasted_iota(jnp.int32, (wlen, cb), 0)
    o_win = []
    for j in range(nb):
        s = _dot(kw, qw_ref[:, j * cb:(j + 1) * cb])
        dist = qpos_of((wlen, cb), j) - kpos_w
        s = jnp.where((dist >= 0) & (dist < WINDOW), s, -jnp.inf)
        e = jnp.exp2(s - jnp.max(s, axis=0, keepdims=True))
        l_w = jnp.maximum(jnp.sum(e, axis=0, keepdims=True), 1e-30)
        o_win.append(_dot(vw, e.astype(BF16)) / l_w)
    o_win = jnp.concatenate(o_win, axis=1)[HEAD_DIM:128, :]

    def key_tile(k0):
        return jnp.concatenate([ksw_ref[0, pl.ds(k0, tk), :], hot_ref[pl.ds(k0, tk), :]], axis=1)

    def val_tile(k0, j):
        return vswT_ref[0, 0:HEAD_DIM, pl.ds(k0, tk)]

    o_sel = _flash_causal(key_tile, val_tile, qs_ref, m_ref, l_ref, acc_ref, s_ref,
                          q0=q0, tq=tq, tk=tk, cb=cb)

    gates = ngT_ref[0]
    out_t = []
    for hd in range(N_HEADS):
        cs = slice(hd * tq, (hd + 1) * tq)
        out_t.append(gates[3 * hd:3 * hd + 1] * o_cmp[:, cs]
                     + gates[3 * hd + 1:3 * hd + 2] * o_sel[:, cs]
                     + gates[3 * hd + 2:3 * hd + 3] * o_win[:, cs])
    o_ref[0] = jnp.concatenate(out_t, axis=0).T.astype(o_ref.dtype)


def _nsa(nqT, kc, vcT, ksw, vswT, hot, ngT, batch, seq):
    tq, tk, cb = NSA_TQ, ATTN_TK, ATTN_CB
    cols = N_HEADS * tq
    nblk = seq // CMP_STRIDE
    assert seq % tk == 0 and (tk % tq == 0 or tq % tk == 0) and cb % tq == 0 and cols % cb == 0
    assert WINDOW % tq == 0 and seq >= WINDOW + tq and (tq & (tq - 1)) == 0
    assert seq // SLC_BLOCK <= MAX_SEL_BLOCKS
    full = lambda b, qi: (b, 0, 0)
    out = pl.pallas_call(
        functools.partial(_nsa_body, tq=tq, tk=tk, cb=cb, seq=seq),
        grid=(batch, seq // tq),
        in_specs=[pl.BlockSpec((1, 256, tq), lambda b, qi: (b, 0, qi)),
                  pl.BlockSpec((1, nblk, 128), full),
                  pl.BlockSpec((1, HEAD_DIM, nblk), full),
                  pl.BlockSpec((1, seq, 128), full),
                  pl.BlockSpec((1, 128, seq), full),
                  pl.BlockSpec((seq, MAX_SEL_BLOCKS), lambda b, qi: (0, 0)),
                  pl.BlockSpec((1, 16, tq), lambda b, qi: (b, 0, qi))],
        out_specs=pl.BlockSpec((1, tq, 256), lambda b, qi: (b, qi, 0)),
        out_shape=jax.ShapeDtypeStruct((batch, seq, 256), BF16),
        scratch_shapes=[pltpu.VMEM((256, cols), BF16),
                        pltpu.VMEM((128, cols), BF16),
                        pltpu.VMEM((1, cols), F32),
                        pltpu.VMEM((1, cols), F32),
                        pltpu.VMEM((HEAD_DIM, cols), F32),
                        pltpu.VMEM((tk, cb), F32)],
        compiler_params=_cparams("arbitrary", "arbitrary"),
        name="nsa",
    )(nqT, kc, vcT, ksw.reshape(batch, seq, 128), vswT, hot, ngT)
    return out.reshape(batch * seq, 256)


def _merge_body(x_ref, g_ref, oa_ref, ob_ref, oc_ref, u_ref, vn_ref, ws_ref, bt_ref,
                wg_ref, wbr_ref, wout_ref, o_ref):
    x = x_ref[...]
    tm, d = x.shape
    h = (_rms_rows(x) * g_ref[...]).astype(BF16)

    r = lax.broadcasted_iota(jnp.int32, (GMLP_CHUNK, GMLP_CHUNK), 0)
    c = lax.broadcasted_iota(jnp.int32, (GMLP_CHUNK, GMLP_CHUNK), 1)
    lane_grp = lax.broadcasted_iota(jnp.int32, (GMLP_CHUNK, W_MIX), 1) // HEAD_DIM
    w_tril = [jnp.where(r >= c, ws_ref[g], 0.0).astype(BF16) for g in range(N_HEADS)]
    od = []
    for ch in range(tm // GMLP_CHUNK):
        sl = slice(ch * GMLP_CHUNK, (ch + 1) * GMLP_CHUNK)
        vch = vn_ref[sl, :]
        sv = jnp.zeros((GMLP_CHUNK, W_MIX), F32)
        for g in range(N_HEADS):
            sv = jnp.where(lane_grp == g, _dot(w_tril[g], vch), sv)
        od.append(u_ref[sl, :] * (sv + bt_ref[...]))
    o_d = jnp.concatenate(od, axis=0).astype(BF16)

    branches = (oa_ref[...], ob_ref[...], oc_ref[...], o_d)
    acc = jnp.zeros_like(x)
    for n in range(4):
        gate = jax.nn.sigmoid(_dot(h, wg_ref[:, n * d:(n + 1) * d]))
        acc = acc + gate * _dot(branches[n], wbr_ref[n])
    o_ref[...] = x + _dot(acc.astype(BF16), wout_ref[...])


def _merge(x2, l, gain, oa, ob, oc, u, vn, w_s, bt, wg, wbr, wout):
    n, d = x2.shape
    tm = MERGE_TM
    row = lambda i: (i, 0)
    return pl.pallas_call(
        _merge_body,
        grid=(n // tm,),
        in_specs=[pl.BlockSpec((tm, d), row),
                  _layer(l, 1, d),
                  pl.BlockSpec((tm, 256), row),
                  pl.BlockSpec((tm, 256), row),
                  pl.BlockSpec((tm, 256), row),
                  pl.BlockSpec((tm, 256), row),
                  pl.BlockSpec((tm, 256), row),
                  _layer(l, N_HEADS, GMLP_CHUNK, GMLP_CHUNK),
                  _layer(l, GMLP_CHUNK, W_MIX),
                  _layer(l, d, 4 * d),
                  _layer(l, 4, W_MIX, d),
                  _layer(l, d, d)],
        out_specs=pl.BlockSpec((tm, d), row),
        out_shape=jax.ShapeDtypeStruct((n, d), F32),
        compiler_params=_cparams("arbitrary"),
        name="merge",
    )(x2, gain, oa, ob, oc, u, vn, w_s, bt, wg, wbr, wout)


def _block_diag_mean(width, group):
    idx = jnp.arange(width) // group
    return ((idx[:, None] == idx[None, :]).astype(F32) / float(group)).astype(BF16)


def _stage_proj_weight(w_in):
    nkv = 1796
    lead = w_in.shape[:-1]
    flog = w_in[..., 1536:1540]
    cols = [w_in[..., 0:1536],
            w_in[..., 1540:1796],
            w_in[..., nkv + 128:nkv + 192],
            w_in[..., nkv + 256:nkv + 320],
            w_in[..., nkv + 192:nkv + 256],
            w_in[..., nkv + 320:nkv + 384],
            w_in[..., nkv:nkv + 128],
            w_in[..., 2192:2704],
            flog, flog, flog,
            jnp.zeros(lead + (_GATE_COL - 12,), w_in.dtype),
            w_in[..., 2180:2192],
            jnp.zeros(lead + (128 - _GATE_COL - 12,), w_in.dtype)]
    w = jnp.concatenate(cols, axis=-1)
    assert w.shape[-1] == _C_TOTAL
    return w.astype(BF16), w_in[..., 2704:].astype(BF16)


def _stage_compress(pe, w1, b1, w2, b2):
    hid = w1.shape[-1]
    half = CMP_STRIDE
    pe_ext = jnp.concatenate([pe[0], pe[1]], axis=-1)
    pe_ext = jnp.stack([pe_ext[:half].reshape(-1), pe_ext[half:].reshape(-1)])
    w1k = w1[0].reshape(CMP_BLOCK, HEAD_DIM, hid)
    w1v = w1[1].reshape(CMP_BLOCK, HEAD_DIM, hid)
    zk = jnp.zeros_like(w1k)
    top = jnp.concatenate([w1k, zk], axis=-1)
    bot = jnp.concatenate([zk, w1v], axis=-1)
    w1e = jnp.concatenate([top, bot], axis=1)
    w1a = w1e[:half].reshape(half * 128, 2 * hid).astype(BF16)
    w1b = w1e[half:].reshape(half * 128, 2 * hid).astype(BF16)
    b1e = jnp.concatenate([b1[0], b1[1]]).reshape(1, 2 * hid)
    z = jnp.zeros((hid, HEAD_DIM), F32)
    w2e = jnp.concatenate([jnp.concatenate([w2[0], z, z, z], axis=1),
                           jnp.concatenate([z, z, w2[1], z], axis=1)], axis=0).astype(BF16)
    zb = jnp.zeros((HEAD_DIM,), F32)
    b2e = jnp.concatenate([b2[0], zb, b2[1], zb]).reshape(1, 256)
    return pe_ext, w1a, w1b, b1e, w2e, b2e


def kernel(x, ffn1_norm, ffn1_w_in, ffn1_w_out, mix_norm, w_in, diff_q_gain, diff_k_gain, diff_lambda, fox_q_gain, fox_k_gain, fox_f_bias, nsa_q_gain, nsa_k_gain, nsa_cmp_pe, nsa_phi_w1, nsa_phi_b1, nsa_phi_w2, nsa_phi_b2, gmlp_v_gain, gmlp_w_s, gmlp_b_s, w_branch, w_out, ffn2_norm, ffn2_w_in, ffn2_w_out):
    batch, seq, d = x.shape
    depth = w_in.shape[0]
    n = batch * seq
    g32 = _block_diag_mean(256, DIFF_QK_DIM)
    g64 = _block_diag_mean(256, HEAD_DIM)
    g64s = g64[:128, :128]
    hot = (jnp.arange(seq)[:, None] // SLC_BLOCK == jnp.arange(MAX_SEL_BLOCKS)[None, :]).astype(BF16)

    norm3 = lambda g: g.reshape(depth, 1, d).astype(F32)
    ffn1_g, ffn2_g, mix_g = norm3(ffn1_norm), norm3(ffn2_norm), norm3(mix_norm)
    ffn1_wi, ffn1_wo = ffn1_w_in.astype(BF16), ffn1_w_out.astype(BF16)
    ffn2_wi, ffn2_wo = ffn2_w_in.astype(BF16), ffn2_w_out.astype(BF16)
    w_cat, w_gate = _stage_proj_weight(w_in)
    tile = lambda g, r: jnp.tile(g.astype(F32), (1, r))
    gains = jnp.stack([tile(diff_q_gain, 8), tile(diff_k_gain, 8), tile(fox_q_gain, 4),
                       tile(fox_k_gain, 4), tile(nsa_q_gain, 4), tile(nsa_k_gain, 4),
                       gmlp_v_gain.astype(F32), jnp.zeros((depth, 256), F32)], axis=1)
    fb = fox_f_bias.astype(F32)
    fbias = jnp.concatenate([fb, fb, fb, jnp.zeros((depth, 116), F32)], axis=1).reshape(depth, 1, 128)
    lam_p = diff_lambda.astype(F32)
    cmp_w = jax.vmap(_stage_compress)(nsa_cmp_pe, nsa_phi_w1, nsa_phi_b1, nsa_phi_w2, nsa_phi_b2)
    kgain = tile(nsa_k_gain, 2).reshape(depth, 1, 128)
    bt = jnp.repeat(jnp.swapaxes(gmlp_b_s, 1, 2), HEAD_DIM, axis=2).astype(F32)
    w_s = gmlp_w_s.astype(F32)
    w_br, w_o = w_branch.astype(BF16), w_out.astype(BF16)

    x2 = x.reshape(n, d).astype(F32)
    for l in range(depth):
        x2 = _ffn(x2, l, ffn1_g, ffn1_wi, ffn1_wo)
        (aqT, ak, avT, fqT, fk, fvT, faug, cT, nqT, ksw, vswT, kvc, u, vn, ngT) = _proj(
            x2, l, mix_g, w_cat, g32, g64, gains, fbias, batch, seq)

        lam_init = 0.8 - 0.6 * math.exp(-0.3 * l)
        o_a = _attn("diff", aqT, ak, avT, (l, lam_p, g64s), batch, seq, lam_init)
        o_b = _attn("fox", fqT, fk, fvT, (faug, cT), batch, seq)
        kc, vcT = _compress(kvc, l, *cmp_w, g64s, kgain, batch, seq)
        o_c = _nsa(nqT, kc, vcT, ksw, vswT, hot, ngT, batch, seq)

        x2 = _merge(x2, l, mix_g, o_a, o_b, o_c, u, vn, w_s, bt, w_gate, w_br, w_o)
        x2 = _ffn(x2, l, ffn2_g, ffn2_wi, ffn2_wo)
    return x2.reshape(batch, seq, d).astype(x.dtype)
```

```python
import functools
import math

import jax
import jax.numpy as jnp
from jax import lax
from jax.experimental import pallas as pl
from jax.experimental.pallas import tpu as pltpu

F32 = jnp.float32
BF16 = jnp.bfloat16

HEAD_DIM = 64
N_HEADS = 4
W_MIX = N_HEADS * HEAD_DIM
DIFF_QK_DIM = HEAD_DIM // 2
CMP_BLOCK = 32
CMP_STRIDE = 16
SLC_BLOCK = 64
TOPK = 16
WINDOW = 512
FORCE_SCORE = 1.0e4
GMLP_CHUNK = 128
RMS_EPS = 1e-6
NEG = -1.0e30
LOG2E = math.log2(math.e)
INT_MIN = -2 ** 31

VMEM_LIMIT_BYTES = 56 * 1024 * 1024

FFN_TM = 512
FFN_TF = 256
PROJ_TM = 512
MERGE_TM = 512
ATTN_COLS = 2048
ATTN_CB = 512
ATTN_TK = 1024
NSA_TQ = 512
MAX_SEL_BLOCKS = 128


def _dot(a, b):
    return jnp.dot(a, b, preferred_element_type=F32)


def _split3(c):
    hi = c.astype(BF16)
    r1 = c - hi.astype(F32)
    mid = r1.astype(BF16)
    lo = (r1 - mid.astype(F32)).astype(BF16)
    return hi, mid, lo


def _dot_exact_lhs(a, x, terms):
    return sum(_dot(a, part) for part in _split3(x)[:terms])


def _dot_exact_rhs(x, a, terms):
    return sum(_dot(part, a) for part in _split3(x)[:terms])


def _rms_rows(x):
    ms = jnp.mean(x * x, axis=-1, keepdims=True)
    return x * lax.rsqrt(ms + RMS_EPS)


def _group_norm(z, gmat, gain):
    ms = _dot_exact_rhs(z * z, gmat, 2)
    return z * lax.rsqrt(ms + RMS_EPS) * gain


def _cparams(*sem):
    return pltpu.CompilerParams(dimension_semantics=sem, vmem_limit_bytes=VMEM_LIMIT_BYTES)


def _ffn_body(x_ref, g_ref, wa_ref, wb_ref, wo_ref, o_ref, *, n_chunks, tf):
    x = x_ref[...]
    h = (_rms_rows(x) * g_ref[...]).astype(BF16)
    acc = jnp.zeros_like(x)
    for c in range(n_chunks):
        a = _dot(h, wa_ref[:, c * tf:(c + 1) * tf])
        b = _dot(h, wb_ref[:, c * tf:(c + 1) * tf])
        g = (a * jax.nn.sigmoid(a) * b).astype(BF16)
        acc = acc + _dot(g, wo_ref[c * tf:(c + 1) * tf, :])
    o_ref[...] = x + 0.5 * acc


def _layer(l, *block):
    return pl.BlockSpec((None,) + block, lambda *_: (l,) + (0,) * len(block))


def _ffn(x2, l, gain, w_in, w_out):
    n, d = x2.shape
    d_ff = w_out.shape[1]
    tm, tf = FFN_TM, FFN_TF
    assert n % tm == 0 and d_ff % tf == 0
    return pl.pallas_call(
        functools.partial(_ffn_body, n_chunks=d_ff // tf, tf=tf),
        grid=(n // tm,),
        in_specs=[pl.BlockSpec((tm, d), lambda i: (i, 0)),
                  _layer(l, 1, d),
                  pl.BlockSpec((None, d, d_ff), lambda i: (l, 0, 0)),
                  pl.BlockSpec((None, d, d_ff), lambda i: (l, 0, 1)),
                  _layer(l, d_ff, d)],
        out_specs=pl.BlockSpec((tm, d), lambda i: (i, 0)),
        out_shape=jax.ShapeDtypeStruct((n, d), F32),
        compiler_params=_cparams("arbitrary"),
        name="ffn",
    )(x2, gain, w_in, w_in, w_out)


_C_AQ, _C_AK, _C_AV = 0, 256, 512
_C_FQ, _C_FK, _C_FV = 768, 1024, 1280
_C_NQ = 1536
_C_KSW = 1792
_C_VSW = 1920
_C_KVC = 2048
_C_UV = 2176
_C_SMALL = 2688
_C_TOTAL = 2816
_GATE_COL = 16
_AUG_HI, _AUG_MID, _AUG_LO, _AUG_ONE = 0, 4, 8, 12


def _proj_body(x_ref, g_ref, w_ref, g32_ref, g64_ref, gains_ref, fbias_ref,
               aqT_ref, ak_ref, avT_ref, fqT_ref, fk_ref, fvT_ref, faug_ref, cT_ref,
               nqT_ref, ksw_ref, vswT_ref, kvc_ref, u_ref, vn_ref, ngT_ref,
               carry_ref, *, tiles_per_seq):
    i = pl.program_id(0)
    x = x_ref[...]
    tm = x.shape[0]
    h = (_rms_rows(x) * g_ref[...]).astype(BF16)
    z = _dot(h, w_ref[...])
    g32 = g32_ref[...]
    g64 = g64_ref[...]
    gains = gains_ref[...]

    def sec(c0, w=256):
        return z[:, c0:c0 + w]

    aq = _group_norm(sec(_C_AQ), g32, gains[0:1]) * (DIFF_QK_DIM ** -0.5 * LOG2E)
    aqT_ref[0] = aq.T.astype(BF16)
    ak_ref[...] = _group_norm(sec(_C_AK), g32, gains[1:2]).astype(BF16)
    avT_ref[0] = sec(_C_AV).T.astype(BF16)
    fq = _group_norm(sec(_C_FQ), g64, gains[2:3]) * (HEAD_DIM ** -0.5 * LOG2E)
    fqT_ref[0] = fq.T.astype(BF16)
    fk_ref[...] = _group_norm(sec(_C_FK), g64, gains[3:4]).astype(BF16)
    fvT_ref[0] = sec(_C_FV).T.astype(BF16)
    nq = _group_norm(sec(_C_NQ), g64, gains[4:5]) * (HEAD_DIM ** -0.5 * LOG2E)
    nqT_ref[0] = nq.T.astype(BF16)
    ksw_ref[...] = _group_norm(sec(_C_KSW, 128), g64[:128, :128], gains[5:6, :128]).astype(BF16)
    vswT_ref[0] = sec(_C_VSW, 128).T.astype(BF16)
    kvc_ref[...] = sec(_C_KVC, 128)
    guv = jax.nn.gelu(sec(_C_UV, 512))
    u_ref[...] = guv[:, :256]
    vn_ref[...] = _group_norm(guv[:, 256:], g64, gains[6:7]).astype(BF16)
    small = sec(_C_SMALL, 128)
    ngT_ref[0] = jax.nn.sigmoid(small).T[_GATE_COL:_GATE_COL + 16, :]
    t = small + fbias_ref[...]
    logf = jnp.minimum(t, 0.0) - jnp.log(1.0 + jnp.exp(-jnp.abs(t)))

    @pl.when(i % tiles_per_seq == 0)
    def _():
        carry_ref[...] = jnp.zeros_like(carry_ref)

    row = lax.broadcasted_iota(jnp.int32, (tm, tm), 0)
    col = lax.broadcasted_iota(jnp.int32, (tm, tm), 1)
    tril = jnp.where(row >= col, 1.0, 0.0).astype(BF16)
    csum = _dot_exact_lhs(tril, logf, 3) + carry_ref[0:1, :]
    carry_ref[...] = jnp.broadcast_to(csum[tm - 1:tm, :], carry_ref.shape)
    c2 = csum * LOG2E
    cT_ref[0] = c2.T[0:8, :]
    hi, mid, lo = _split3(c2)
    lane = lax.broadcasted_iota(jnp.int32, (tm, 128), 1)
    one = jnp.where(lane < _AUG_ONE + 3, 1.0, 0.0).astype(BF16)
    faug_ref[...] = jnp.where(lane < _AUG_MID, hi,
                              jnp.where(lane < _AUG_LO, mid, jnp.where(lane < _AUG_ONE, lo, one)))


def _proj(x2, l, gain, w_cat, g32, g64, gains, fbias, batch, seq):
    n, d = x2.shape
    tm = PROJ_TM
    assert seq % tm == 0
    tps = seq // tm
    nt = n // tm
    const = lambda i: (0, 0)
    row = lambda i: (i, 0)
    trn = lambda i: (i // tps, 0, i % tps)
    nat = lambda w, dt: (jax.ShapeDtypeStruct((n, w), dt), pl.BlockSpec((tm, w), row))
    tr = lambda r, dt: (jax.ShapeDtypeStruct((batch, r, seq), dt), pl.BlockSpec((1, r, tm), trn))
    outs = [
        tr(256, BF16), nat(256, BF16), tr(256, BF16),
        tr(256, BF16), nat(256, BF16), tr(256, BF16),
        nat(128, BF16), tr(8, F32),
        tr(256, BF16), nat(128, BF16), tr(128, BF16),
        nat(128, F32), nat(256, F32), nat(256, BF16), tr(16, F32),
    ]
    return pl.pallas_call(
        functools.partial(_proj_body, tiles_per_seq=tps),
        grid=(nt,),
        in_specs=[pl.BlockSpec((tm, d), row),
                  _layer(l, 1, d),
                  _layer(l, d, _C_TOTAL),
                  pl.BlockSpec((256, 256), const),
                  pl.BlockSpec((256, 256), const),
                  _layer(l, 8, 256),
                  _layer(l, 1, 128)],
        out_specs=[o[1] for o in outs],
        out_shape=[o[0] for o in outs],
        scratch_shapes=[pltpu.VMEM((8, 128), F32)],
        compiler_params=_cparams("arbitrary"),
        name="proj",
    )(x2, gain, w_cat, g32, g64, gains, fbias)


def _flash_causal(key_tile, val_tile, qs_ref, m_ref, l_ref, acc_ref, s_ref, *, q0, tq, tk, cb):
    cols = qs_ref.shape[1]
    nb = cols // cb
    n_full = q0 // tk
    m_ref[...] = jnp.full_like(m_ref, NEG)
    l_ref[...] = jnp.zeros_like(l_ref)
    acc_ref[...] = jnp.zeros_like(acc_ref)
    krow = lax.broadcasted_iota(jnp.int32, (tk, cb), 0)
    kcol = lax.broadcasted_iota(jnp.int32, (tk, cb), 1)
    row_minus_q = [krow - ((j * cb + kcol) & (tq - 1)) for j in range(nb)]
    no_mask = jnp.int32(2 ** 30)

    def scores(kk, j, thr):
        s = _dot(kk, qs_ref[:, j * cb:(j + 1) * cb])
        if thr is not None:
            s = jnp.where(row_minus_q[j] <= thr, s, NEG)
        return s

    s_ref[...] = scores(key_tile(0), 0, jnp.where(n_full == 0, q0, no_mask))

    def step(ki, masked, next_masked, final=False):
        k0 = pl.multiple_of(ki * tk, tk)
        kk = key_tile(k0)
        s = s_ref[...]
        for j in range(nb):
            cs = slice(j * cb, (j + 1) * cb)
            s_next = None
            if j + 1 < nb:
                s_next = scores(kk, j + 1, q0 - k0 if masked else None)
            elif not final:
                k1 = pl.multiple_of(k0 + tk, tk)
                s_ref[...] = scores(key_tile(k1), 0, q0 - k1 if next_masked else None)
            m_old = m_ref[:, cs]
            m_new = jnp.maximum(m_old, jnp.max(s, axis=0, keepdims=True))
            alpha = jnp.exp2(m_old - m_new)
            p = jnp.exp2(s - m_new)
            l_ref[:, cs] = alpha * l_ref[:, cs] + jnp.sum(p, axis=0, keepdims=True)
            acc_ref[:, cs] = alpha * acc_ref[:, cs] + _dot(val_tile(k0, j), p.astype(BF16))
            m_ref[:, cs] = m_new
            s = s_next

    def loop_body(ki, carry):
        step(ki, False, False)
        return carry

    lax.fori_loop(0, n_full - 1, loop_body, 0)

    @pl.when(n_full > 0)
    def _():
        step(n_full - 1, False, True)

    n_diag = max(tq // tk, 1)
    for dt in range(n_diag):
        step(n_full + dt, True, True, final=dt == n_diag - 1)
    return acc_ref[...] / jnp.maximum(l_ref[...], 1e-30)


def _attn_body(*refs, mode, tq, tk, cb, lam_init):
    if mode == "fox":
        qT_ref, k_ref, vT_ref, aug_ref, cT_ref, o_ref, qs_ref, m_ref, l_ref, acc_ref, s_ref = refs
    else:
        qT_ref, k_ref, vT_ref, lam_ref, g64_ref, o_ref, qs_ref, m_ref, l_ref, acc_ref, s_ref = refs
    nv = 2 if mode == "fox" else 4
    wv = 128 // nv
    hp = pl.program_id(1)
    qi = pl.program_id(2)
    q0 = qi * tq

    qT = qT_ref[0]
    rowi = lax.broadcasted_iota(jnp.int32, (128, tq), 0)
    zero = jnp.zeros_like(qT)
    for vh in range(nv):
        keep = (rowi >= vh * wv) & (rowi < (vh + 1) * wv)
        qs_ref[0:128, vh * tq:(vh + 1) * tq] = jnp.where(keep, qT, zero)
        if mode == "fox":
            hd = 2 * hp + vh
            hi, mid, lo = _split3(cT_ref[0, pl.ds(hd, 1), :])
            neg = (rowi == _AUG_HI + hd) | (rowi == _AUG_MID + hd) | (rowi == _AUG_LO + hd)
            aug = jnp.where(neg, -1.0, 0.0).astype(BF16)
            aug = jnp.where(rowi == _AUG_ONE, hi, aug)
            aug = jnp.where(rowi == _AUG_ONE + 1, mid, aug)
            aug = jnp.where(rowi == _AUG_ONE + 2, lo, aug)
            qs_ref[128:256, vh * tq:(vh + 1) * tq] = aug

    def key_tile(k0):
        kk = k_ref[0, pl.ds(k0, tk), :]
        if mode == "fox":
            kk = jnp.concatenate([kk, aug_ref[0, pl.ds(k0, tk), :]], axis=1)
        return kk

    def val_tile(k0, j):
        hd = ((j * cb) // tq) // (nv // 2)
        return vT_ref[0, hd * HEAD_DIM:(hd + 1) * HEAD_DIM, pl.ds(k0, tk)]

    o = _flash_causal(key_tile, val_tile, qs_ref, m_ref, l_ref, acc_ref, s_ref,
                      q0=q0, tq=tq, tk=tk, cb=cb)
    if mode == "fox":
        out_t = jnp.concatenate([o[:, 0:tq], o[:, tq:2 * tq]], axis=0)
    else:
        lp = lam_ref[...]
        lam = (jnp.exp(jnp.sum(lp[0:1] * lp[1:2], axis=-1, keepdims=True))
               - jnp.exp(jnp.sum(lp[2:3] * lp[3:4], axis=-1, keepdims=True)) + lam_init)
        d0 = o[:, 0:tq] - lam * o[:, tq:2 * tq]
        d1 = o[:, 2 * tq:3 * tq] - lam * o[:, 3 * tq:4 * tq]
        dd = jnp.concatenate([d0, d1], axis=0)
        ms = _dot_exact_lhs(g64_ref[...], dd * dd, 2)
        out_t = dd * lax.rsqrt(ms + RMS_EPS) * (1.0 - lam_init)
    o_ref[0] = out_t.T.astype(o_ref.dtype)


def _attn(mode, qT, k, vT, extra, batch, seq, lam_init=0.0):
    nv = 2 if mode == "fox" else 4
    tq = ATTN_COLS // nv
    tk, cb = ATTN_TK, ATTN_CB
    assert (tk % tq == 0 or tq % tk == 0) and seq % tk == 0 and (tq % cb == 0 or cb % tq == 0)
    assert (tq & (tq - 1)) == 0
    k3 = k.reshape(batch, seq, 256)
    in_specs = [pl.BlockSpec((1, 128, tq), lambda b, hp, qi: (b, hp, qi)),
                pl.BlockSpec((1, seq, 128), lambda b, hp, qi: (b, 0, hp)),
                pl.BlockSpec((1, 128, seq), lambda b, hp, qi: (b, hp, 0))]
    if mode == "fox":
        faug, cT = extra
        args = (qT, k3, vT, faug.reshape(batch, seq, 128), cT)
        in_specs += [pl.BlockSpec((1, seq, 128), lambda b, hp, qi: (b, 0, 0)),
                     pl.BlockSpec((1, 8, tq), lambda b, hp, qi: (b, 0, qi))]
        depth = 256
    else:
        l, lam_p, g64s = extra
        args = (qT, k3, vT, lam_p, g64s)
        in_specs += [_layer(l, 4, DIFF_QK_DIM),
                     pl.BlockSpec((128, 128), lambda b, hp, qi: (0, 0))]
        depth = 128
    cols = nv * tq
    out = pl.pallas_call(
        functools.partial(_attn_body, mode=mode, tq=tq, tk=tk, cb=cb, lam_init=lam_init),
        grid=(batch, 2, seq // tq),
        in_specs=in_specs,
        out_specs=pl.BlockSpec((1, tq, 128), lambda b, hp, qi: (b, qi, hp)),
        out_shape=jax.ShapeDtypeStruct((batch, seq, 256), BF16),
        scratch_shapes=[pltpu.VMEM((depth, cols), BF16),
                        pltpu.VMEM((1, cols), F32),
                        pltpu.VMEM((1, cols), F32),
                        pltpu.VMEM((HEAD_DIM, cols), F32),
                        pltpu.VMEM((tk, cb), F32)],
        compiler_params=_cparams("arbitrary", "arbitrary", "arbitrary"),
        name="attn_" + mode,
    )(*args)
    return out.reshape(batch * seq, 256)


def _compress_body(x_ref, pe_ref, w1a_ref, w1b_ref, b1_ref, w2_ref, b2_ref, g64_ref, kg_ref,
                   kc_ref, vcT_ref):
    x = x_ref[0]
    nblk = x.shape[0]
    a = _dot((x + pe_ref[0:1, :]).astype(BF16), w1a_ref[...])
    b = _dot((x + pe_ref[1:2, :]).astype(BF16), w1b_ref[...])
    pre = a + pltpu.roll(b, nblk - 1, 0) + b1_ref[...]
    hid = jax.nn.gelu(pre).astype(BF16)
    out = _dot(hid, w2_ref[...]) + b2_ref[...]
    kc_ref[0] = _group_norm(out[:, 0:128], g64_ref[...], kg_ref[...]).astype(BF16)
    vcT_ref[0] = out[:, 128:256].T[0:HEAD_DIM, :].astype(BF16)


def _compress(kvc, l, pe_ext, w1a, w1b, b1, w2, b2, g64s, kgain, batch, seq):
    nblk = seq // CMP_STRIDE
    x = kvc.reshape(batch, nblk, CMP_STRIDE * 128)
    const = lambda b: (0, 0)
    return pl.pallas_call(
        _compress_body,
        grid=(batch,),
        in_specs=[pl.BlockSpec((1, nblk, CMP_STRIDE * 128), lambda b: (b, 0, 0)),
                  _layer(l, 2, CMP_STRIDE * 128),
                  _layer(l, *w1a.shape[1:]),
                  _layer(l, *w1b.shape[1:]),
                  _layer(l, *b1.shape[1:]),
                  _layer(l, *w2.shape[1:]),
                  _layer(l, *b2.shape[1:]),
                  pl.BlockSpec((128, 128), const),
                  _layer(l, 1, 128)],
        out_specs=[pl.BlockSpec((1, nblk, 128), lambda b: (b, 0, 0)),
                   pl.BlockSpec((1, HEAD_DIM, nblk), lambda b: (b, 0, 0))],
        out_shape=[jax.ShapeDtypeStruct((batch, nblk, 128), BF16),
                   jax.ShapeDtypeStruct((batch, HEAD_DIM, nblk), BF16)],
        compiler_params=_cparams("arbitrary"),
        name="nsa_compress",
    )(x, pe_ext, w1a, w1b, b1, w2, b2, g64s, kgain)


def _topk_select(imp, k):
    nsel, tq = imp.shape

    def count(mask):
        return jnp.sum(jnp.where(mask, 1.0, 0.0), axis=0, keepdims=True)

    kf = float(k)
    bits = jnp.zeros((1, tq), jnp.int32)
    for bit in range(30, -1, -1):
        cand = bits + (1 << bit)
        if bit < 23:
            cand = jnp.where(bits != 0, cand, 0x7F800000)
        ok = count(imp >= lax.bitcast_convert_type(cand, F32)) >= kf
        bits = jnp.where(ok, cand, bits)
    kth = jnp.where(count(imp >= 0.0) >= kf, lax.bitcast_convert_type(bits, F32), -1.0)
    gt = imp > kth
    eq = imp == kth
    need = kf - count(gt)
    r = lax.broadcasted_iota(jnp.int32, (nsel, nsel), 0)
    c = lax.broadcasted_iota(jnp.int32, (nsel, nsel), 1)
    tril = jnp.where(r >= c, 1.0, 0.0).astype(BF16)
    prefix_eq = _dot(tril, jnp.where(eq, 1.0, 0.0).astype(BF16))
    return gt | (eq & (prefix_eq <= need))


def _nsa_body(nqT_ref, kc_ref, vcT_ref, ksw_ref, vswT_ref, hot_ref, ngT_ref, o_ref,
              qs_ref, qw_ref, m_ref, l_ref, acc_ref, s_ref, *, tq, tk, cb, seq):
    qi = pl.program_id(1)
    q0 = qi * tq
    cols = N_HEADS * tq
    nblk = kc_ref.shape[1]
    nsel = seq // SLC_BLOCK
    nb = cols // cb

    zero = jnp.zeros((HEAD_DIM, tq), BF16)
    for hd in range(N_HEADS):
        qh = nqT_ref[0, hd * HEAD_DIM:(hd + 1) * HEAD_DIM, :]
        cs = slice(hd * tq, (hd + 1) * tq)
        qs_ref[0:HEAD_DIM, cs] = qh
        qs_ref[HEAD_DIM:128, cs] = zero
        qw_ref[0:HEAD_DIM, cs] = zero
        qw_ref[HEAD_DIM:128, cs] = qh

    def qoff(shape):
        return lax.broadcasted_iota(jnp.int32, shape, 1) & (tq - 1)

    kc = kc_ref[0]
    vcT = vcT_ref[0]
    cend_minus_q = (lax.broadcasted_iota(jnp.int32, (nblk, cb), 0) * CMP_STRIDE
                    + (CMP_BLOCK - 1) - qoff((nblk, cb)))
    o_cmp, p_sum = [], None
    for j in range(nb):
        s = _dot(kc, qs_ref[0:128, j * cb:(j + 1) * cb])
        s = jnp.where(cend_minus_q <= q0, s, -jnp.inf)
        m = jnp.max(s, axis=0, keepdims=True)
        m = jnp.where(m == -jnp.inf, 0.0, m)
        e = jnp.exp2(s - m)
        p = e / jnp.maximum(jnp.sum(e, axis=0, keepdims=True), 1e-30)
        o_cmp.append(_dot(vcT, p.astype(BF16)))
        for c0 in range(0, cb, tq):
            ph = p[:, c0:c0 + tq]
            p_sum = ph if p_sum is None else p_sum + ph
    o_cmp = jnp.concatenate(o_cmp, axis=1)

    jj = lax.broadcasted_iota(jnp.int32, (nsel, nblk), 0)
    cc = lax.broadcasted_iota(jnp.int32, (nsel, nblk), 1)
    ov_t = jnp.where((cc * CMP_STRIDE < (jj + 1) * SLC_BLOCK)
                     & (cc * CMP_STRIDE + CMP_BLOCK > jj * SLC_BLOCK), 1.0, 0.0).astype(BF16)
    imp = _dot_exact_lhs(ov_t, p_sum, 3)
    sj = lax.broadcasted_iota(jnp.int32, (nsel, tq), 0)
    qpos_l = q0 + lax.broadcasted_iota(jnp.int32, (nsel, tq), 1)
    cur = qpos_l >> 6
    forced = (sj == 0) | (sj == cur) | (sj == cur - 1)
    valid = sj * SLC_BLOCK <= qpos_l
    imp = jnp.where(forced, FORCE_SCORE, jnp.where(valid, imp, -1.0))
    sel = _topk_select(imp, min(TOPK, nsel))
    bias = jnp.where(sel, 0.0, -(2.0 ** 30)).astype(BF16)
    if nsel < MAX_SEL_BLOCKS:
        bias = jnp.concatenate([bias, jnp.zeros((MAX_SEL_BLOCKS - nsel, tq), BF16)], axis=0)
    for hd in range(N_HEADS):
        qs_ref[128:256, hd * tq:(hd + 1) * tq] = bias

    key_minus_q = lax.broadcasted_iota(jnp.int32, (tq, cb), 0) - qoff((tq, cb))

    def window_half(k0, own):
        kw = ksw_ref[0, pl.ds(k0, tq), :]
        vw = vswT_ref[0, HEAD_DIM:128, pl.ds(k0, tq)]
        keep = key_minus_q <= 0 if own else key_minus_q > 0
        s_next = _dot(kw, qw_ref[:, 0:cb])
        for j in range(nb):
            cs = slice(j * cb, (j + 1) * cb)
            s = jnp.where(keep, s_next, NEG)
            if j + 1 < nb:
                s_next = _dot(kw, qw_ref[:, (j + 1) * cb:(j + 2) * cb])
            if own:
                m_new = jnp.max(s, axis=0, keepdims=True)
                p = jnp.exp2(s - m_new)
                l_ref[:, cs] = jnp.sum(p, axis=0, keepdims=True)
                acc_ref[:, cs] = _dot(vw, p.astype(BF16))
            else:
                m_old = m_ref[:, cs]
                m_new = jnp.maximum(m_old, jnp.max(s, axis=0, keepdims=True))
                alpha = jnp.exp2(m_old - m_new)
                p = jnp.exp2(s - m_new)
                l_ref[:, cs] = alpha * l_ref[:, cs] + jnp.sum(p, axis=0, keepdims=True)
                acc_ref[:, cs] = alpha * acc_ref[:, cs] + _dot(vw, p.astype(BF16))
            m_ref[:, cs] = m_new

    window_half(pl.multiple_of(q0, tq), True)

    @pl.when(qi > 0)
    def _():
        window_half(pl.multiple_of(q0 - WINDOW, tq), False)

    o_win = acc_ref[...] / jnp.maximum(l_ref[...], 1e-30)

    def key_tile(k0):
        return jnp.concatenate([ksw_ref[0, pl.ds(k0, tk), :], hot_ref[pl.ds(k0, tk), :]], axis=1)

    def val_tile(k0, j):
        return vswT_ref[0, 0:HEAD_DIM, pl.ds(k0, tk)]

    o_sel = _flash_causal(key_tile, val_tile, qs_ref, m_ref, l_ref, acc_ref, s_ref,
                          q0=q0, tq=tq, tk=tk, cb=cb)

    gates = ngT_ref[0]
    out_t = []
    for hd in range(N_HEADS):
        cs = slice(hd * tq, (hd + 1) * tq)
        out_t.append(gates[3 * hd:3 * hd + 1] * o_cmp[:, cs]
                     + gates[3 * hd + 1:3 * hd + 2] * o_sel[:, cs]
                     + gates[3 * hd + 2:3 * hd + 3] * o_win[:, cs])
    o_ref[0] = jnp.concatenate(out_t, axis=0).T.astype(o_ref.dtype)


def _nsa(nqT, kc, vcT, ksw, vswT, hot, ngT, batch, seq):
    tq, tk, cb = NSA_TQ, ATTN_TK, ATTN_CB
    cols = N_HEADS * tq
    nblk = seq // CMP_STRIDE
    assert seq % tk == 0 and (tk % tq == 0 or tq % tk == 0) and cb % tq == 0 and cols % cb == 0
    assert tq == WINDOW and seq % tq == 0 and (tq & (tq - 1)) == 0
    assert seq // SLC_BLOCK <= MAX_SEL_BLOCKS
    full = lambda b, qi: (b, 0, 0)
    out = pl.pallas_call(
        functools.partial(_nsa_body, tq=tq, tk=tk, cb=cb, seq=seq),
        grid=(batch, seq // tq),
        in_specs=[pl.BlockSpec((1, 256, tq), lambda b, qi: (b, 0, qi)),
                  pl.BlockSpec((1, nblk, 128), full),
                  pl.BlockSpec((1, HEAD_DIM, nblk), full),
                  pl.BlockSpec((1, seq, 128), full),
                  pl.BlockSpec((1, 128, seq), full),
                  pl.BlockSpec((seq, MAX_SEL_BLOCKS), lambda b, qi: (0, 0)),
                  pl.BlockSpec((1, 16, tq), lambda b, qi: (b, 0, qi))],
        out_specs=pl.BlockSpec((1, tq, 256), lambda b, qi: (b, qi, 0)),
        out_shape=jax.ShapeDtypeStruct((batch, seq, 256), BF16),
        scratch_shapes=[pltpu.VMEM((256, cols), BF16),
                        pltpu.VMEM((128, cols), BF16),
                        pltpu.VMEM((1, cols), F32),
                        pltpu.VMEM((1, cols), F32),
                        pltpu.VMEM((HEAD_DIM, cols), F32),
                        pltpu.VMEM((tk, cb), F32)],
        compiler_params=_cparams("arbitrary", "arbitrary"),
        name="nsa",
    )(nqT, kc, vcT, ksw.reshape(batch, seq, 128), vswT, hot, ngT)
    return out.reshape(batch * seq, 256)


def _merge_body(x_ref, g_ref, oa_ref, ob_ref, oc_ref, u_ref, vn_ref, ws_ref, bt_ref,
                wg_ref, wbr_ref, wout_ref, o_ref):
    x = x_ref[...]
    tm, d = x.shape
    h = (_rms_rows(x) * g_ref[...]).astype(BF16)

    r = lax.broadcasted_iota(jnp.int32, (GMLP_CHUNK, GMLP_CHUNK), 0)
    c = lax.broadcasted_iota(jnp.int32, (GMLP_CHUNK, GMLP_CHUNK), 1)
    lane_grp = lax.broadcasted_iota(jnp.int32, (GMLP_CHUNK, W_MIX), 1) // HEAD_DIM
    w_tril = [jnp.where(r >= c, ws_ref[g], 0.0).astype(BF16) for g in range(N_HEADS)]
    od = []
    for ch in range(tm // GMLP_CHUNK):
        sl = slice(ch * GMLP_CHUNK, (ch + 1) * GMLP_CHUNK)
        vch = vn_ref[sl, :]
        sv = jnp.zeros((GMLP_CHUNK, W_MIX), F32)
        for g in range(N_HEADS):
            sv = jnp.where(lane_grp == g, _dot(w_tril[g], vch), sv)
        od.append(u_ref[sl, :] * (sv + bt_ref[...]))
    o_d = jnp.concatenate(od, axis=0).astype(BF16)

    branches = (oa_ref[...], ob_ref[...], oc_ref[...], o_d)
    acc = jnp.zeros_like(x)
    for n in range(4):
        gate = jax.nn.sigmoid(_dot(h, wg_ref[:, n * d:(n + 1) * d]))
        acc = acc + gate * _dot(branches[n], wbr_ref[n])
    o_ref[...] = x + _dot(acc.astype(BF16), wout_ref[...])


def _merge(x2, l, gain, oa, ob, oc, u, vn, w_s, bt, wg, wbr, wout):
    n, d = x2.shape
    tm = MERGE_TM
    row = lambda i: (i, 0)
    return pl.pallas_call(
        _merge_body,
        grid=(n // tm,),
        in_specs=[pl.BlockSpec((tm, d), row),
                  _layer(l, 1, d),
                  pl.BlockSpec((tm, 256), row),
                  pl.BlockSpec((tm, 256), row),
                  pl.BlockSpec((tm, 256), row),
                  pl.BlockSpec((tm, 256), row),
                  pl.BlockSpec((tm, 256), row),
                  _layer(l, N_HEADS, GMLP_CHUNK, GMLP_CHUNK),
                  _layer(l, GMLP_CHUNK, W_MIX),
                  _layer(l, d, 4 * d),
                  _layer(l, 4, W_MIX, d),
                  _layer(l, d, d)],
        out_specs=pl.BlockSpec((tm, d), row),
        out_shape=jax.ShapeDtypeStruct((n, d), F32),
        compiler_params=_cparams("arbitrary"),
        name="merge",
    )(x2, gain, oa, ob, oc, u, vn, w_s, bt, wg, wbr, wout)


def _block_diag_mean(width, group):
    idx = jnp.arange(width) // group
    return ((idx[:, None] == idx[None, :]).astype(F32) / float(group)).astype(BF16)


def _stage_proj_weight(w_in):
    nkv = 1796
    lead = w_in.shape[:-1]
    flog = w_in[..., 1536:1540]
    cols = [w_in[..., 0:1536],
            w_in[..., 1540:1796],
            w_in[..., nkv + 128:nkv + 192],
            w_in[..., nkv + 256:nkv + 320],
            w_in[..., nkv + 192:nkv + 256],
            w_in[..., nkv + 320:nkv + 384],
            w_in[..., nkv:nkv + 128],
            w_in[..., 2192:2704],
            flog, flog, flog,
            jnp.zeros(lead + (_GATE_COL - 12,), w_in.dtype),
            w_in[..., 2180:2192],
            jnp.zeros(lead + (128 - _GATE_COL - 12,), w_in.dtype)]
    w = jnp.concatenate(cols, axis=-1)
    assert w.shape[-1] == _C_TOTAL
    return w.astype(BF16), w_in[..., 2704:].astype(BF16)


def _stage_compress(pe, w1, b1, w2, b2):
    hid = w1.shape[-1]
    half = CMP_STRIDE
    pe_ext = jnp.concatenate([pe[0], pe[1]], axis=-1)
    pe_ext = jnp.stack([pe_ext[:half].reshape(-1), pe_ext[half:].reshape(-1)])
    w1k = w1[0].reshape(CMP_BLOCK, HEAD_DIM, hid)
    w1v = w1[1].reshape(CMP_BLOCK, HEAD_DIM, hid)
    zk = jnp.zeros_like(w1k)
    top = jnp.concatenate([w1k, zk], axis=-1)
    bot = jnp.concatenate([zk, w1v], axis=-1)
    w1e = jnp.concatenate([top, bot], axis=1)
    w1a = w1e[:half].reshape(half * 128, 2 * hid).astype(BF16)
    w1b = w1e[half:].reshape(half * 128, 2 * hid).astype(BF16)
    b1e = jnp.concatenate([b1[0], b1[1]]).reshape(1, 2 * hid)
    z = jnp.zeros((hid, HEAD_DIM), F32)
    w2e = jnp.concatenate([jnp.concatenate([w2[0], z, z, z], axis=1),
                           jnp.concatenate([z, z, w2[1], z], axis=1)], axis=0).astype(BF16)
    zb = jnp.zeros((HEAD_DIM,), F32)
    b2e = jnp.concatenate([b2[0], zb, b2[1], zb]).reshape(1, 256)
    return pe_ext, w1a, w1b, b1e, w2e, b2e


def kernel(x, ffn1_norm, ffn1_w_in, ffn1_w_out, mix_norm, w_in, diff_q_gain, diff_k_gain, diff_lambda, fox_q_gain, fox_k_gain, fox_f_bias, nsa_q_gain, nsa_k_gain, nsa_cmp_pe, nsa_phi_w1, nsa_phi_b1, nsa_phi_w2, nsa_phi_b2, gmlp_v_gain, gmlp_w_s, gmlp_b_s, w_branch, w_out, ffn2_norm, ffn2_w_in, ffn2_w_out):
    batch, seq, d = x.shape
    depth = w_in.shape[0]
    n = batch * seq
    g32 = _block_diag_mean(256, DIFF_QK_DIM)
    g64 = _block_diag_mean(256, HEAD_DIM)
    g64s = g64[:128, :128]
    hot = (jnp.arange(seq)[:, None] // SLC_BLOCK == jnp.arange(MAX_SEL_BLOCKS)[None, :]).astype(BF16)

    norm3 = lambda g: g.reshape(depth, 1, d).astype(F32)
    ffn1_g, ffn2_g, mix_g = norm3(ffn1_norm), norm3(ffn2_norm), norm3(mix_norm)
    ffn1_wi, ffn1_wo = ffn1_w_in.astype(BF16), ffn1_w_out.astype(BF16)
    ffn2_wi, ffn2_wo = ffn2_w_in.astype(BF16), ffn2_w_out.astype(BF16)
    w_cat, w_gate = _stage_proj_weight(w_in)
    tile = lambda g, r: jnp.tile(g.astype(F32), (1, r))
    gains = jnp.stack([tile(diff_q_gain, 8), tile(diff_k_gain, 8), tile(fox_q_gain, 4),
                       tile(fox_k_gain, 4), tile(nsa_q_gain, 4), tile(nsa_k_gain, 4),
                       gmlp_v_gain.astype(F32), jnp.zeros((depth, 256), F32)], axis=1)
    fb = fox_f_bias.astype(F32)
    fbias = jnp.concatenate([fb, fb, fb, jnp.zeros((depth, 116), F32)], axis=1).reshape(depth, 1, 128)
    lam_p = diff_lambda.astype(F32)
    cmp_w = jax.vmap(_stage_compress)(nsa_cmp_pe, nsa_phi_w1, nsa_phi_b1, nsa_phi_w2, nsa_phi_b2)
    kgain = tile(nsa_k_gain, 2).reshape(depth, 1, 128)
    bt = jnp.repeat(jnp.swapaxes(gmlp_b_s, 1, 2), HEAD_DIM, axis=2).astype(F32)
    w_s = gmlp_w_s.astype(F32)
    w_br, w_o = w_branch.astype(BF16), w_out.astype(BF16)

    x2 = x.reshape(n, d).astype(F32)
    for l in range(depth):
        x2 = _ffn(x2, l, ffn1_g, ffn1_wi, ffn1_wo)
        (aqT, ak, avT, fqT, fk, fvT, faug, cT, nqT, ksw, vswT, kvc, u, vn, ngT) = _proj(
            x2, l, mix_g, w_cat, g32, g64, gains, fbias, batch, seq)

        lam_init = 0.8 - 0.6 * math.exp(-0.3 * l)
        o_a = _attn("diff", aqT, ak, avT, (l, lam_p, g64s), batch, seq, lam_init)
        o_b = _attn("fox", fqT, fk, fvT, (faug, cT), batch, seq)
        kc, vcT = _compress(kvc, l, *cmp_w, g64s, kgain, batch, seq)
        o_c = _nsa(nqT, kc, vcT, ksw, vswT, hot, ngT, batch, seq)

        x2 = _merge(x2, l, mix_g, o_a, o_b, o_c, u, vn, w_s, bt, w_gate, w_br, w_o)
        x2 = _ffn(x2, l, ffn2_g, ffn2_wi, ffn2_wo)
    return x2.reshape(batch, seq, d).astype(x.dtype)
```

```python
import functools
import math

import jax
import jax.numpy as jnp
from jax import lax
from jax.experimental import pallas as pl
from jax.experimental.pallas import tpu as pltpu

F32 = jnp.float32
BF16 = jnp.bfloat16

HEAD_DIM = 64
N_HEADS = 4
W_MIX = N_HEADS * HEAD_DIM
DIFF_QK_DIM = HEAD_DIM // 2
CMP_BLOCK = 32
CMP_STRIDE = 16
SLC_BLOCK = 64
TOPK = 16
WINDOW = 512
FORCE_SCORE = 1.0e4
GMLP_CHUNK = 128
RMS_EPS = 1e-6
NEG = -1.0e30
LOG2E = math.log2(math.e)
INT_MIN = -2 ** 31

VMEM_LIMIT_BYTES = 56 * 1024 * 1024

FFN_TM = 512
FFN_TF = 256
PROJ_TM = 512
MERGE_TM = 512
ATTN_COLS = 2048
ATTN_CB = 512
ATTN_TK = 1024
NSA_TQ = 512
MAX_SEL_BLOCKS = 128


def _dot(a, b):
    return jnp.dot(a, b, preferred_element_type=F32)


def _split3(c):
    hi = c.astype(BF16)
    r1 = c - hi.astype(F32)
    mid = r1.astype(BF16)
    lo = (r1 - mid.astype(F32)).astype(BF16)
    return hi, mid, lo


def _dot_exact_lhs(a, x, terms):
    return sum(_dot(a, part) for part in _split3(x)[:terms])


def _dot_exact_rhs(x, a, terms):
    return sum(_dot(part, a) for part in _split3(x)[:terms])


def _rms_rows(x):
    ms = jnp.mean(x * x, axis=-1, keepdims=True)
    return x * lax.rsqrt(ms + RMS_EPS)


def _group_norm(z, gmat, gain):
    ms = _dot_exact_rhs(z * z, gmat, 2)
    return z * lax.rsqrt(ms + RMS_EPS) * gain


def _cparams(*sem):
    return pltpu.CompilerParams(dimension_semantics=sem, vmem_limit_bytes=VMEM_LIMIT_BYTES)


def _ffn_body(x_ref, g_ref, wa_ref, wb_ref, wo_ref, o_ref, *, n_chunks, tf):
    x = x_ref[...]
    h = (_rms_rows(x) * g_ref[...]).astype(BF16)
    acc = jnp.zeros_like(x)
    for c in range(n_chunks):
        a = _dot(h, wa_ref[:, c * tf:(c + 1) * tf])
        b = _dot(h, wb_ref[:, c * tf:(c + 1) * tf])
        g = (a * jax.nn.sigmoid(a) * b).astype(BF16)
        acc = acc + _dot(g, wo_ref[c * tf:(c + 1) * tf, :])
    o_ref[...] = x + 0.5 * acc


def _layer(l, *block):
    return pl.BlockSpec((None,) + block, lambda *_: (l,) + (0,) * len(block))


def _ffn(x2, l, gain, w_in, w_out):
    n, d = x2.shape
    d_ff = w_out.shape[1]
    tm, tf = FFN_TM, FFN_TF
    assert n % tm == 0 and d_ff % tf == 0
    return pl.pallas_call(
        functools.partial(_ffn_body, n_chunks=d_ff // tf, tf=tf),
        grid=(n // tm,),
        in_specs=[pl.BlockSpec((tm, d), lambda i: (i, 0)),
                  _layer(l, 1, d),
                  pl.BlockSpec((None, d, d_ff), lambda i: (l, 0, 0)),
                  pl.BlockSpec((None, d, d_ff), lambda i: (l, 0, 1)),
                  _layer(l, d_ff, d)],
        out_specs=pl.BlockSpec((tm, d), lambda i: (i, 0)),
        out_shape=jax.ShapeDtypeStruct((n, d), F32),
        compiler_params=_cparams("arbitrary"),
        name="ffn",
    )(x2, gain, w_in, w_in, w_out)


_C_AQ, _C_AK, _C_AV = 0, 256, 512
_C_FQ, _C_FK, _C_FV = 768, 1024, 1280
_C_NQ = 1536
_C_KSW = 1792
_C_VSW = 1920
_C_KVC = 2048
_C_UV = 2176
_C_SMALL = 2688
_C_TOTAL = 2816
_GATE_COL = 16
_AUG_HI, _AUG_MID, _AUG_LO, _AUG_ONE = 0, 4, 8, 12


def _proj_body(x_ref, g_ref, w_ref, g32_ref, g64_ref, gains_ref, fbias_ref,
               aqT_ref, ak_ref, avT_ref, fqT_ref, fk_ref, fvT_ref, faug_ref, cT_ref,
               nqT_ref, ksw_ref, vswT_ref, kvc_ref, u_ref, vn_ref, ngT_ref,
               carry_ref, *, tiles_per_seq):
    i = pl.program_id(0)
    x = x_ref[...]
    tm = x.shape[0]
    h = (_rms_rows(x) * g_ref[...]).astype(BF16)
    z = _dot(h, w_ref[...])
    g32 = g32_ref[...]
    g64 = g64_ref[...]
    gains = gains_ref[...]

    def sec(c0, w=256):
        return z[:, c0:c0 + w]

    aq = _group_norm(sec(_C_AQ), g32, gains[0:1]) * (DIFF_QK_DIM ** -0.5 * LOG2E)
    aqT_ref[0] = aq.T.astype(BF16)
    ak_ref[...] = _group_norm(sec(_C_AK), g32, gains[1:2]).astype(BF16)
    avT_ref[0] = sec(_C_AV).T.astype(BF16)
    fq = _group_norm(sec(_C_FQ), g64, gains[2:3]) * (HEAD_DIM ** -0.5 * LOG2E)
    fqT_ref[0] = fq.T.astype(BF16)
    fk_ref[...] = _group_norm(sec(_C_FK), g64, gains[3:4]).astype(BF16)
    fvT_ref[0] = sec(_C_FV).T.astype(BF16)
    nq = _group_norm(sec(_C_NQ), g64, gains[4:5]) * (HEAD_DIM ** -0.5 * LOG2E)
    nqT_ref[0] = nq.T.astype(BF16)
    ksw_ref[...] = _group_norm(sec(_C_KSW, 128), g64[:128, :128], gains[5:6, :128]).astype(BF16)
    vswT_ref[0] = sec(_C_VSW, 128).T.astype(BF16)
    kvc_ref[...] = sec(_C_KVC, 128)
    guv = jax.nn.gelu(sec(_C_UV, 512))
    u_ref[...] = guv[:, :256]
    vn_ref[...] = _group_norm(guv[:, 256:], g64, gains[6:7]).astype(BF16)
    small = sec(_C_SMALL, 128)
    ngT_ref[0] = jax.nn.sigmoid(small).T[_GATE_COL:_GATE_COL + 16, :]
    t = small + fbias_ref[...]
    logf = jnp.minimum(t, 0.0) - jnp.log(1.0 + jnp.exp(-jnp.abs(t)))

    @pl.when(i % tiles_per_seq == 0)
    def _():
        carry_ref[...] = jnp.zeros_like(carry_ref)

    row = lax.broadcasted_iota(jnp.int32, (tm, tm), 0)
    col = lax.broadcasted_iota(jnp.int32, (tm, tm), 1)
    tril = jnp.where(row >= col, 1.0, 0.0).astype(BF16)
    csum = _dot_exact_lhs(tril, logf, 3) + carry_ref[0:1, :]
    carry_ref[...] = jnp.broadcast_to(csum[tm - 1:tm, :], carry_ref.shape)
    c2 = csum * LOG2E
    cT_ref[0] = c2.T[0:8, :]
    hi, mid, lo = _split3(c2)
    lane = lax.broadcasted_iota(jnp.int32, (tm, 128), 1)
    one = jnp.where(lane < _AUG_ONE + 3, 1.0, 0.0).astype(BF16)
    faug_ref[...] = jnp.where(lane < _AUG_MID, hi,
                              jnp.where(lane < _AUG_LO, mid, jnp.where(lane < _AUG_ONE, lo, one)))


def _proj(x2, l, gain, w_cat, g32, g64, gains, fbias, batch, seq):
    n, d = x2.shape
    tm = PROJ_TM
    assert seq % tm == 0
    tps = seq // tm
    nt = n // tm
    const = lambda i: (0, 0)
    row = lambda i: (i, 0)
    trn = lambda i: (i // tps, 0, i % tps)
    nat = lambda w, dt: (jax.ShapeDtypeStruct((n, w), dt), pl.BlockSpec((tm, w), row))
    tr = lambda r, dt: (jax.ShapeDtypeStruct((batch, r, seq), dt), pl.BlockSpec((1, r, tm), trn))
    outs = [
        tr(256, BF16), nat(256, BF16), tr(256, BF16),
        tr(256, BF16), nat(256, BF16), tr(256, BF16),
        nat(128, BF16), tr(8, F32),
        tr(256, BF16), nat(128, BF16), tr(128, BF16),
        nat(128, F32), nat(256, F32), nat(256, BF16), tr(16, F32),
    ]
    return pl.pallas_call(
        functools.partial(_proj_body, tiles_per_seq=tps),
        grid=(nt,),
        in_specs=[pl.BlockSpec((tm, d), row),
                  _layer(l, 1, d),
                  _layer(l, d, _C_TOTAL),
                  pl.BlockSpec((256, 256), const),
                  pl.BlockSpec((256, 256), const),
                  _layer(l, 8, 256),
                  _layer(l, 1, 128)],
        out_specs=[o[1] for o in outs],
        out_shape=[o[0] for o in outs],
        scratch_shapes=[pltpu.VMEM((8, 128), F32)],
        compiler_params=_cparams("arbitrary"),
        name="proj",
    )(x2, gain, w_cat, g32, g64, gains, fbias)


def _flash_causal(key_tile, val_tile, qs_ref, m_ref, l_ref, acc_ref, s_ref, *, q0, tq, tk, cb):
    cols = qs_ref.shape[1]
    nb = cols // cb
    n_full = q0 // tk
    m_ref[...] = jnp.full_like(m_ref, NEG)
    l_ref[...] = jnp.zeros_like(l_ref)
    acc_ref[...] = jnp.zeros_like(acc_ref)
    krow = lax.broadcasted_iota(jnp.int32, (tk, cb), 0)
    kcol = lax.broadcasted_iota(jnp.int32, (tk, cb), 1)
    row_minus_q = [krow - ((j * cb + kcol) & (tq - 1)) for j in range(nb)]
    no_mask = jnp.int32(2 ** 30)

    def scores(kk, j, thr):
        s = _dot(kk, qs_ref[:, j * cb:(j + 1) * cb])
        if thr is not None:
            s = jnp.where(row_minus_q[j][0:s.shape[0]] <= thr, s, NEG)
        return s

    s_ref[...] = scores(key_tile(0, tk), 0, jnp.where(n_full == 0, q0, no_mask))

    def step(ki, masked, next_masked, final=False, rows=tk):
        k0 = pl.multiple_of(ki * tk, tk)
        kk = key_tile(k0, rows)
        s = s_ref[0:rows, :]
        for j in range(nb):
            cs = slice(j * cb, (j + 1) * cb)
            s_next = None
            if j + 1 < nb:
                s_next = scores(kk, j + 1, q0 - k0 if masked else None)
            elif not final:
                k1 = pl.multiple_of(k0 + tk, tk)
                s_ref[...] = scores(key_tile(k1, tk), 0, q0 - k1 if next_masked else None)
            m_old = m_ref[:, cs]
            m_new = jnp.maximum(m_old, jnp.max(s, axis=0, keepdims=True))
            alpha = jnp.exp2(m_old - m_new)
            p = jnp.exp2(s - m_new)
            l_ref[:, cs] = alpha * l_ref[:, cs] + jnp.sum(p, axis=0, keepdims=True)
            acc_ref[:, cs] = alpha * acc_ref[:, cs] + _dot(val_tile(k0, j, rows), p.astype(BF16))
            m_ref[:, cs] = m_new
            s = s_next

    def loop_body(ki, carry):
        step(ki, False, False)
        return carry

    lax.fori_loop(0, n_full - 1, loop_body, 0)

    @pl.when(n_full > 0)
    def _():
        step(n_full - 1, False, True)

    if tq >= tk:
        n_diag = tq // tk
        for dt in range(n_diag):
            step(n_full + dt, True, True, final=dt == n_diag - 1)
    else:
        for rows in range(tq, tk + 1, tq):
            @pl.when(q0 - n_full * tk + tq == rows)
            def _(rows=rows):
                step(n_full, True, True, final=True, rows=rows)
    return acc_ref[...] / jnp.maximum(l_ref[...], 1e-30)


def _attn_body(*refs, mode, tq, tk, cb, lam_init):
    if mode == "fox":
        qT_ref, k_ref, vT_ref, aug_ref, cT_ref, o_ref, qs_ref, m_ref, l_ref, acc_ref, s_ref = refs
    else:
        qT_ref, k_ref, vT_ref, lam_ref, g64_ref, o_ref, qs_ref, m_ref, l_ref, acc_ref, s_ref = refs
    nv = 2 if mode == "fox" else 4
    wv = 128 // nv
    hp = pl.program_id(1)
    qi = pl.program_id(2)
    q0 = qi * tq

    qT = qT_ref[0]
    rowi = lax.broadcasted_iota(jnp.int32, (128, tq), 0)
    zero = jnp.zeros_like(qT)
    for vh in range(nv):
        keep = (rowi >= vh * wv) & (rowi < (vh + 1) * wv)
        qs_ref[0:128, vh * tq:(vh + 1) * tq] = jnp.where(keep, qT, zero)
        if mode == "fox":
            hd = 2 * hp + vh
            hi, mid, lo = _split3(cT_ref[0, pl.ds(hd, 1), :])
            neg = (rowi == _AUG_HI + hd) | (rowi == _AUG_MID + hd) | (rowi == _AUG_LO + hd)
            aug = jnp.where(neg, -1.0, 0.0).astype(BF16)
            aug = jnp.where(rowi == _AUG_ONE, hi, aug)
            aug = jnp.where(rowi == _AUG_ONE + 1, mid, aug)
            aug = jnp.where(rowi == _AUG_ONE + 2, lo, aug)
            qs_ref[128:256, vh * tq:(vh + 1) * tq] = aug

    def key_tile(k0, rows):
        kk = k_ref[0, pl.ds(k0, rows), :]
        if mode == "fox":
            kk = jnp.concatenate([kk, aug_ref[0, pl.ds(k0, rows), :]], axis=1)
        return kk

    def val_tile(k0, j, rows):
        hd = ((j * cb) // tq) // (nv // 2)
        return vT_ref[0, hd * HEAD_DIM:(hd + 1) * HEAD_DIM, pl.ds(k0, rows)]

    o = _flash_causal(key_tile, val_tile, qs_ref, m_ref, l_ref, acc_ref, s_ref,
                      q0=q0, tq=tq, tk=tk, cb=cb)
    if mode == "fox":
        out_t = jnp.concatenate([o[:, 0:tq], o[:, tq:2 * tq]], axis=0)
    else:
        lp = lam_ref[...]
        lam = (jnp.exp(jnp.sum(lp[0:1] * lp[1:2], axis=-1, keepdims=True))
               - jnp.exp(jnp.sum(lp[2:3] * lp[3:4], axis=-1, keepdims=True)) + lam_init)
        d0 = o[:, 0:tq] - lam * o[:, tq:2 * tq]
        d1 = o[:, 2 * tq:3 * tq] - lam * o[:, 3 * tq:4 * tq]
        dd = jnp.concatenate([d0, d1], axis=0)
        ms = _dot_exact_lhs(g64_ref[...], dd * dd, 2)
        out_t = dd * lax.rsqrt(ms + RMS_EPS) * (1.0 - lam_init)
    o_ref[0] = out_t.T.astype(o_ref.dtype)


def _attn(mode, qT, k, vT, extra, batch, seq, lam_init=0.0):
    nv = 2 if mode == "fox" else 4
    tq = ATTN_COLS // nv
    tk, cb = ATTN_TK, ATTN_CB
    assert (tk % tq == 0 or tq % tk == 0) and seq % tk == 0 and (tq % cb == 0 or cb % tq == 0)
    assert (tq & (tq - 1)) == 0
    k3 = k.reshape(batch, seq, 256)
    in_specs = [pl.BlockSpec((1, 128, tq), lambda b, hp, qi: (b, hp, qi)),
                pl.BlockSpec((1, seq, 128), lambda b, hp, qi: (b, 0, hp)),
                pl.BlockSpec((1, 128, seq), lambda b, hp, qi: (b, hp, 0))]
    if mode == "fox":
        faug, cT = extra
        args = (qT, k3, vT, faug.reshape(batch, seq, 128), cT)
        in_specs += [pl.BlockSpec((1, seq, 128), lambda b, hp, qi: (b, 0, 0)),
                     pl.BlockSpec((1, 8, tq), lambda b, hp, qi: (b, 0, qi))]
        depth = 256
    else:
        l, lam_p, g64s = extra
        args = (qT, k3, vT, lam_p, g64s)
        in_specs += [_layer(l, 4, DIFF_QK_DIM),
                     pl.BlockSpec((128, 128), lambda b, hp, qi: (0, 0))]
        depth = 128
    cols = nv * tq
    out = pl.pallas_call(
        functools.partial(_attn_body, mode=mode, tq=tq, tk=tk, cb=cb, lam_init=lam_init),
        grid=(batch, 2, seq // tq),
        in_specs=in_specs,
        out_specs=pl.BlockSpec((1, tq, 128), lambda b, hp, qi: (b, qi, hp)),
        out_shape=jax.ShapeDtypeStruct((batch, seq, 256), BF16),
        scratch_shapes=[pltpu.VMEM((depth, cols), BF16),
                        pltpu.VMEM((1, cols), F32),
                        pltpu.VMEM((1, cols), F32),
                        pltpu.VMEM((HEAD_DIM, cols), F32),
                        pltpu.VMEM((tk, cb), F32)],
        compiler_params=_cparams("arbitrary", "arbitrary", "arbitrary"),
        name="attn_" + mode,
    )(*args)
    return out.reshape(batch * seq, 256)


def _compress_body(x_ref, pe_ref, w1a_ref, w1b_ref, b1_ref, w2_ref, b2_ref, g64_ref, kg_ref,
                   kc_ref, vcT_ref):
    x = x_ref[0]
    nblk = x.shape[0]
    a = _dot((x + pe_ref[0:1, :]).astype(BF16), w1a_ref[...])
    b = _dot((x + pe_ref[1:2, :]).astype(BF16), w1b_ref[...])
    pre = a + pltpu.roll(b, nblk - 1, 0) + b1_ref[...]
    hid = jax.nn.gelu(pre).astype(BF16)
    out = _dot(hid, w2_ref[...]) + b2_ref[...]
    kc_ref[0] = _group_norm(out[:, 0:128], g64_ref[...], kg_ref[...]).astype(BF16)
    vcT_ref[0] = out[:, 128:256].T[0:HEAD_DIM, :].astype(BF16)


def _compress(kvc, l, pe_ext, w1a, w1b, b1, w2, b2, g64s, kgain, batch, seq):
    nblk = seq // CMP_STRIDE
    x = kvc.reshape(batch, nblk, CMP_STRIDE * 128)
    const = lambda b: (0, 0)
    return pl.pallas_call(
        _compress_body,
        grid=(batch,),
        in_specs=[pl.BlockSpec((1, nblk, CMP_STRIDE * 128), lambda b: (b, 0, 0)),
                  _layer(l, 2, CMP_STRIDE * 128),
                  _layer(l, *w1a.shape[1:]),
                  _layer(l, *w1b.shape[1:]),
                  _layer(l, *b1.shape[1:]),
                  _layer(l, *w2.shape[1:]),
                  _layer(l, *b2.shape[1:]),
                  pl.BlockSpec((128, 128), const),
                  _layer(l, 1, 128)],
        out_specs=[pl.BlockSpec((1, nblk, 128), lambda b: (b, 0, 0)),
                   pl.BlockSpec((1, HEAD_DIM, nblk), lambda b: (b, 0, 0))],
        out_shape=[jax.ShapeDtypeStruct((batch, nblk, 128), BF16),
                   jax.ShapeDtypeStruct((batch, HEAD_DIM, nblk), BF16)],
        compiler_params=_cparams("arbitrary"),
        name="nsa_compress",
    )(x, pe_ext, w1a, w1b, b1, w2, b2, g64s, kgain)


def _topk_select(imp, k):
    nsel, tq = imp.shape

    def count(mask):
        return jnp.sum(jnp.where(mask, 1.0, 0.0), axis=0, keepdims=True)

    kf = float(k)
    bits = jnp.zeros((1, tq), jnp.int32)
    for bit in range(30, -1, -1):
        cand = bits + (1 << bit)
        if bit < 23:
            cand = jnp.where(bits != 0, cand, 0x7F800000)
        ok = count(imp >= lax.bitcast_convert_type(cand, F32)) >= kf
        bits = jnp.where(ok, cand, bits)
    kth = jnp.where(count(imp >= 0.0) >= kf, lax.bitcast_convert_type(bits, F32), -1.0)
    gt = imp > kth
    eq = imp == kth
    need = kf - count(gt)
    r = lax.broadcasted_iota(jnp.int32, (nsel, nsel), 0)
    c = lax.broadcasted_iota(jnp.int32, (nsel, nsel), 1)
    tril = jnp.where(r >= c, 1.0, 0.0).astype(BF16)
    prefix_eq = _dot(tril, jnp.where(eq, 1.0, 0.0).astype(BF16))
    return gt | (eq & (prefix_eq <= need))


def _nsa_body(nqT_ref, kc_ref, vcT_ref, ksw_ref, vswT_ref, hot_ref, ngT_ref, o_ref,
              qs_ref, qw_ref, m_ref, l_ref, acc_ref, s_ref, *, tq, tk, cb, seq):
    qi = pl.program_id(1)
    q0 = qi * tq
    cols = N_HEADS * tq
    nblk = kc_ref.shape[1]
    nsel = seq // SLC_BLOCK
    nb = cols // cb

    zero = jnp.zeros((HEAD_DIM, tq), BF16)
    for hd in range(N_HEADS):
        qh = nqT_ref[0, hd * HEAD_DIM:(hd + 1) * HEAD_DIM, :]
        cs = slice(hd * tq, (hd + 1) * tq)
        qs_ref[0:HEAD_DIM, cs] = qh
        qs_ref[HEAD_DIM:128, cs] = zero
        qw_ref[0:HEAD_DIM, cs] = zero
        qw_ref[HEAD_DIM:128, cs] = qh

    def qoff(shape):
        return lax.broadcasted_iota(jnp.int32, shape, 1) & (tq - 1)

    kc = kc_ref[0]
    vcT = vcT_ref[0]
    cend_minus_q = (lax.broadcasted_iota(jnp.int32, (nblk, cb), 0) * CMP_STRIDE
                    + (CMP_BLOCK - 1) - qoff((nblk, cb)))
    o_cmp, p_sum = [], None
    for j in range(nb):
        s = _dot(kc, qs_ref[0:128, j * cb:(j + 1) * cb])
        s = jnp.where(cend_minus_q <= q0, s, -jnp.inf)
        m = jnp.max(s, axis=0, keepdims=True)
        m = jnp.where(m == -jnp.inf, 0.0, m)
        e = jnp.exp2(s - m)
        p = e / jnp.maximum(jnp.sum(e, axis=0, keepdims=True), 1e-30)
        o_cmp.append(_dot(vcT, p.astype(BF16)))
        for c0 in range(0, cb, tq):
            ph = p[:, c0:c0 + tq]
            p_sum = ph if p_sum is None else p_sum + ph
    o_cmp = jnp.concatenate(o_cmp, axis=1)

    jj = lax.broadcasted_iota(jnp.int32, (nsel, nblk), 0)
    cc = lax.broadcasted_iota(jnp.int32, (nsel, nblk), 1)
    ov_t = jnp.where((cc * CMP_STRIDE < (jj + 1) * SLC_BLOCK)
                     & (cc * CMP_STRIDE + CMP_BLOCK > jj * SLC_BLOCK), 1.0, 0.0).astype(BF16)
    imp = _dot_exact_lhs(ov_t, p_sum, 3)
    sj = lax.broadcasted_iota(jnp.int32, (nsel, tq), 0)
    qpos_l = q0 + lax.broadcasted_iota(jnp.int32, (nsel, tq), 1)
    cur = qpos_l >> 6
    forced = (sj == 0) | (sj == cur) | (sj == cur - 1)
    valid = sj * SLC_BLOCK <= qpos_l
    imp = jnp.where(forced, FORCE_SCORE, jnp.where(valid, imp, -1.0))
    sel = _topk_select(imp, min(TOPK, nsel))
    bias = jnp.where(sel, 0.0, -(2.0 ** 30)).astype(BF16)
    if nsel < MAX_SEL_BLOCKS:
        bias = jnp.concatenate([bias, jnp.zeros((MAX_SEL_BLOCKS - nsel, tq), BF16)], axis=0)
    for hd in range(N_HEADS):
        qs_ref[128:256, hd * tq:(hd + 1) * tq] = bias

    key_minus_q = lax.broadcasted_iota(jnp.int32, (tq, cb), 0) - qoff((tq, cb))

    def window_half(k0, own):
        kw = ksw_ref[0, pl.ds(k0, tq), :]
        vw = vswT_ref[0, HEAD_DIM:128, pl.ds(k0, tq)]
        keep = key_minus_q <= 0 if own else key_minus_q > 0
        s_next = _dot(kw, qw_ref[:, 0:cb])
        for j in range(nb):
            cs = slice(j * cb, (j + 1) * cb)
            s = jnp.where(keep, s_next, NEG)
            if j + 1 < nb:
                s_next = _dot(kw, qw_ref[:, (j + 1) * cb:(j + 2) * cb])
            if own:
                m_new = jnp.max(s, axis=0, keepdims=True)
                p = jnp.exp2(s - m_new)
                l_ref[:, cs] = jnp.sum(p, axis=0, keepdims=True)
                acc_ref[:, cs] = _dot(vw, p.astype(BF16))
            else:
                m_old = m_ref[:, cs]
                m_new = jnp.maximum(m_old, jnp.max(s, axis=0, keepdims=True))
                alpha = jnp.exp2(m_old - m_new)
                p = jnp.exp2(s - m_new)
                l_ref[:, cs] = alpha * l_ref[:, cs] + jnp.sum(p, axis=0, keepdims=True)
                acc_ref[:, cs] = alpha * acc_ref[:, cs] + _dot(vw, p.astype(BF16))
            m_ref[:, cs] = m_new

    window_half(pl.multiple_of(q0, tq), True)

    @pl.when(qi > 0)
    def _():
        window_half(pl.multiple_of(q0 - WINDOW, tq), False)

    o_win = acc_ref[...] / jnp.maximum(l_ref[...], 1e-30)

    def key_tile(k0, rows):
        return jnp.concatenate([ksw_ref[0, pl.ds(k0, rows), :], hot_ref[pl.ds(k0, rows), :]], axis=1)

    def val_tile(k0, j, rows):
        return vswT_ref[0, 0:HEAD_DIM, pl.ds(k0, rows)]

    o_sel = _flash_causal(key_tile, val_tile, qs_ref, m_ref, l_ref, acc_ref, s_ref,
                          q0=q0, tq=tq, tk=tk, cb=cb)

    gates = ngT_ref[0]
    out_t = []
    for hd in range(N_HEADS):
        cs = slice(hd * tq, (hd + 1) * tq)
        out_t.append(gates[3 * hd:3 * hd + 1] * o_cmp[:, cs]
                     + gates[3 * hd + 1:3 * hd + 2] * o_sel[:, cs]
                     + gates[3 * hd + 2:3 * hd + 3] * o_win[:, cs])
    o_ref[0] = jnp.concatenate(out_t, axis=0).T.astype(o_ref.dtype)


def _nsa(nqT, kc, vcT, ksw, vswT, hot, ngT, batch, seq):
    tq, tk, cb = NSA_TQ, ATTN_TK, ATTN_CB
    cols = N_HEADS * tq
    nblk = seq // CMP_STRIDE
    assert seq % tk == 0 and (tk % tq == 0 or tq % tk == 0) and cb % tq == 0 and cols % cb == 0
    assert tq == WINDOW and seq % tq == 0 and (tq & (tq - 1)) == 0
    assert seq // SLC_BLOCK <= MAX_SEL_BLOCKS
    full = lambda b, qi: (b, 0, 0)
    out = pl.pallas_call(
        functools.partial(_nsa_body, tq=tq, tk=tk, cb=cb, seq=seq),
        grid=(batch, seq // tq),
        in_specs=[pl.BlockSpec((1, 256, tq), lambda b, qi: (b, 0, qi)),
                  pl.BlockSpec((1, nblk, 128), full),
                  pl.BlockSpec((1, HEAD_DIM, nblk), full),
                  pl.BlockSpec((1, seq, 128), full),
                  pl.BlockSpec((1, 128, seq), full),
                  pl.BlockSpec((seq, MAX_SEL_BLOCKS), lambda b, qi: (0, 0)),
                  pl.BlockSpec((1, 16, tq), lambda b, qi: (b, 0, qi))],
        out_specs=pl.BlockSpec((1, tq, 256), lambda b, qi: (b, qi, 0)),
        out_shape=jax.ShapeDtypeStruct((batch, seq, 256), BF16),
        scratch_shapes=[pltpu.VMEM((256, cols), BF16),
                        pltpu.VMEM((128, cols), BF16),
                        pltpu.VMEM((1, cols), F32),
                        pltpu.VMEM((1, cols), F32),
                        pltpu.VMEM((HEAD_DIM, cols), F32),
                        pltpu.VMEM((tk, cb), F32)],
        compiler_params=_cparams("arbitrary", "arbitrary"),
        name="nsa",
    )(nqT, kc, vcT, ksw.reshape(batch, seq, 128), vswT, hot, ngT)
    return out.reshape(batch * seq, 256)


def _merge_body(x_ref, g_ref, oa_ref, ob_ref, oc_ref, u_ref, vn_ref, ws_ref, bt_ref,
                wg_ref, wbr_ref, wout_ref, o_ref):
    x = x_ref[...]
    tm, d = x.shape
    h = (_rms_rows(x) * g_ref[...]).astype(BF16)

    r = lax.broadcasted_iota(jnp.int32, (GMLP_CHUNK, GMLP_CHUNK), 0)
    c = lax.broadcasted_iota(jnp.int32, (GMLP_CHUNK, GMLP_CHUNK), 1)
    lane_grp = lax.broadcasted_iota(jnp.int32, (GMLP_CHUNK, W_MIX), 1) // HEAD_DIM
    w_tril = [jnp.where(r >= c, ws_ref[g], 0.0).astype(BF16) for g in range(N_HEADS)]
    od = []
    for ch in range(tm // GMLP_CHUNK):
        sl = slice(ch * GMLP_CHUNK, (ch + 1) * GMLP_CHUNK)
        vch = vn_ref[sl, :]
        sv = jnp.zeros((GMLP_CHUNK, W_MIX), F32)
        for g in range(N_HEADS):
            sv = jnp.where(lane_grp == g, _dot(w_tril[g], vch), sv)
        od.append(u_ref[sl, :] * (sv + bt_ref[...]))
    o_d = jnp.concatenate(od, axis=0).astype(BF16)

    branches = (oa_ref[...], ob_ref[...], oc_ref[...], o_d)
    acc = jnp.zeros_like(x)
    for n in range(4):
        gate = jax.nn.sigmoid(_dot(h, wg_ref[:, n * d:(n + 1) * d]))
        acc = acc + gate * _dot(branches[n], wbr_ref[n])
    o_ref[...] = x + _dot(acc.astype(BF16), wout_ref[...])


def _merge(x2, l, gain, oa, ob, oc, u, vn, w_s, bt, wg, wbr, wout):
    n, d = x2.shape
    tm = MERGE_TM
    row = lambda i: (i, 0)
    return pl.pallas_call(
        _merge_body,
        grid=(n // tm,),
        in_specs=[pl.BlockSpec((tm, d), row),
                  _layer(l, 1, d),
                  pl.BlockSpec((tm, 256), row),
                  pl.BlockSpec((tm, 256), row),
                  pl.BlockSpec((tm, 256), row),
                  pl.BlockSpec((tm, 256), row),
                  pl.BlockSpec((tm, 256), row),
                  _layer(l, N_HEADS, GMLP_CHUNK, GMLP_CHUNK),
                  _layer(l, GMLP_CHUNK, W_MIX),
                  _layer(l, d, 4 * d),
                  _layer(l, 4, W_MIX, d),
                  _layer(l, d, d)],
        out_specs=pl.BlockSpec((tm, d), row),
        out_shape=jax.ShapeDtypeStruct((n, d), F32),
        compiler_params=_cparams("arbitrary"),
        name="merge",
    )(x2, gain, oa, ob, oc, u, vn, w_s, bt, wg, wbr, wout)


def _block_diag_mean(width, group):
    idx = jnp.arange(width) // group
    return ((idx[:, None] == idx[None, :]).astype(F32) / float(group)).astype(BF16)


def _stage_proj_weight(w_in):
    nkv = 1796
    w_in = w_in.astype(BF16)
    lead = w_in.shape[:-1]
    flog = w_in[..., 1536:1540]
    cols = [w_in[..., 0:1536],
            w_in[..., 1540:1796],
            w_in[..., nkv + 128:nkv + 192],
            w_in[..., nkv + 256:nkv + 320],
            w_in[..., nkv + 192:nkv + 256],
            w_in[..., nkv + 320:nkv + 384],
            w_in[..., nkv:nkv + 128],
            w_in[..., 2192:2704],
            flog, flog, flog,
            jnp.zeros(lead + (_GATE_COL - 12,), w_in.dtype),
            w_in[..., 2180:2192],
            jnp.zeros(lead + (128 - _GATE_COL - 12,), w_in.dtype)]
    w = jnp.concatenate(cols, axis=-1)
    assert w.shape[-1] == _C_TOTAL
    return w.astype(BF16), w_in[..., 2704:].astype(BF16)


def _stage_compress(pe, w1, b1, w2, b2):
    hid = w1.shape[-1]
    half = CMP_STRIDE
    pe_ext = jnp.concatenate([pe[0], pe[1]], axis=-1)
    pe_ext = jnp.stack([pe_ext[:half].reshape(-1), pe_ext[half:].reshape(-1)])
    w1k = w1[0].reshape(CMP_BLOCK, HEAD_DIM, hid)
    w1v = w1[1].reshape(CMP_BLOCK, HEAD_DIM, hid)
    zk = jnp.zeros_like(w1k)
    top = jnp.concatenate([w1k, zk], axis=-1)
    bot = jnp.concatenate([zk, w1v], axis=-1)
    w1e = jnp.concatenate([top, bot], axis=1)
    w1a = w1e[:half].reshape(half * 128, 2 * hid).astype(BF16)
    w1b = w1e[half:].reshape(half * 128, 2 * hid).astype(BF16)
    b1e = jnp.concatenate([b1[0], b1[1]]).reshape(1, 2 * hid)
    z = jnp.zeros((hid, HEAD_DIM), F32)
    w2e = jnp.concatenate([jnp.concatenate([w2[0], z, z, z], axis=1),
                           jnp.concatenate([z, z, w2[1], z], axis=1)], axis=0).astype(BF16)
    zb = jnp.zeros((HEAD_DIM,), F32)
    b2e = jnp.concatenate([b2[0], zb, b2[1], zb]).reshape(1, 256)
    return pe_ext, w1a, w1b, b1e, w2e, b2e


def kernel(x, ffn1_norm, ffn1_w_in, ffn1_w_out, mix_norm, w_in, diff_q_gain, diff_k_gain, diff_lambda, fox_q_gain, fox_k_gain, fox_f_bias, nsa_q_gain, nsa_k_gain, nsa_cmp_pe, nsa_phi_w1, nsa_phi_b1, nsa_phi_w2, nsa_phi_b2, gmlp_v_gain, gmlp_w_s, gmlp_b_s, w_branch, w_out, ffn2_norm, ffn2_w_in, ffn2_w_out):
    batch, seq, d = x.shape
    depth = w_in.shape[0]
    n = batch * seq
    g32 = _block_diag_mean(256, DIFF_QK_DIM)
    g64 = _block_diag_mean(256, HEAD_DIM)
    g64s = g64[:128, :128]
    hot = (jnp.arange(seq)[:, None] // SLC_BLOCK == jnp.arange(MAX_SEL_BLOCKS)[None, :]).astype(BF16)

    norm3 = lambda g: g.reshape(depth, 1, d).astype(F32)
    ffn1_g, ffn2_g, mix_g = norm3(ffn1_norm), norm3(ffn2_norm), norm3(mix_norm)
    ffn1_wi, ffn1_wo = ffn1_w_in.astype(BF16), ffn1_w_out.astype(BF16)
    ffn2_wi, ffn2_wo = ffn2_w_in.astype(BF16), ffn2_w_out.astype(BF16)
    w_cat, w_gate = _stage_proj_weight(w_in)
    tile = lambda g, r: jnp.tile(g.astype(F32), (1, r))
    gains = jnp.stack([tile(diff_q_gain, 8), tile(diff_k_gain, 8), tile(fox_q_gain, 4),
                       tile(fox_k_gain, 4), tile(nsa_q_gain, 4), tile(nsa_k_gain, 4),
                       gmlp_v_gain.astype(F32), jnp.zeros((depth, 256), F32)], axis=1)
    fb = fox_f_bias.astype(F32)
    fbias = jnp.concatenate([fb, fb, fb, jnp.zeros((depth, 116), F32)], axis=1).reshape(depth, 1, 128)
    lam_p = diff_lambda.astype(F32)
    cmp_w = jax.vmap(_stage_compress)(nsa_cmp_pe, nsa_phi_w1, nsa_phi_b1, nsa_phi_w2, nsa_phi_b2)
    kgain = tile(nsa_k_gain, 2).reshape(depth, 1, 128)
    bt = jnp.repeat(jnp.swapaxes(gmlp_b_s, 1, 2), HEAD_DIM, axis=2).astype(F32)
    w_s = gmlp_w_s.astype(F32)
    w_br, w_o = w_branch.astype(BF16), w_out.astype(BF16)

    x2 = x.reshape(n, d).astype(F32)
    for l in range(depth):
        x2 = _ffn(x2, l, ffn1_g, ffn1_wi, ffn1_wo)
        (aqT, ak, avT, fqT, fk, fvT, faug, cT, nqT, ksw, vswT, kvc, u, vn, ngT) = _proj(
            x2, l, mix_g, w_cat, g32, g64, gains, fbias, batch, seq)

        lam_init = 0.8 - 0.6 * math.exp(-0.3 * l)
        o_a = _attn("diff", aqT, ak, avT, (l, lam_p, g64s), batch, seq, lam_init)
        o_b = _attn("fox", fqT, fk, fvT, (faug, cT), batch, seq)
        kc, vcT = _compress(kvc, l, *cmp_w, g64s, kgain, batch, seq)
        o_c = _nsa(nqT, kc, vcT, ksw, vswT, hot, ngT, batch, seq)

        x2 = _merge(x2, l, mix_g, o_a, o_b, o_c, u, vn, w_s, bt, w_gate, w_br, w_o)
        x2 = _ffn(x2, l, ffn2_g, ffn2_wi, ffn2_wo)
    return x2.reshape(batch, seq, d).astype(x.dtype)
```

```python
import functools
import math

import jax
import jax.numpy as jnp
from jax import lax
from jax.experimental import pallas as pl
from jax.experimental.pallas import tpu as pltpu

F32 = jnp.float32
BF16 = jnp.bfloat16

HEAD_DIM = 64
N_HEADS = 4
W_MIX = N_HEADS * HEAD_DIM
DIFF_QK_DIM = HEAD_DIM // 2
CMP_BLOCK = 32
CMP_STRIDE = 16
SLC_BLOCK = 64
TOPK = 16
WINDOW = 512
FORCE_SCORE = 1.0e4
GMLP_CHUNK = 128
RMS_EPS = 1e-6
NEG = -1.0e30
LOG2E = math.log2(math.e)
INT_MIN = -2 ** 31

VMEM_LIMIT_BYTES = 56 * 1024 * 1024

FFN_TM = 512
FFN_TF = 256
PROJ_TM = 512
MERGE_TM = 512
ATTN_COLS = 2048
ATTN_CB = 512
ATTN_TK = 1024
NSA_TQ = 512
MAX_SEL_BLOCKS = 128
SCORE_BOUND = 40.0
BOUND_MARGIN = 1.02


def _dot(a, b):
    return jnp.dot(a, b, preferred_element_type=F32)


def _split3(c):
    hi = c.astype(BF16)
    r1 = c - hi.astype(F32)
    mid = r1.astype(BF16)
    lo = (r1 - mid.astype(F32)).astype(BF16)
    return hi, mid, lo


def _dot_exact_lhs(a, x, terms):
    return sum(_dot(a, part) for part in _split3(x)[:terms])


def _dot_exact_rhs(x, a, terms):
    return sum(_dot(part, a) for part in _split3(x)[:terms])


def _rms_rows(x):
    ms = jnp.mean(x * x, axis=-1, keepdims=True)
    return x * lax.rsqrt(ms + RMS_EPS)


def _group_norm(z, gmat, gain):
    ms = _dot_exact_rhs(z * z, gmat, 2)
    return z * lax.rsqrt(ms + RMS_EPS) * gain


def _cparams(*sem):
    return pltpu.CompilerParams(dimension_semantics=sem, vmem_limit_bytes=VMEM_LIMIT_BYTES)


def _ffn_body(x_ref, g_ref, wa_ref, wb_ref, wo_ref, o_ref, *, n_chunks, tf):
    x = x_ref[...]
    h = (_rms_rows(x) * g_ref[...]).astype(BF16)
    acc = jnp.zeros_like(x)
    for c in range(n_chunks):
        a = _dot(h, wa_ref[:, c * tf:(c + 1) * tf])
        b = _dot(h, wb_ref[:, c * tf:(c + 1) * tf])
        g = (a * jax.nn.sigmoid(a) * b).astype(BF16)
        acc = acc + _dot(g, wo_ref[c * tf:(c + 1) * tf, :])
    o_ref[...] = x + 0.5 * acc


def _layer(l, *block):
    return pl.BlockSpec((None,) + block, lambda *_: (l,) + (0,) * len(block))


def _ffn(x2, l, gain, w_in, w_out):
    n, d = x2.shape
    d_ff = w_out.shape[1]
    tm, tf = FFN_TM, FFN_TF
    assert n % tm == 0 and d_ff % tf == 0
    return pl.pallas_call(
        functools.partial(_ffn_body, n_chunks=d_ff // tf, tf=tf),
        grid=(n // tm,),
        in_specs=[pl.BlockSpec((tm, d), lambda i: (i, 0)),
                  _layer(l, 1, d),
                  pl.BlockSpec((None, d, d_ff), lambda i: (l, 0, 0)),
                  pl.BlockSpec((None, d, d_ff), lambda i: (l, 0, 1)),
                  _layer(l, d_ff, d)],
        out_specs=pl.BlockSpec((tm, d), lambda i: (i, 0)),
        out_shape=jax.ShapeDtypeStruct((n, d), F32),
        compiler_params=_cparams("arbitrary"),
        name="ffn",
    )(x2, gain, w_in, w_in, w_out)


_C_AQ, _C_AK, _C_AV = 0, 256, 512
_C_FQ, _C_FK, _C_FV = 768, 1024, 1280
_C_NQ = 1536
_C_KSW = 1792
_C_VSW = 1920
_C_KVC = 2048
_C_UV = 2176
_C_SMALL = 2688
_C_TOTAL = 2816
_GATE_COL = 16
_AUG_HI, _AUG_MID, _AUG_LO, _AUG_ONE = 0, 4, 8, 12


def _proj_body(x_ref, g_ref, w_ref, g32_ref, g64_ref, gains_ref, fbias_ref,
               aqT_ref, ak_ref, avT_ref, fqT_ref, fk_ref, fvT_ref, faug_ref, cT_ref,
               nqT_ref, ksw_ref, vswT_ref, kvc_ref, u_ref, vn_ref, ngT_ref,
               carry_ref, *, tiles_per_seq):
    i = pl.program_id(0)
    x = x_ref[...]
    tm = x.shape[0]
    h = (_rms_rows(x) * g_ref[...]).astype(BF16)
    z = _dot(h, w_ref[...])
    g32 = g32_ref[...]
    g64 = g64_ref[...]
    gains = gains_ref[...]

    def sec(c0, w=256):
        return z[:, c0:c0 + w]

    aq = _group_norm(sec(_C_AQ), g32, gains[0:1]) * (DIFF_QK_DIM ** -0.5 * LOG2E)
    aqT_ref[0] = aq.T.astype(BF16)
    ak_ref[...] = _group_norm(sec(_C_AK), g32, gains[1:2]).astype(BF16)
    avT_ref[0] = sec(_C_AV).T.astype(BF16)
    fq = _group_norm(sec(_C_FQ), g64, gains[2:3]) * (HEAD_DIM ** -0.5 * LOG2E)
    fqT_ref[0] = fq.T.astype(BF16)
    fk_ref[...] = _group_norm(sec(_C_FK), g64, gains[3:4]).astype(BF16)
    fvT_ref[0] = sec(_C_FV).T.astype(BF16)
    nq = _group_norm(sec(_C_NQ), g64, gains[4:5]) * (HEAD_DIM ** -0.5 * LOG2E)
    nqT_ref[0] = nq.T.astype(BF16)
    ksw_ref[...] = _group_norm(sec(_C_KSW, 128), g64[:128, :128], gains[5:6, :128]).astype(BF16)
    vswT_ref[0] = sec(_C_VSW, 128).T.astype(BF16)
    kvc_ref[...] = sec(_C_KVC, 128)
    guv = jax.nn.gelu(sec(_C_UV, 512))
    u_ref[...] = guv[:, :256]
    vn_ref[...] = _group_norm(guv[:, 256:], g64, gains[6:7]).astype(BF16)
    small = sec(_C_SMALL, 128)
    ngT_ref[0] = jax.nn.sigmoid(small).T[_GATE_COL:_GATE_COL + 16, :]
    t = small + fbias_ref[...]
    logf = jnp.minimum(t, 0.0) - jnp.log(1.0 + jnp.exp(-jnp.abs(t)))

    @pl.when(i % tiles_per_seq == 0)
    def _():
        carry_ref[...] = jnp.zeros_like(carry_ref)

    row = lax.broadcasted_iota(jnp.int32, (tm, tm), 0)
    col = lax.broadcasted_iota(jnp.int32, (tm, tm), 1)
    tril = jnp.where(row >= col, 1.0, 0.0).astype(BF16)
    csum = _dot_exact_lhs(tril, logf, 3) + carry_ref[0:1, :]
    carry_ref[...] = jnp.broadcast_to(csum[tm - 1:tm, :], carry_ref.shape)
    c2 = csum * LOG2E
    cT_ref[0] = c2.T[0:8, :]
    hi, mid, lo = _split3(c2)
    lane = lax.broadcasted_iota(jnp.int32, (tm, 128), 1)
    one = jnp.where(lane < _AUG_ONE + 3, 1.0, 0.0).astype(BF16)
    faug_ref[...] = jnp.where(lane < _AUG_MID, hi,
                              jnp.where(lane < _AUG_LO, mid, jnp.where(lane < _AUG_ONE, lo, one)))


def _proj(x2, l, gain, w_cat, g32, g64, gains, fbias, batch, seq):
    n, d = x2.shape
    tm = PROJ_TM
    assert seq % tm == 0
    tps = seq // tm
    nt = n // tm
    const = lambda i: (0, 0)
    row = lambda i: (i, 0)
    trn = lambda i: (i // tps, 0, i % tps)
    nat = lambda w, dt: (jax.ShapeDtypeStruct((n, w), dt), pl.BlockSpec((tm, w), row))
    tr = lambda r, dt: (jax.ShapeDtypeStruct((batch, r, seq), dt), pl.BlockSpec((1, r, tm), trn))
    outs = [
        tr(256, BF16), nat(256, BF16), tr(256, BF16),
        tr(256, BF16), nat(256, BF16), tr(256, BF16),
        nat(128, BF16), tr(8, F32),
        tr(256, BF16), nat(128, BF16), tr(128, BF16),
        nat(128, F32), nat(256, F32), nat(256, BF16), tr(16, F32),
    ]
    return pl.pallas_call(
        functools.partial(_proj_body, tiles_per_seq=tps),
        grid=(nt,),
        in_specs=[pl.BlockSpec((tm, d), row),
                  _layer(l, 1, d),
                  _layer(l, d, _C_TOTAL),
                  pl.BlockSpec((256, 256), const),
                  pl.BlockSpec((256, 256), const),
                  _layer(l, 8, 256),
                  _layer(l, 1, 128)],
        out_specs=[o[1] for o in outs],
        out_shape=[o[0] for o in outs],
        scratch_shapes=[pltpu.VMEM((8, 128), F32)],
        compiler_params=_cparams("arbitrary"),
        name="proj",
    )(x2, gain, w_cat, g32, g64, gains, fbias)


def _flash_causal(key_tile, val_tile, qs_ref, m_ref, l_ref, acc_ref, s_ref, *, q0, tq, tk, cb,
                  bounded):
    cols = qs_ref.shape[1]
    nb = cols // cb
    n_full = q0 // tk
    m_ref[...] = jnp.full_like(m_ref, NEG)
    l_ref[...] = jnp.zeros_like(l_ref)
    acc_ref[...] = jnp.zeros_like(acc_ref)
    krow = lax.broadcasted_iota(jnp.int32, (tk, cb), 0)
    kcol = lax.broadcasted_iota(jnp.int32, (tk, cb), 1)
    row_minus_q = [krow - ((j * cb + kcol) & (tq - 1)) for j in range(nb)]
    no_mask = jnp.int32(2 ** 30)

    def scores(kk, j, thr):
        s = _dot(kk, qs_ref[:, j * cb:(j + 1) * cb])
        if thr is not None:
            s = jnp.where(row_minus_q[j][0:s.shape[0]] <= thr, s, NEG)
        return s

    def carried(kk, j, thr):
        s = scores(kk, j, thr)
        if not bounded:
            return s
        p = jnp.exp2(s)
        l_ref[:, j * cb:(j + 1) * cb] += jnp.sum(p, axis=0, keepdims=True)
        return p.astype(BF16)

    s_ref[...] = carried(key_tile(0, tk), 0, jnp.where(n_full == 0, q0, no_mask))

    def step(ki, masked, next_masked, final=False, rows=tk):
        k0 = pl.multiple_of(ki * tk, tk)
        kk = key_tile(k0, rows)
        s = s_ref[0:rows, :]
        for j in range(nb):
            cs = slice(j * cb, (j + 1) * cb)
            s_next = None
            if j + 1 < nb:
                s_next = carried(kk, j + 1, q0 - k0 if masked else None)
            elif not final:
                k1 = pl.multiple_of(k0 + tk, tk)
                s_ref[...] = carried(key_tile(k1, tk), 0, q0 - k1 if next_masked else None)
            if bounded:
                acc_ref[:, cs] += _dot(val_tile(k0, j, rows), s)
            else:
                m_old = m_ref[:, cs]
                m_new = jnp.maximum(m_old, jnp.max(s, axis=0, keepdims=True))
                alpha = jnp.exp2(m_old - m_new)
                p = jnp.exp2(s - m_new)
                l_ref[:, cs] = alpha * l_ref[:, cs] + jnp.sum(p, axis=0, keepdims=True)
                acc_ref[:, cs] = (alpha * acc_ref[:, cs]
                                  + _dot(val_tile(k0, j, rows), p.astype(BF16)))
                m_ref[:, cs] = m_new
            s = s_next

    def loop_body(ki, carry):
        step(ki, False, False)
        return carry

    lax.fori_loop(0, n_full - 1, loop_body, 0)

    @pl.when(n_full > 0)
    def _():
        step(n_full - 1, False, True)

    if tq >= tk:
        n_diag = tq // tk
        for dt in range(n_diag):
            step(n_full + dt, True, True, final=dt == n_diag - 1)
    else:
        for rows in range(tq, tk + 1, tq):
            @pl.when(q0 - n_full * tk + tq == rows)
            def _(rows=rows):
                step(n_full, True, True, final=True, rows=rows)
    return acc_ref[...] / jnp.maximum(l_ref[...], 1e-30)


def _attn_body(*refs, mode, tq, tk, cb, lam_init, bounded):
    if mode == "fox":
        qT_ref, k_ref, vT_ref, aug_ref, cT_ref, o_ref, qs_ref, m_ref, l_ref, acc_ref, s_ref = refs
    else:
        qT_ref, k_ref, vT_ref, lam_ref, g64_ref, o_ref, qs_ref, m_ref, l_ref, acc_ref, s_ref = refs
    nv = 2 if mode == "fox" else 4
    wv = 128 // nv
    hp = pl.program_id(1)
    qi = pl.program_id(2)
    q0 = qi * tq

    qT = qT_ref[0]
    rowi = lax.broadcasted_iota(jnp.int32, (128, tq), 0)
    zero = jnp.zeros_like(qT)
    for vh in range(nv):
        keep = (rowi >= vh * wv) & (rowi < (vh + 1) * wv)
        qs_ref[0:128, vh * tq:(vh + 1) * tq] = jnp.where(keep, qT, zero)
        if mode == "fox":
            hd = 2 * hp + vh
            hi, mid, lo = _split3(cT_ref[0, pl.ds(hd, 1), :])
            neg = (rowi == _AUG_HI + hd) | (rowi == _AUG_MID + hd) | (rowi == _AUG_LO + hd)
            aug = jnp.where(neg, -1.0, 0.0).astype(BF16)
            aug = jnp.where(rowi == _AUG_ONE, hi, aug)
            aug = jnp.where(rowi == _AUG_ONE + 1, mid, aug)
            aug = jnp.where(rowi == _AUG_ONE + 2, lo, aug)
            qs_ref[128:256, vh * tq:(vh + 1) * tq] = aug

    def key_tile(k0, rows):
        kk = k_ref[0, pl.ds(k0, rows), :]
        if mode == "fox":
            kk = jnp.concatenate([kk, aug_ref[0, pl.ds(k0, rows), :]], axis=1)
        return kk

    def val_tile(k0, j, rows):
        hd = ((j * cb) // tq) // (nv // 2)
        return vT_ref[0, hd * HEAD_DIM:(hd + 1) * HEAD_DIM, pl.ds(k0, rows)]

    o = _flash_causal(key_tile, val_tile, qs_ref, m_ref, l_ref, acc_ref, s_ref,
                      q0=q0, tq=tq, tk=tk, cb=cb, bounded=bounded)
    if mode == "fox":
        out_t = jnp.concatenate([o[:, 0:tq], o[:, tq:2 * tq]], axis=0)
    else:
        lp = lam_ref[...]
        lam = (jnp.exp(jnp.sum(lp[0:1] * lp[1:2], axis=-1, keepdims=True))
               - jnp.exp(jnp.sum(lp[2:3] * lp[3:4], axis=-1, keepdims=True)) + lam_init)
        d0 = o[:, 0:tq] - lam * o[:, tq:2 * tq]
        d1 = o[:, 2 * tq:3 * tq] - lam * o[:, 3 * tq:4 * tq]
        dd = jnp.concatenate([d0, d1], axis=0)
        ms = _dot_exact_lhs(g64_ref[...], dd * dd, 2)
        out_t = dd * lax.rsqrt(ms + RMS_EPS) * (1.0 - lam_init)
    o_ref[0] = out_t.T.astype(o_ref.dtype)


def _attn(mode, qT, k, vT, extra, batch, seq, score_bound, lam_init=0.0):
    nv = 2 if mode == "fox" else 4
    tq = ATTN_COLS // nv
    tk, cb = ATTN_TK, ATTN_CB
    assert (tk % tq == 0 or tq % tk == 0) and seq % tk == 0 and (tq % cb == 0 or cb % tq == 0)
    assert (tq & (tq - 1)) == 0
    k3 = k.reshape(batch, seq, 256)
    in_specs = [pl.BlockSpec((1, 128, tq), lambda b, hp, qi: (b, hp, qi)),
                pl.BlockSpec((1, seq, 128), lambda b, hp, qi: (b, 0, hp)),
                pl.BlockSpec((1, 128, seq), lambda b, hp, qi: (b, hp, 0))]
    if mode == "fox":
        faug, cT = extra
        args = (qT, k3, vT, faug.reshape(batch, seq, 128), cT)
        in_specs += [pl.BlockSpec((1, seq, 128), lambda b, hp, qi: (b, 0, 0)),
                     pl.BlockSpec((1, 8, tq), lambda b, hp, qi: (b, 0, qi))]
        depth = 256
    else:
        l, lam_p, g64s = extra
        args = (qT, k3, vT, lam_p, g64s)
        in_specs += [_layer(l, 4, DIFF_QK_DIM),
                     pl.BlockSpec((128, 128), lambda b, hp, qi: (0, 0))]
        depth = 128
    cols = nv * tq
    def call(bounded):
        return pl.pallas_call(
            functools.partial(_attn_body, mode=mode, tq=tq, tk=tk, cb=cb, lam_init=lam_init,
                              bounded=bounded),
            grid=(batch, 2, seq // tq),
            in_specs=in_specs,
            out_specs=pl.BlockSpec((1, tq, 128), lambda b, hp, qi: (b, qi, hp)),
            out_shape=jax.ShapeDtypeStruct((batch, seq, 256), BF16),
            scratch_shapes=[pltpu.VMEM((depth, cols), BF16),
                            pltpu.VMEM((1, cols), F32),
                            pltpu.VMEM((1, cols), F32),
                            pltpu.VMEM((HEAD_DIM, cols), F32),
                            pltpu.VMEM((tk, cb), BF16 if bounded else F32)],
            compiler_params=_cparams("arbitrary", "arbitrary", "arbitrary"),
            name="attn_" + mode + ("_bounded" if bounded else ""),
        )(*args)

    out = lax.cond(score_bound <= SCORE_BOUND, lambda: call(True), lambda: call(False))
    return out.reshape(batch * seq, 256)


def _compress_body(x_ref, pe_ref, w1a_ref, w1b_ref, b1_ref, w2_ref, b2_ref, g64_ref, kg_ref,
                   kc_ref, vcT_ref):
    x = x_ref[0]
    nblk = x.shape[0]
    a = _dot((x + pe_ref[0:1, :]).astype(BF16), w1a_ref[...])
    b = _dot((x + pe_ref[1:2, :]).astype(BF16), w1b_ref[...])
    pre = a + pltpu.roll(b, nblk - 1, 0) + b1_ref[...]
    hid = jax.nn.gelu(pre).astype(BF16)
    out = _dot(hid, w2_ref[...]) + b2_ref[...]
    kc_ref[0] = _group_norm(out[:, 0:128], g64_ref[...], kg_ref[...]).astype(BF16)
    vcT_ref[0] = out[:, 128:256].T[0:HEAD_DIM, :].astype(BF16)


def _compress(kvc, l, pe_ext, w1a, w1b, b1, w2, b2, g64s, kgain, batch, seq):
    nblk = seq // CMP_STRIDE
    x = kvc.reshape(batch, nblk, CMP_STRIDE * 128)
    const = lambda b: (0, 0)
    return pl.pallas_call(
        _compress_body,
        grid=(batch,),
        in_specs=[pl.BlockSpec((1, nblk, CMP_STRIDE * 128), lambda b: (b, 0, 0)),
                  _layer(l, 2, CMP_STRIDE * 128),
                  _layer(l, *w1a.shape[1:]),
                  _layer(l, *w1b.shape[1:]),
                  _layer(l, *b1.shape[1:]),
                  _layer(l, *w2.shape[1:]),
                  _layer(l, *b2.shape[1:]),
                  pl.BlockSpec((128, 128), const),
                  _layer(l, 1, 128)],
        out_specs=[pl.BlockSpec((1, nblk, 128), lambda b: (b, 0, 0)),
                   pl.BlockSpec((1, HEAD_DIM, nblk), lambda b: (b, 0, 0))],
        out_shape=[jax.ShapeDtypeStruct((batch, nblk, 128), BF16),
                   jax.ShapeDtypeStruct((batch, HEAD_DIM, nblk), BF16)],
        compiler_params=_cparams("arbitrary"),
        name="nsa_compress",
    )(x, pe_ext, w1a, w1b, b1, w2, b2, g64s, kgain)


def _topk_select(imp, k):
    nsel, tq = imp.shape

    def count(mask):
        return jnp.sum(jnp.where(mask, 1.0, 0.0), axis=0, keepdims=True)

    kf = float(k)
    bits = jnp.zeros((1, tq), jnp.int32)
    for bit in range(30, -1, -1):
        cand = bits + (1 << bit)
        if bit < 23:
            cand = jnp.where(bits != 0, cand, 0x7F800000)
        ok = count(imp >= lax.bitcast_convert_type(cand, F32)) >= kf
        bits = jnp.where(ok, cand, bits)
    kth = jnp.where(count(imp >= 0.0) >= kf, lax.bitcast_convert_type(bits, F32), -1.0)
    gt = imp > kth
    eq = imp == kth
    need = kf - count(gt)
    r = lax.broadcasted_iota(jnp.int32, (nsel, nsel), 0)
    c = lax.broadcasted_iota(jnp.int32, (nsel, nsel), 1)
    tril = jnp.where(r >= c, 1.0, 0.0).astype(BF16)
    prefix_eq = _dot(tril, jnp.where(eq, 1.0, 0.0).astype(BF16))
    return gt | (eq & (prefix_eq <= need))


def _nsa_body(nqT_ref, kc_ref, vcT_ref, ksw_ref, vswT_ref, hot_ref, ngT_ref, o_ref,
              qs_ref, qw_ref, m_ref, l_ref, acc_ref, s_ref, *, tq, tk, cb, seq, bounded):
    qi = pl.program_id(1)
    q0 = qi * tq
    cols = N_HEADS * tq
    nblk = kc_ref.shape[1]
    nsel = seq // SLC_BLOCK
    nb = cols // cb

    zero = jnp.zeros((HEAD_DIM, tq), BF16)
    for hd in range(N_HEADS):
        qh = nqT_ref[0, hd * HEAD_DIM:(hd + 1) * HEAD_DIM, :]
        cs = slice(hd * tq, (hd + 1) * tq)
        qs_ref[0:HEAD_DIM, cs] = qh
        qs_ref[HEAD_DIM:128, cs] = zero
        qw_ref[0:HEAD_DIM, cs] = zero
        qw_ref[HEAD_DIM:128, cs] = qh

    def qoff(shape):
        return lax.broadcasted_iota(jnp.int32, shape, 1) & (tq - 1)

    kc = kc_ref[0]
    vcT = vcT_ref[0]
    cend_minus_q = (lax.broadcasted_iota(jnp.int32, (nblk, cb), 0) * CMP_STRIDE
                    + (CMP_BLOCK - 1) - qoff((nblk, cb)))
    o_cmp, p_sum = [], None
    for j in range(nb):
        s = _dot(kc, qs_ref[0:128, j * cb:(j + 1) * cb])
        s = jnp.where(cend_minus_q <= q0, s, -jnp.inf)
        if bounded:
            e = jnp.exp2(s)
        else:
            m = jnp.max(s, axis=0, keepdims=True)
            m = jnp.where(m == -jnp.inf, 0.0, m)
            e = jnp.exp2(s - m)
        p = e / jnp.maximum(jnp.sum(e, axis=0, keepdims=True), 1e-30)
        o_cmp.append(_dot(vcT, p.astype(BF16)))
        for c0 in range(0, cb, tq):
            ph = p[:, c0:c0 + tq]
            p_sum = ph if p_sum is None else p_sum + ph
    o_cmp = jnp.concatenate(o_cmp, axis=1)

    jj = lax.broadcasted_iota(jnp.int32, (nsel, nblk), 0)
    cc = lax.broadcasted_iota(jnp.int32, (nsel, nblk), 1)
    ov_t = jnp.where((cc * CMP_STRIDE < (jj + 1) * SLC_BLOCK)
                     & (cc * CMP_STRIDE + CMP_BLOCK > jj * SLC_BLOCK), 1.0, 0.0).astype(BF16)
    imp = _dot_exact_lhs(ov_t, p_sum, 3)
    sj = lax.broadcasted_iota(jnp.int32, (nsel, tq), 0)
    qpos_l = q0 + lax.broadcasted_iota(jnp.int32, (nsel, tq), 1)
    cur = qpos_l >> 6
    forced = (sj == 0) | (sj == cur) | (sj == cur - 1)
    valid = sj * SLC_BLOCK <= qpos_l
    imp = jnp.where(forced, FORCE_SCORE, jnp.where(valid, imp, -1.0))
    sel = _topk_select(imp, min(TOPK, nsel))
    bias = jnp.where(sel, 0.0, -(2.0 ** 30)).astype(BF16)
    if nsel < MAX_SEL_BLOCKS:
        bias = jnp.concatenate([bias, jnp.zeros((MAX_SEL_BLOCKS - nsel, tq), BF16)], axis=0)
    for hd in range(N_HEADS):
        qs_ref[128:256, hd * tq:(hd + 1) * tq] = bias

    key_minus_q = lax.broadcasted_iota(jnp.int32, (tq, cb), 0) - qoff((tq, cb))

    def window_half(k0, own):
        kw = ksw_ref[0, pl.ds(k0, tq), :]
        vw = vswT_ref[0, HEAD_DIM:128, pl.ds(k0, tq)]
        keep = key_minus_q <= 0 if own else key_minus_q > 0
        s_next = _dot(kw, qw_ref[:, 0:cb])
        for j in range(nb):
            cs = slice(j * cb, (j + 1) * cb)
            s = jnp.where(keep, s_next, NEG)
            if j + 1 < nb:
                s_next = _dot(kw, qw_ref[:, (j + 1) * cb:(j + 2) * cb])
            if bounded:
                p = jnp.exp2(s)
                lsum, pv = jnp.sum(p, axis=0, keepdims=True), _dot(vw, p.astype(BF16))
                l_ref[:, cs] = lsum if own else l_ref[:, cs] + lsum
                acc_ref[:, cs] = pv if own else acc_ref[:, cs] + pv
            elif own:
                m_new = jnp.max(s, axis=0, keepdims=True)
                p = jnp.exp2(s - m_new)
                l_ref[:, cs] = jnp.sum(p, axis=0, keepdims=True)
                acc_ref[:, cs] = _dot(vw, p.astype(BF16))
                m_ref[:, cs] = m_new
            else:
                m_old = m_ref[:, cs]
                m_new = jnp.maximum(m_old, jnp.max(s, axis=0, keepdims=True))
                alpha = jnp.exp2(m_old - m_new)
                p = jnp.exp2(s - m_new)
                l_ref[:, cs] = alpha * l_ref[:, cs] + jnp.sum(p, axis=0, keepdims=True)
                acc_ref[:, cs] = alpha * acc_ref[:, cs] + _dot(vw, p.astype(BF16))
                m_ref[:, cs] = m_new

    window_half(pl.multiple_of(q0, tq), True)

    @pl.when(qi > 0)
    def _():
        window_half(pl.multiple_of(q0 - WINDOW, tq), False)

    o_win = acc_ref[...] / jnp.maximum(l_ref[...], 1e-30)

    def key_tile(k0, rows):
        return jnp.concatenate([ksw_ref[0, pl.ds(k0, rows), :], hot_ref[pl.ds(k0, rows), :]], axis=1)

    def val_tile(k0, j, rows):
        return vswT_ref[0, 0:HEAD_DIM, pl.ds(k0, rows)]

    o_sel = _flash_causal(key_tile, val_tile, qs_ref, m_ref, l_ref, acc_ref, s_ref,
                          q0=q0, tq=tq, tk=tk, cb=cb, bounded=bounded)

    gates = ngT_ref[0]
    out_t = []
    for hd in range(N_HEADS):
        cs = slice(hd * tq, (hd + 1) * tq)
        out_t.append(gates[3 * hd:3 * hd + 1] * o_cmp[:, cs]
                     + gates[3 * hd + 1:3 * hd + 2] * o_sel[:, cs]
                     + gates[3 * hd + 2:3 * hd + 3] * o_win[:, cs])
    o_ref[0] = jnp.concatenate(out_t, axis=0).T.astype(o_ref.dtype)


def _nsa(nqT, kc, vcT, ksw, vswT, hot, ngT, batch, seq, score_bound):
    tq, tk, cb = NSA_TQ, ATTN_TK, ATTN_CB
    cols = N_HEADS * tq
    nblk = seq // CMP_STRIDE
    assert seq % tk == 0 and (tk % tq == 0 or tq % tk == 0) and cb % tq == 0 and cols % cb == 0
    assert tq == WINDOW and seq % tq == 0 and (tq & (tq - 1)) == 0
    assert seq // SLC_BLOCK <= MAX_SEL_BLOCKS
    full = lambda b, qi: (b, 0, 0)
    args = (nqT, kc, vcT, ksw.reshape(batch, seq, 128), vswT, hot, ngT)

    def call(bounded):
        return pl.pallas_call(
            functools.partial(_nsa_body, tq=tq, tk=tk, cb=cb, seq=seq, bounded=bounded),
            grid=(batch, seq // tq),
            in_specs=[pl.BlockSpec((1, 256, tq), lambda b, qi: (b, 0, qi)),
                      pl.BlockSpec((1, nblk, 128), full),
                      pl.BlockSpec((1, HEAD_DIM, nblk), full),
                      pl.BlockSpec((1, seq, 128), full),
                      pl.BlockSpec((1, 128, seq), full),
                      pl.BlockSpec((seq, MAX_SEL_BLOCKS), lambda b, qi: (0, 0)),
                      pl.BlockSpec((1, 16, tq), lambda b, qi: (b, 0, qi))],
            out_specs=pl.BlockSpec((1, tq, 256), lambda b, qi: (b, qi, 0)),
            out_shape=jax.ShapeDtypeStruct((batch, seq, 256), BF16),
            scratch_shapes=[pltpu.VMEM((256, cols), BF16),
                            pltpu.VMEM((128, cols), BF16),
                            pltpu.VMEM((1, cols), F32),
                            pltpu.VMEM((1, cols), F32),
                            pltpu.VMEM((HEAD_DIM, cols), F32),
                            pltpu.VMEM((tk, cb), BF16 if bounded else F32)],
            compiler_params=_cparams("arbitrary", "arbitrary"),
            name="nsa" + ("_bounded" if bounded else ""),
        )(*args)

    out = lax.cond(score_bound <= SCORE_BOUND, lambda: call(True), lambda: call(False))
    return out.reshape(batch * seq, 256)


def _merge_body(x_ref, g_ref, oa_ref, ob_ref, oc_ref, u_ref, vn_ref, ws_ref, bt_ref,
                wg_ref, wbr_ref, wout_ref, o_ref):
    x = x_ref[...]
    tm, d = x.shape
    h = (_rms_rows(x) * g_ref[...]).astype(BF16)

    r = lax.broadcasted_iota(jnp.int32, (GMLP_CHUNK, GMLP_CHUNK), 0)
    c = lax.broadcasted_iota(jnp.int32, (GMLP_CHUNK, GMLP_CHUNK), 1)
    lane_grp = lax.broadcasted_iota(jnp.int32, (GMLP_CHUNK, W_MIX), 1) // HEAD_DIM
    w_tril = [jnp.where(r >= c, ws_ref[g], 0.0).astype(BF16) for g in range(N_HEADS)]
    od = []
    for ch in range(tm // GMLP_CHUNK):
        sl = slice(ch * GMLP_CHUNK, (ch + 1) * GMLP_CHUNK)
        vch = vn_ref[sl, :]
        sv = jnp.zeros((GMLP_CHUNK, W_MIX), F32)
        for g in range(N_HEADS):
            sv = jnp.where(lane_grp == g, _dot(w_tril[g], vch), sv)
        od.append(u_ref[sl, :] * (sv + bt_ref[...]))
    o_d = jnp.concatenate(od, axis=0).astype(BF16)

    branches = (oa_ref[...], ob_ref[...], oc_ref[...], o_d)
    acc = jnp.zeros_like(x)
    for n in range(4):
        gate = jax.nn.sigmoid(_dot(h, wg_ref[:, n * d:(n + 1) * d]))
        acc = acc + gate * _dot(branches[n], wbr_ref[n])
    o_ref[...] = x + _dot(acc.astype(BF16), wout_ref[...])


def _merge(x2, l, gain, oa, ob, oc, u, vn, w_s, bt, wg, wbr, wout):
    n, d = x2.shape
    tm = MERGE_TM
    row = lambda i: (i, 0)
    return pl.pallas_call(
        _merge_body,
        grid=(n // tm,),
        in_specs=[pl.BlockSpec((tm, d), row),
                  _layer(l, 1, d),
                  pl.BlockSpec((tm, 256), row),
                  pl.BlockSpec((tm, 256), row),
                  pl.BlockSpec((tm, 256), row),
                  pl.BlockSpec((tm, 256), row),
                  pl.BlockSpec((tm, 256), row),
                  _layer(l, N_HEADS, GMLP_CHUNK, GMLP_CHUNK),
                  _layer(l, GMLP_CHUNK, W_MIX),
                  _layer(l, d, 4 * d),
                  _layer(l, 4, W_MIX, d),
                  _layer(l, d, d)],
        out_specs=pl.BlockSpec((tm, d), row),
        out_shape=jax.ShapeDtypeStruct((n, d), F32),
        compiler_params=_cparams("arbitrary"),
        name="merge",
    )(x2, gain, oa, ob, oc, u, vn, w_s, bt, wg, wbr, wout)


def _block_diag_mean(width, group):
    idx = jnp.arange(width) // group
    return ((idx[:, None] == idx[None, :]).astype(F32) / float(group)).astype(BF16)


def _stage_proj_weight(w_in):
    nkv = 1796
    w_in = w_in.astype(BF16)
    lead = w_in.shape[:-1]
    flog = w_in[..., 1536:1540]
    cols = [w_in[..., 0:1536],
            w_in[..., 1540:1796],
            w_in[..., nkv + 128:nkv + 192],
            w_in[..., nkv + 256:nkv + 320],
            w_in[..., nkv + 192:nkv + 256],
            w_in[..., nkv + 320:nkv + 384],
            w_in[..., nkv:nkv + 128],
            w_in[..., 2192:2704],
            flog, flog, flog,
            jnp.zeros(lead + (_GATE_COL - 12,), w_in.dtype),
            w_in[..., 2180:2192],
            jnp.zeros(lead + (128 - _GATE_COL - 12,), w_in.dtype)]
    w = jnp.concatenate(cols, axis=-1)
    assert w.shape[-1] == _C_TOTAL
    return w.astype(BF16), w_in[..., 2704:].astype(BF16)


def _stage_compress(pe, w1, b1, w2, b2):
    hid = w1.shape[-1]
    half = CMP_STRIDE
    pe_ext = jnp.concatenate([pe[0], pe[1]], axis=-1)
    pe_ext = jnp.stack([pe_ext[:half].reshape(-1), pe_ext[half:].reshape(-1)])
    w1k = w1[0].reshape(CMP_BLOCK, HEAD_DIM, hid)
    w1v = w1[1].reshape(CMP_BLOCK, HEAD_DIM, hid)
    zk = jnp.zeros_like(w1k)
    top = jnp.concatenate([w1k, zk], axis=-1)
    bot = jnp.concatenate([zk, w1v], axis=-1)
    w1e = jnp.concatenate([top, bot], axis=1)
    w1a = w1e[:half].reshape(half * 128, 2 * hid).astype(BF16)
    w1b = w1e[half:].reshape(half * 128, 2 * hid).astype(BF16)
    b1e = jnp.concatenate([b1[0], b1[1]]).reshape(1, 2 * hid)
    z = jnp.zeros((hid, HEAD_DIM), F32)
    w2e = jnp.concatenate([jnp.concatenate([w2[0], z, z, z], axis=1),
                           jnp.concatenate([z, z, w2[1], z], axis=1)], axis=0).astype(BF16)
    zb = jnp.zeros((HEAD_DIM,), F32)
    b2e = jnp.concatenate([b2[0], zb, b2[1], zb]).reshape(1, 256)
    return pe_ext, w1a, w1b, b1e, w2e, b2e


def kernel(x, ffn1_norm, ffn1_w_in, ffn1_w_out, mix_norm, w_in, diff_q_gain, diff_k_gain, diff_lambda, fox_q_gain, fox_k_gain, fox_f_bias, nsa_q_gain, nsa_k_gain, nsa_cmp_pe, nsa_phi_w1, nsa_phi_b1, nsa_phi_w2, nsa_phi_b2, gmlp_v_gain, gmlp_w_s, gmlp_b_s, w_branch, w_out, ffn2_norm, ffn2_w_in, ffn2_w_out):
    batch, seq, d = x.shape
    depth = w_in.shape[0]
    n = batch * seq
    g32 = _block_diag_mean(256, DIFF_QK_DIM)
    g64 = _block_diag_mean(256, HEAD_DIM)
    g64s = g64[:128, :128]
    hot = (jnp.arange(seq)[:, None] // SLC_BLOCK == jnp.arange(MAX_SEL_BLOCKS)[None, :]).astype(BF16)

    norm3 = lambda g: g.reshape(depth, 1, d).astype(F32)
    ffn1_g, ffn2_g, mix_g = norm3(ffn1_norm), norm3(ffn2_norm), norm3(mix_norm)
    ffn1_wi, ffn1_wo = ffn1_w_in.astype(BF16), ffn1_w_out.astype(BF16)
    ffn2_wi, ffn2_wo = ffn2_w_in.astype(BF16), ffn2_w_out.astype(BF16)
    w_cat, w_gate = _stage_proj_weight(w_in)
    tile = lambda g, r: jnp.tile(g.astype(F32), (1, r))
    gains = jnp.stack([tile(diff_q_gain, 8), tile(diff_k_gain, 8), tile(fox_q_gain, 4),
                       tile(fox_k_gain, 4), tile(nsa_q_gain, 4), tile(nsa_k_gain, 4),
                       gmlp_v_gain.astype(F32), jnp.zeros((depth, 256), F32)], axis=1)
    fb = fox_f_bias.astype(F32)
    fbias = jnp.concatenate([fb, fb, fb, jnp.zeros((depth, 116), F32)], axis=1).reshape(depth, 1, 128)
    lam_p = diff_lambda.astype(F32)
    cmp_w = jax.vmap(_stage_compress)(nsa_cmp_pe, nsa_phi_w1, nsa_phi_b1, nsa_phi_w2, nsa_phi_b2)
    kgain = tile(nsa_k_gain, 2).reshape(depth, 1, 128)
    bt = jnp.repeat(jnp.swapaxes(gmlp_b_s, 1, 2), HEAD_DIM, axis=2).astype(F32)
    w_s = gmlp_w_s.astype(F32)
    w_br, w_o = w_branch.astype(BF16), w_out.astype(BF16)
    amax = lambda g: jnp.max(jnp.abs(g.astype(F32)), axis=-1)
    bound = lambda gq, gk, grp: (grp ** 0.5 * LOG2E * BOUND_MARGIN) * amax(gq) * amax(gk)
    b_diff = bound(diff_q_gain, diff_k_gain, DIFF_QK_DIM)
    b_fox = bound(fox_q_gain, fox_k_gain, HEAD_DIM)
    b_nsa = bound(nsa_q_gain, nsa_k_gain, HEAD_DIM)

    x2 = x.reshape(n, d).astype(F32)
    for l in range(depth):
        x2 = _ffn(x2, l, ffn1_g, ffn1_wi, ffn1_wo)
        (aqT, ak, avT, fqT, fk, fvT, faug, cT, nqT, ksw, vswT, kvc, u, vn, ngT) = _proj(
            x2, l, mix_g, w_cat, g32, g64, gains, fbias, batch, seq)

        lam_init = 0.8 - 0.6 * math.exp(-0.3 * l)
        o_a = _attn("diff", aqT, ak, avT, (l, lam_p, g64s), batch, seq, b_diff[l], lam_init)
        o_b = _attn("fox", fqT, fk, fvT, (faug, cT), batch, seq, b_fox[l])
        kc, vcT = _compress(kvc, l, *cmp_w, g64s, kgain, batch, seq)
        o_c = _nsa(nqT, kc, vcT, ksw, vswT, hot, ngT, batch, seq, b_nsa[l])

        x2 = _merge(x2, l, mix_g, o_a, o_b, o_c, u, vn, w_s, bt, w_gate, w_br, w_o)
        x2 = _ffn(x2, l, ffn2_g, ffn2_wi, ffn2_wo)
    return x2.reshape(batch, seq, d).astype(x.dtype)
```

```python
import functools
import math

import jax
import jax.numpy as jnp
from jax import lax
from jax.experimental import pallas as pl
from jax.experimental.pallas import tpu as pltpu

F32 = jnp.float32
BF16 = jnp.bfloat16

HEAD_DIM = 64
N_HEADS = 4
W_MIX = N_HEADS * HEAD_DIM
DIFF_QK_DIM = HEAD_DIM // 2
CMP_BLOCK = 32
CMP_STRIDE = 16
SLC_BLOCK = 64
TOPK = 16
WINDOW = 512
FORCE_SCORE = 1.0e4
GMLP_CHUNK = 128
RMS_EPS = 1e-6
NEG = -1.0e30
LOG2E = math.log2(math.e)

VMEM_LIMIT_BYTES = 56 * 1024 * 1024

FFN_TM = 512
FFN_TF = 256
PROJ_TM = 512
MERGE_TM = 512
ATTN_COLS = 2048
ATTN_CB = 512
ATTN_TK = 1024
NSA_TQ = 512
MAX_SEL_BLOCKS = 128
SCORE_BOUND = 40.0
BOUND_MARGIN = 1.02


def _dot(a, b):
    return jnp.dot(a, b, preferred_element_type=F32)


def _split3(c):
    hi = c.astype(BF16)
    r1 = c - hi.astype(F32)
    mid = r1.astype(BF16)
    lo = (r1 - mid.astype(F32)).astype(BF16)
    return hi, mid, lo


def _dot_exact_lhs(a, x, terms):
    return sum(_dot(a, part) for part in _split3(x)[:terms])


def _dot_exact_rhs(x, a, terms):
    return sum(_dot(part, a) for part in _split3(x)[:terms])


def _rms_rows(x):
    ms = jnp.mean(x * x, axis=-1, keepdims=True)
    return x * lax.rsqrt(ms + RMS_EPS)


def _group_norm(z, gmat, gain):
    ms = _dot_exact_rhs(z * z, gmat, 2)
    return z * lax.rsqrt(ms + RMS_EPS) * gain


def _cparams(*sem):
    return pltpu.CompilerParams(dimension_semantics=sem, vmem_limit_bytes=VMEM_LIMIT_BYTES)


def _ffn_body(x_ref, g_ref, wa_ref, wb_ref, wo_ref, o_ref, *, n_chunks, tf):
    x = x_ref[...]
    h = (_rms_rows(x) * g_ref[...]).astype(BF16)
    acc = jnp.zeros_like(x)
    for c in range(n_chunks):
        a = _dot(h, wa_ref[:, c * tf:(c + 1) * tf])
        b = _dot(h, wb_ref[:, c * tf:(c + 1) * tf])
        g = (a * jax.nn.sigmoid(a) * b).astype(BF16)
        acc = acc + _dot(g, wo_ref[c * tf:(c + 1) * tf, :])
    o_ref[...] = x + 0.5 * acc


def _layer(l, *block):
    return pl.BlockSpec((None,) + block, lambda *_: (l,) + (0,) * len(block))


def _ffn(x2, l, gain, w_in, w_out):
    n, d = x2.shape
    d_ff = w_out.shape[1]
    tm, tf = FFN_TM, FFN_TF
    assert n % tm == 0 and d_ff % tf == 0
    return pl.pallas_call(
        functools.partial(_ffn_body, n_chunks=d_ff // tf, tf=tf),
        grid=(n // tm,),
        in_specs=[pl.BlockSpec((tm, d), lambda i: (i, 0)),
                  _layer(l, 1, d),
                  pl.BlockSpec((None, d, d_ff), lambda i: (l, 0, 0)),
                  pl.BlockSpec((None, d, d_ff), lambda i: (l, 0, 1)),
                  _layer(l, d_ff, d)],
        out_specs=pl.BlockSpec((tm, d), lambda i: (i, 0)),
        out_shape=jax.ShapeDtypeStruct((n, d), F32),
        compiler_params=_cparams("arbitrary"),
        name="ffn",
    )(x2, gain, w_in, w_in, w_out)


_C_AQ, _C_AK, _C_AV = 0, 256, 512
_C_FQ, _C_FK, _C_FV = 768, 1024, 1280
_C_NQ = 1536
_C_KSW = 1792
_C_VSW = 1920
_C_KVC = 2048
_C_UV = 2176
_C_SMALL = 2688
_C_TOTAL = 2816
_GATE_COL = 16
_AUG_HI, _AUG_MID, _AUG_LO, _AUG_ONE = 0, 4, 8, 12


def _proj_body(x_ref, g_ref, w_ref, g32_ref, g64_ref, gains_ref, fbias_ref,
               aqT_ref, ak_ref, avT_ref, fqT_ref, fk_ref, fvT_ref, faug_ref, cT_ref,
               nqT_ref, ksw_ref, vswT_ref, kvc_ref, u_ref, vn_ref, ngT_ref,
               carry_ref, *, tiles_per_seq):
    i = pl.program_id(0)
    x = x_ref[...]
    tm = x.shape[0]
    h = (_rms_rows(x) * g_ref[...]).astype(BF16)
    z = _dot(h, w_ref[...])
    g32 = g32_ref[...]
    g64 = g64_ref[...]
    gains = gains_ref[...]

    def sec(c0, w=256):
        return z[:, c0:c0 + w]

    aq = _group_norm(sec(_C_AQ), g32, gains[0:1]) * (DIFF_QK_DIM ** -0.5 * LOG2E)
    aqT_ref[0] = aq.T.astype(BF16)
    ak_ref[...] = _group_norm(sec(_C_AK), g32, gains[1:2]).astype(BF16)
    avT_ref[0] = sec(_C_AV).T.astype(BF16)
    fq = _group_norm(sec(_C_FQ), g64, gains[2:3]) * (HEAD_DIM ** -0.5 * LOG2E)
    fqT_ref[0] = fq.T.astype(BF16)
    fk_ref[...] = _group_norm(sec(_C_FK), g64, gains[3:4]).astype(BF16)
    fvT_ref[0] = sec(_C_FV).T.astype(BF16)
    nq = _group_norm(sec(_C_NQ), g64, gains[4:5]) * (HEAD_DIM ** -0.5 * LOG2E)
    nqT_ref[0] = nq.T.astype(BF16)
    ksw_ref[...] = _group_norm(sec(_C_KSW, 128), g64[:128, :128], gains[5:6, :128]).astype(BF16)
    vswT_ref[0] = sec(_C_VSW, 128).T.astype(BF16)
    kvc_ref[...] = sec(_C_KVC, 128)
    guv = jax.nn.gelu(sec(_C_UV, 512))
    u_ref[...] = guv[:, :256]
    vn_ref[...] = _group_norm(guv[:, 256:], g64, gains[6:7]).astype(BF16)
    small = sec(_C_SMALL, 128)
    ngT_ref[0] = jax.nn.sigmoid(small).T[_GATE_COL:_GATE_COL + 16, :]
    t = small + fbias_ref[...]
    logf = jnp.minimum(t, 0.0) - jnp.log(1.0 + jnp.exp(-jnp.abs(t)))

    @pl.when(i % tiles_per_seq == 0)
    def _():
        carry_ref[...] = jnp.zeros_like(carry_ref)

    row = lax.broadcasted_iota(jnp.int32, (tm, tm), 0)
    col = lax.broadcasted_iota(jnp.int32, (tm, tm), 1)
    tril = jnp.where(row >= col, 1.0, 0.0).astype(BF16)
    csum = _dot_exact_lhs(tril, logf, 3) + carry_ref[0:1, :]
    carry_ref[...] = jnp.broadcast_to(csum[tm - 1:tm, :], carry_ref.shape)
    c2 = csum * LOG2E
    cT_ref[0] = c2.T[0:8, :]
    hi, mid, lo = _split3(c2)
    lane = lax.broadcasted_iota(jnp.int32, (tm, 128), 1)
    one = jnp.where(lane < _AUG_ONE + 3, 1.0, 0.0).astype(BF16)
    faug_ref[...] = jnp.where(lane < _AUG_MID, hi,
                              jnp.where(lane < _AUG_LO, mid, jnp.where(lane < _AUG_ONE, lo, one)))


def _proj(x2, l, gain, w_cat, g32, g64, gains, fbias, batch, seq):
    n, d = x2.shape
    tm = PROJ_TM
    assert seq % tm == 0
    tps = seq // tm
    nt = n // tm
    const = lambda i: (0, 0)
    row = lambda i: (i, 0)
    trn = lambda i: (i // tps, 0, i % tps)
    nat = lambda w, dt: (jax.ShapeDtypeStruct((n, w), dt), pl.BlockSpec((tm, w), row))
    tr = lambda r, dt: (jax.ShapeDtypeStruct((batch, r, seq), dt), pl.BlockSpec((1, r, tm), trn))
    outs = [
        tr(256, BF16), nat(256, BF16), tr(256, BF16),
        tr(256, BF16), nat(256, BF16), tr(256, BF16),
        nat(128, BF16), tr(8, F32),
        tr(256, BF16), nat(128, BF16), tr(128, BF16),
        nat(128, F32), nat(256, F32), nat(256, BF16), tr(16, F32),
    ]
    return pl.pallas_call(
        functools.partial(_proj_body, tiles_per_seq=tps),
        grid=(nt,),
        in_specs=[pl.BlockSpec((tm, d), row),
                  _layer(l, 1, d),
                  _layer(l, d, _C_TOTAL),
                  pl.BlockSpec((256, 256), const),
                  pl.BlockSpec((256, 256), const),
                  _layer(l, 8, 256),
                  _layer(l, 1, 128)],
        out_specs=[o[1] for o in outs],
        out_shape=[o[0] for o in outs],
        scratch_shapes=[pltpu.VMEM((8, 128), F32)],
        compiler_params=_cparams("arbitrary"),
        name="proj",
    )(x2, gain, w_cat, g32, g64, gains, fbias)


def _flash_causal(key_tile, val_tile, qs_ref, m_ref, l_ref, acc_ref, s_ref, *, q0, tq, tk, cb,
                  bounded):
    cols = qs_ref.shape[1]
    nb = cols // cb
    n_full = q0 // tk
    m_ref[...] = jnp.full_like(m_ref, NEG)
    l_ref[...] = jnp.zeros_like(l_ref)
    acc_ref[...] = jnp.zeros_like(acc_ref)
    krow = lax.broadcasted_iota(jnp.int32, (tk, cb), 0)
    kcol = lax.broadcasted_iota(jnp.int32, (tk, cb), 1)
    row_minus_q = [krow - ((j * cb + kcol) & (tq - 1)) for j in range(nb)]
    no_mask = jnp.int32(2 ** 30)

    def scores(kk, j, thr):
        s = _dot(kk, qs_ref[:, j * cb:(j + 1) * cb])
        if thr is not None:
            s = jnp.where(row_minus_q[j][0:s.shape[0]] <= thr, s, NEG)
        return s

    def carried(kk, j, thr):
        s = scores(kk, j, thr)
        if not bounded:
            return s
        p = jnp.exp2(s)
        l_ref[:, j * cb:(j + 1) * cb] += jnp.sum(p, axis=0, keepdims=True)
        return p.astype(BF16)

    s_ref[...] = carried(key_tile(0, tk), 0, jnp.where(n_full == 0, q0, no_mask))

    def step(ki, masked, next_masked, final=False, rows=tk):
        k0 = pl.multiple_of(ki * tk, tk)
        kk = key_tile(k0, rows)
        s = s_ref[0:rows, :]
        for j in range(nb):
            cs = slice(j * cb, (j + 1) * cb)
            s_next = None
            if j + 1 < nb:
                s_next = carried(kk, j + 1, q0 - k0 if masked else None)
            elif not final:
                k1 = pl.multiple_of(k0 + tk, tk)
                s_ref[...] = carried(key_tile(k1, tk), 0, q0 - k1 if next_masked else None)
            if bounded:
                acc_ref[:, cs] += _dot(val_tile(k0, j, rows), s)
            else:
                m_old = m_ref[:, cs]
                m_new = jnp.maximum(m_old, jnp.max(s, axis=0, keepdims=True))
                alpha = jnp.exp2(m_old - m_new)
                p = jnp.exp2(s - m_new)
                l_ref[:, cs] = alpha * l_ref[:, cs] + jnp.sum(p, axis=0, keepdims=True)
                acc_ref[:, cs] = (alpha * acc_ref[:, cs]
                                  + _dot(val_tile(k0, j, rows), p.astype(BF16)))
                m_ref[:, cs] = m_new
            s = s_next

    def loop_body(ki, carry):
        step(ki, False, False)
        return carry

    lax.fori_loop(0, n_full - 1, loop_body, 0)

    @pl.when(n_full > 0)
    def _():
        step(n_full - 1, False, True)

    if tq >= tk:
        n_diag = tq // tk
        for dt in range(n_diag):
            step(n_full + dt, True, True, final=dt == n_diag - 1)
    else:
        for rows in range(tq, tk + 1, tq):
            @pl.when(q0 - n_full * tk + tq == rows)
            def _(rows=rows):
                step(n_full, True, True, final=True, rows=rows)
    return acc_ref[...] / jnp.maximum(l_ref[...], 1e-30)


def _attn_body(*refs, mode, tq, tk, cb, lam_init, bounded):
    if mode == "fox":
        qT_ref, k_ref, vT_ref, aug_ref, cT_ref, o_ref, qs_ref, m_ref, l_ref, acc_ref, s_ref = refs
    else:
        qT_ref, k_ref, vT_ref, lam_ref, g64_ref, o_ref, qs_ref, m_ref, l_ref, acc_ref, s_ref = refs
    nv = 2 if mode == "fox" else 4
    wv = 128 // nv
    hp = pl.program_id(1)
    qi = pl.program_id(2)
    q0 = qi * tq

    qT = qT_ref[0]
    rowi = lax.broadcasted_iota(jnp.int32, (128, tq), 0)
    zero = jnp.zeros_like(qT)
    for vh in range(nv):
        keep = (rowi >= vh * wv) & (rowi < (vh + 1) * wv)
        qs_ref[0:128, vh * tq:(vh + 1) * tq] = jnp.where(keep, qT, zero)
        if mode == "fox":
            hd = 2 * hp + vh
            hi, mid, lo = _split3(cT_ref[0, pl.ds(hd, 1), :])
            neg = (rowi == _AUG_HI + hd) | (rowi == _AUG_MID + hd) | (rowi == _AUG_LO + hd)
            aug = jnp.where(neg, -1.0, 0.0).astype(BF16)
            aug = jnp.where(rowi == _AUG_ONE, hi, aug)
            aug = jnp.where(rowi == _AUG_ONE + 1, mid, aug)
            aug = jnp.where(rowi == _AUG_ONE + 2, lo, aug)
            qs_ref[128:256, vh * tq:(vh + 1) * tq] = aug

    def key_tile(k0, rows):
        kk = k_ref[0, pl.ds(k0, rows), :]
        if mode == "fox":
            kk = jnp.concatenate([kk, aug_ref[0, pl.ds(k0, rows), :]], axis=1)
        return kk

    def val_tile(k0, j, rows):
        hd = ((j * cb) // tq) // (nv // 2)
        return vT_ref[0, hd * HEAD_DIM:(hd + 1) * HEAD_DIM, pl.ds(k0, rows)]

    o = _flash_causal(key_tile, val_tile, qs_ref, m_ref, l_ref, acc_ref, s_ref,
                      q0=q0, tq=tq, tk=tk, cb=cb, bounded=bounded)
    if mode == "fox":
        out_t = jnp.concatenate([o[:, 0:tq], o[:, tq:2 * tq]], axis=0)
    else:
        lp = lam_ref[...]
        lam = (jnp.exp(jnp.sum(lp[0:1] * lp[1:2], axis=-1, keepdims=True))
               - jnp.exp(jnp.sum(lp[2:3] * lp[3:4], axis=-1, keepdims=True)) + lam_init)
        d0 = o[:, 0:tq] - lam * o[:, tq:2 * tq]
        d1 = o[:, 2 * tq:3 * tq] - lam * o[:, 3 * tq:4 * tq]
        dd = jnp.concatenate([d0, d1], axis=0)
        ms = _dot_exact_lhs(g64_ref[...], dd * dd, 2)
        out_t = dd * lax.rsqrt(ms + RMS_EPS) * (1.0 - lam_init)
    o_ref[0] = out_t.T.astype(o_ref.dtype)


def _attn(mode, qT, k, vT, extra, batch, seq, score_bound, lam_init=0.0):
    nv = 2 if mode == "fox" else 4
    tq = ATTN_COLS // nv
    tk, cb = ATTN_TK, ATTN_CB
    assert (tk % tq == 0 or tq % tk == 0) and seq % tk == 0 and (tq % cb == 0 or cb % tq == 0)
    assert (tq & (tq - 1)) == 0
    k3 = k.reshape(batch, seq, 256)
    in_specs = [pl.BlockSpec((1, 128, tq), lambda b, hp, qi: (b, hp, qi)),
                pl.BlockSpec((1, seq, 128), lambda b, hp, qi: (b, 0, hp)),
                pl.BlockSpec((1, 128, seq), lambda b, hp, qi: (b, hp, 0))]
    if mode == "fox":
        faug, cT = extra
        args = (qT, k3, vT, faug.reshape(batch, seq, 128), cT)
        in_specs += [pl.BlockSpec((1, seq, 128), lambda b, hp, qi: (b, 0, 0)),
                     pl.BlockSpec((1, 8, tq), lambda b, hp, qi: (b, 0, qi))]
        depth = 256
    else:
        l, lam_p, g64s = extra
        args = (qT, k3, vT, lam_p, g64s)
        in_specs += [_layer(l, 4, DIFF_QK_DIM),
                     pl.BlockSpec((128, 128), lambda b, hp, qi: (0, 0))]
        depth = 128
    cols = nv * tq
    def call(bounded):
        return pl.pallas_call(
            functools.partial(_attn_body, mode=mode, tq=tq, tk=tk, cb=cb, lam_init=lam_init,
                              bounded=bounded),
            grid=(batch, 2, seq // tq),
            in_specs=in_specs,
            out_specs=pl.BlockSpec((1, tq, 128), lambda b, hp, qi: (b, qi, hp)),
            out_shape=jax.ShapeDtypeStruct((batch, seq, 256), BF16),
            scratch_shapes=[pltpu.VMEM((depth, cols), BF16),
                            pltpu.VMEM((1, cols), F32),
                            pltpu.VMEM((1, cols), F32),
                            pltpu.VMEM((HEAD_DIM, cols), F32),
                            pltpu.VMEM((tk, cb), BF16 if bounded else F32)],
            compiler_params=_cparams("arbitrary", "arbitrary", "arbitrary"),
            name="attn_" + mode + ("_bounded" if bounded else ""),
        )(*args)

    out = lax.cond(score_bound <= SCORE_BOUND, lambda: call(True), lambda: call(False))
    return out.reshape(batch * seq, 256)


def _compress_body(x_ref, pe_ref, w1a_ref, w1b_ref, b1_ref, w2_ref, b2_ref, g64_ref, kg_ref,
                   kc_ref, vcT_ref):
    x = x_ref[0]
    nblk = x.shape[0]
    a = _dot((x + pe_ref[0:1, :]).astype(BF16), w1a_ref[...])
    b = _dot((x + pe_ref[1:2, :]).astype(BF16), w1b_ref[...])
    pre = a + pltpu.roll(b, nblk - 1, 0) + b1_ref[...]
    hid = jax.nn.gelu(pre).astype(BF16)
    out = _dot(hid, w2_ref[...]) + b2_ref[...]
    kc_ref[0] = _group_norm(out[:, 0:128], g64_ref[...], kg_ref[...]).astype(BF16)
    vcT_ref[0] = out[:, 128:256].T[0:HEAD_DIM, :].astype(BF16)


def _compress(kvc, l, pe_ext, w1a, w1b, b1, w2, b2, g64s, kgain, batch, seq):
    nblk = seq // CMP_STRIDE
    x = kvc.reshape(batch, nblk, CMP_STRIDE * 128)
    const = lambda b: (0, 0)
    return pl.pallas_call(
        _compress_body,
        grid=(batch,),
        in_specs=[pl.BlockSpec((1, nblk, CMP_STRIDE * 128), lambda b: (b, 0, 0)),
                  _layer(l, 2, CMP_STRIDE * 128),
                  _layer(l, *w1a.shape[1:]),
                  _layer(l, *w1b.shape[1:]),
                  _layer(l, *b1.shape[1:]),
                  _layer(l, *w2.shape[1:]),
                  _layer(l, *b2.shape[1:]),
                  pl.BlockSpec((128, 128), const),
                  _layer(l, 1, 128)],
        out_specs=[pl.BlockSpec((1, nblk, 128), lambda b: (b, 0, 0)),
                   pl.BlockSpec((1, HEAD_DIM, nblk), lambda b: (b, 0, 0))],
        out_shape=[jax.ShapeDtypeStruct((batch, nblk, 128), BF16),
                   jax.ShapeDtypeStruct((batch, HEAD_DIM, nblk), BF16)],
        compiler_params=_cparams("arbitrary"),
        name="nsa_compress",
    )(x, pe_ext, w1a, w1b, b1, w2, b2, g64s, kgain)


def _topk_select(imp, k):
    nsel, tq = imp.shape

    def count(mask):
        return jnp.sum(jnp.where(mask, 1.0, 0.0), axis=0, keepdims=True)

    kf = float(k)
    bits = jnp.zeros((1, tq), jnp.int32)
    for bit in range(30, -1, -1):
        cand = bits + (1 << bit)
        if bit < 23:
            cand = jnp.where(bits != 0, cand, 0x7F800000)
        ok = count(imp >= lax.bitcast_convert_type(cand, F32)) >= kf
        bits = jnp.where(ok, cand, bits)
    kth = jnp.where(count(imp >= 0.0) >= kf, lax.bitcast_convert_type(bits, F32), -1.0)
    gt = imp > kth
    eq = imp == kth
    need = kf - count(gt)
    r = lax.broadcasted_iota(jnp.int32, (nsel, nsel), 0)
    c = lax.broadcasted_iota(jnp.int32, (nsel, nsel), 1)
    tril = jnp.where(r >= c, 1.0, 0.0).astype(BF16)
    prefix_eq = _dot(tril, jnp.where(eq, 1.0, 0.0).astype(BF16))
    return gt | (eq & (prefix_eq <= need))


def _nsa_body(nqT_ref, kc_ref, vcT_ref, ksw_ref, vswT_ref, hot_ref, ngT_ref, o_ref,
              qs_ref, qw_ref, m_ref, l_ref, acc_ref, s_ref, *, tq, tk, cb, seq, bounded):
    qi = pl.program_id(1)
    q0 = qi * tq
    cols = N_HEADS * tq
    nblk = kc_ref.shape[1]
    nsel = seq // SLC_BLOCK
    nb = cols // cb

    zero = jnp.zeros((HEAD_DIM, tq), BF16)
    for hd in range(N_HEADS):
        qh = nqT_ref[0, hd * HEAD_DIM:(hd + 1) * HEAD_DIM, :]
        cs = slice(hd * tq, (hd + 1) * tq)
        qs_ref[0:HEAD_DIM, cs] = qh
        qs_ref[HEAD_DIM:128, cs] = zero
        qw_ref[0:HEAD_DIM, cs] = zero
        qw_ref[HEAD_DIM:128, cs] = qh

    def qoff(shape):
        return lax.broadcasted_iota(jnp.int32, shape, 1) & (tq - 1)

    kc = kc_ref[0]
    vcT = vcT_ref[0]
    cend_minus_q = (lax.broadcasted_iota(jnp.int32, (nblk, cb), 0) * CMP_STRIDE
                    + (CMP_BLOCK - 1) - qoff((nblk, cb)))
    o_cmp, p_sum = [], None
    for j in range(nb):
        s = _dot(kc, qs_ref[0:128, j * cb:(j + 1) * cb])
        s = jnp.where(cend_minus_q <= q0, s, -jnp.inf)
        if bounded:
            e = jnp.exp2(s)
        else:
            m = jnp.max(s, axis=0, keepdims=True)
            m = jnp.where(m == -jnp.inf, 0.0, m)
            e = jnp.exp2(s - m)
        p = e / jnp.maximum(jnp.sum(e, axis=0, keepdims=True), 1e-30)
        o_cmp.append(_dot(vcT, p.astype(BF16)))
        for c0 in range(0, cb, tq):
            ph = p[:, c0:c0 + tq]
            p_sum = ph if p_sum is None else p_sum + ph
    o_cmp = jnp.concatenate(o_cmp, axis=1)

    jj = lax.broadcasted_iota(jnp.int32, (nsel, nblk), 0)
    cc = lax.broadcasted_iota(jnp.int32, (nsel, nblk), 1)
    ov_t = jnp.where((cc * CMP_STRIDE < (jj + 1) * SLC_BLOCK)
                     & (cc * CMP_STRIDE + CMP_BLOCK > jj * SLC_BLOCK), 1.0, 0.0).astype(BF16)
    imp = _dot_exact_lhs(ov_t, p_sum, 3)
    sj = lax.broadcasted_iota(jnp.int32, (nsel, tq), 0)
    qpos_l = q0 + lax.broadcasted_iota(jnp.int32, (nsel, tq), 1)
    cur = qpos_l >> 6
    forced = (sj == 0) | (sj == cur) | (sj == cur - 1)
    valid = sj * SLC_BLOCK <= qpos_l
    imp = jnp.where(forced, FORCE_SCORE, jnp.where(valid, imp, -1.0))
    sel = _topk_select(imp, min(TOPK, nsel))
    bias = jnp.where(sel, 0.0, -(2.0 ** 30)).astype(BF16)
    if nsel < MAX_SEL_BLOCKS:
        bias = jnp.concatenate([bias, jnp.zeros((MAX_SEL_BLOCKS - nsel, tq), BF16)], axis=0)
    for hd in range(N_HEADS):
        qs_ref[128:256, hd * tq:(hd + 1) * tq] = bias

    key_minus_q = lax.broadcasted_iota(jnp.int32, (tq, cb), 0) - qoff((tq, cb))

    def window_half(k0, own):
        kw = ksw_ref[0, pl.ds(k0, tq), :]
        vw = vswT_ref[0, HEAD_DIM:128, pl.ds(k0, tq)]
        keep = key_minus_q <= 0 if own else key_minus_q > 0
        s_next = _dot(kw, qw_ref[:, 0:cb])
        for j in range(nb):
            cs = slice(j * cb, (j + 1) * cb)
            s = jnp.where(keep, s_next, NEG)
            if j + 1 < nb:
                s_next = _dot(kw, qw_ref[:, (j + 1) * cb:(j + 2) * cb])
            if bounded:
                p = jnp.exp2(s)
                lsum, pv = jnp.sum(p, axis=0, keepdims=True), _dot(vw, p.astype(BF16))
                l_ref[:, cs] = lsum if own else l_ref[:, cs] + lsum
                acc_ref[:, cs] = pv if own else acc_ref[:, cs] + pv
            elif own:
                m_new = jnp.max(s, axis=0, keepdims=True)
                p = jnp.exp2(s - m_new)
                l_ref[:, cs] = jnp.sum(p, axis=0, keepdims=True)
                acc_ref[:, cs] = _dot(vw, p.astype(BF16))
                m_ref[:, cs] = m_new
            else:
                m_old = m_ref[:, cs]
                m_new = jnp.maximum(m_old, jnp.max(s, axis=0, keepdims=True))
                alpha = jnp.exp2(m_old - m_new)
                p = jnp.exp2(s - m_new)
                l_ref[:, cs] = alpha * l_ref[:, cs] + jnp.sum(p, axis=0, keepdims=True)
                acc_ref[:, cs] = alpha * acc_ref[:, cs] + _dot(vw, p.astype(BF16))
                m_ref[:, cs] = m_new

    window_half(pl.multiple_of(q0, tq), True)

    @pl.when(qi > 0)
    def _():
        window_half(pl.multiple_of(q0 - WINDOW, tq), False)

    o_win = acc_ref[...] / jnp.maximum(l_ref[...], 1e-30)

    def key_tile(k0, rows):
        return jnp.concatenate([ksw_ref[0, pl.ds(k0, rows), :], hot_ref[pl.ds(k0, rows), :]], axis=1)

    def val_tile(k0, j, rows):
        return vswT_ref[0, 0:HEAD_DIM, pl.ds(k0, rows)]

    o_sel = _flash_causal(key_tile, val_tile, qs_ref, m_ref, l_ref, acc_ref, s_ref,
                          q0=q0, tq=tq, tk=tk, cb=cb, bounded=bounded)

    gates = ngT_ref[0]
    out_t = []
    for hd in range(N_HEADS):
        cs = slice(hd * tq, (hd + 1) * tq)
        out_t.append(gates[3 * hd:3 * hd + 1] * o_cmp[:, cs]
                     + gates[3 * hd + 1:3 * hd + 2] * o_sel[:, cs]
                     + gates[3 * hd + 2:3 * hd + 3] * o_win[:, cs])
    o_ref[0] = jnp.concatenate(out_t, axis=0).T.astype(o_ref.dtype)


def _nsa(nqT, kc, vcT, ksw, vswT, hot, ngT, batch, seq, score_bound):
    tq, tk, cb = NSA_TQ, ATTN_TK, ATTN_CB
    cols = N_HEADS * tq
    nblk = seq // CMP_STRIDE
    assert seq % tk == 0 and (tk % tq == 0 or tq % tk == 0) and cb % tq == 0 and cols % cb == 0
    assert tq == WINDOW and seq % tq == 0 and (tq & (tq - 1)) == 0
    assert seq // SLC_BLOCK <= MAX_SEL_BLOCKS
    full = lambda b, qi: (b, 0, 0)
    args = (nqT, kc, vcT, ksw.reshape(batch, seq, 128), vswT, hot, ngT)

    def call(bounded):
        return pl.pallas_call(
            functools.partial(_nsa_body, tq=tq, tk=tk, cb=cb, seq=seq, bounded=bounded),
            grid=(batch, seq // tq),
            in_specs=[pl.BlockSpec((1, 256, tq), lambda b, qi: (b, 0, qi)),
                      pl.BlockSpec((1, nblk, 128), full),
                      pl.BlockSpec((1, HEAD_DIM, nblk), full),
                      pl.BlockSpec((1, seq, 128), full),
                      pl.BlockSpec((1, 128, seq), full),
                      pl.BlockSpec((seq, MAX_SEL_BLOCKS), lambda b, qi: (0, 0)),
                      pl.BlockSpec((1, 16, tq), lambda b, qi: (b, 0, qi))],
            out_specs=pl.BlockSpec((1, tq, 256), lambda b, qi: (b, qi, 0)),
            out_shape=jax.ShapeDtypeStruct((batch, seq, 256), BF16),
            scratch_shapes=[pltpu.VMEM((256, cols), BF16),
                            pltpu.VMEM((128, cols), BF16),
                            pltpu.VMEM((1, cols), F32),
                            pltpu.VMEM((1, cols), F32),
                            pltpu.VMEM((HEAD_DIM, cols), F32),
                            pltpu.VMEM((tk, cb), BF16 if bounded else F32)],
            compiler_params=_cparams("arbitrary", "arbitrary"),
            name="nsa" + ("_bounded" if bounded else ""),
        )(*args)

    out = lax.cond(score_bound <= SCORE_BOUND, lambda: call(True), lambda: call(False))
    return out.reshape(batch * seq, 256)


def _merge_body(x_ref, g_ref, oa_ref, ob_ref, oc_ref, u_ref, vn_ref, ws_ref, bt_ref,
                wg_ref, wbr_ref, wout_ref, o_ref):
    x = x_ref[...]
    tm, d = x.shape
    h = (_rms_rows(x) * g_ref[...]).astype(BF16)

    r = lax.broadcasted_iota(jnp.int32, (GMLP_CHUNK, GMLP_CHUNK), 0)
    c = lax.broadcasted_iota(jnp.int32, (GMLP_CHUNK, GMLP_CHUNK), 1)
    lane_grp = lax.broadcasted_iota(jnp.int32, (GMLP_CHUNK, W_MIX), 1) // HEAD_DIM
    w_tril = [jnp.where(r >= c, ws_ref[g], 0.0).astype(BF16) for g in range(N_HEADS)]
    od = []
    for ch in range(tm // GMLP_CHUNK):
        sl = slice(ch * GMLP_CHUNK, (ch + 1) * GMLP_CHUNK)
        vch = vn_ref[sl, :]
        sv = jnp.zeros((GMLP_CHUNK, W_MIX), F32)
        for g in range(N_HEADS):
            sv = jnp.where(lane_grp == g, _dot(w_tril[g], vch), sv)
        od.append(u_ref[sl, :] * (sv + bt_ref[...]))
    o_d = jnp.concatenate(od, axis=0).astype(BF16)

    branches = (oa_ref[...], ob_ref[...], oc_ref[...], o_d)
    acc = jnp.zeros_like(x)
    for n in range(4):
        gate = jax.nn.sigmoid(_dot(h, wg_ref[:, n * d:(n + 1) * d]))
        acc = acc + gate * _dot(branches[n], wbr_ref[n])
    o_ref[...] = x + _dot(acc.astype(BF16), wout_ref[...])


def _merge(x2, l, gain, oa, ob, oc, u, vn, w_s, bt, wg, wbr, wout):
    n, d = x2.shape
    tm = MERGE_TM
    row = lambda i: (i, 0)
    return pl.pallas_call(
        _merge_body,
        grid=(n // tm,),
        in_specs=[pl.BlockSpec((tm, d), row),
                  _layer(l, 1, d),
                  pl.BlockSpec((tm, 256), row),
                  pl.BlockSpec((tm, 256), row),
                  pl.BlockSpec((tm, 256), row),
                  pl.BlockSpec((tm, 256), row),
                  pl.BlockSpec((tm, 256), row),
                  _layer(l, N_HEADS, GMLP_CHUNK, GMLP_CHUNK),
                  _layer(l, GMLP_CHUNK, W_MIX),
                  _layer(l, d, 4 * d),
                  _layer(l, 4, W_MIX, d),
                  _layer(l, d, d)],
        out_specs=pl.BlockSpec((tm, d), row),
        out_shape=jax.ShapeDtypeStruct((n, d), F32),
        compiler_params=_cparams("arbitrary"),
        name="merge",
    )(x2, gain, oa, ob, oc, u, vn, w_s, bt, wg, wbr, wout)


def _block_diag_mean(width, group):
    idx = jnp.arange(width) // group
    return ((idx[:, None] == idx[None, :]).astype(F32) / float(group)).astype(BF16)


def _stage_proj_weight(w_in):
    nkv = 1796
    w_in = w_in.astype(BF16)
    lead = w_in.shape[:-1]
    flog = w_in[..., 1536:1540]
    cols = [w_in[..., 0:1536],
            w_in[..., 1540:1796],
            w_in[..., nkv + 128:nkv + 192],
            w_in[..., nkv + 256:nkv + 320],
            w_in[..., nkv + 192:nkv + 256],
            w_in[..., nkv + 320:nkv + 384],
            w_in[..., nkv:nkv + 128],
            w_in[..., 2192:2704],
            flog, flog, flog,
            jnp.zeros(lead + (_GATE_COL - 12,), w_in.dtype),
            w_in[..., 2180:2192],
            jnp.zeros(lead + (128 - _GATE_COL - 12,), w_in.dtype)]
    w = jnp.concatenate(cols, axis=-1)
    assert w.shape[-1] == _C_TOTAL
    return w, w_in[..., 2704:]


def _stage_compress(pe, w1, b1, w2, b2):
    hid = w1.shape[-1]
    half = CMP_STRIDE
    pe_ext = jnp.concatenate([pe[0], pe[1]], axis=-1)
    pe_ext = jnp.stack([pe_ext[:half].reshape(-1), pe_ext[half:].reshape(-1)])
    w1k = w1[0].reshape(CMP_BLOCK, HEAD_DIM, hid)
    w1v = w1[1].reshape(CMP_BLOCK, HEAD_DIM, hid)
    zk = jnp.zeros_like(w1k)
    top = jnp.concatenate([w1k, zk], axis=-1)
    bot = jnp.concatenate([zk, w1v], axis=-1)
    w1e = jnp.concatenate([top, bot], axis=1)
    w1a = w1e[:half].reshape(half * 128, 2 * hid).astype(BF16)
    w1b = w1e[half:].reshape(half * 128, 2 * hid).astype(BF16)
    b1e = jnp.concatenate([b1[0], b1[1]]).reshape(1, 2 * hid)
    z = jnp.zeros((hid, HEAD_DIM), F32)
    w2e = jnp.concatenate([jnp.concatenate([w2[0], z, z, z], axis=1),
                           jnp.concatenate([z, z, w2[1], z], axis=1)], axis=0).astype(BF16)
    zb = jnp.zeros((HEAD_DIM,), F32)
    b2e = jnp.concatenate([b2[0], zb, b2[1], zb]).reshape(1, 256)
    return pe_ext, w1a, w1b, b1e, w2e, b2e


def kernel(x, ffn1_norm, ffn1_w_in, ffn1_w_out, mix_norm, w_in, diff_q_gain, diff_k_gain, diff_lambda, fox_q_gain, fox_k_gain, fox_f_bias, nsa_q_gain, nsa_k_gain, nsa_cmp_pe, nsa_phi_w1, nsa_phi_b1, nsa_phi_w2, nsa_phi_b2, gmlp_v_gain, gmlp_w_s, gmlp_b_s, w_branch, w_out, ffn2_norm, ffn2_w_in, ffn2_w_out):
    batch, seq, d = x.shape
    depth = w_in.shape[0]
    n = batch * seq
    g32 = _block_diag_mean(256, DIFF_QK_DIM)
    g64 = _block_diag_mean(256, HEAD_DIM)
    g64s = g64[:128, :128]
    hot = (jnp.arange(seq)[:, None] // SLC_BLOCK == jnp.arange(MAX_SEL_BLOCKS)[None, :]).astype(BF16)

    norm3 = lambda g: g.reshape(depth, 1, d).astype(F32)
    ffn1_g, ffn2_g, mix_g = norm3(ffn1_norm), norm3(ffn2_norm), norm3(mix_norm)
    ffn1_wi, ffn1_wo = ffn1_w_in.astype(BF16), ffn1_w_out.astype(BF16)
    ffn2_wi, ffn2_wo = ffn2_w_in.astype(BF16), ffn2_w_out.astype(BF16)
    w_cat, w_gate = _stage_proj_weight(w_in)
    tile = lambda g, r: jnp.tile(g.astype(F32), (1, r))
    gains = jnp.stack([tile(diff_q_gain, 8), tile(diff_k_gain, 8), tile(fox_q_gain, 4),
                       tile(fox_k_gain, 4), tile(nsa_q_gain, 4), tile(nsa_k_gain, 4),
                       gmlp_v_gain.astype(F32), jnp.zeros((depth, 256), F32)], axis=1)
    fb = fox_f_bias.astype(F32)
    fbias = jnp.concatenate([fb, fb, fb, jnp.zeros((depth, 116), F32)], axis=1).reshape(depth, 1, 128)
    lam_p = diff_lambda.astype(F32)
    cmp_w = jax.vmap(_stage_compress)(nsa_cmp_pe, nsa_phi_w1, nsa_phi_b1, nsa_phi_w2, nsa_phi_b2)
    kgain = tile(nsa_k_gain, 2).reshape(depth, 1, 128)
    bt = jnp.repeat(jnp.swapaxes(gmlp_b_s, 1, 2), HEAD_DIM, axis=2).astype(F32)
    w_s = gmlp_w_s.astype(F32)
    w_br, w_o = w_branch.astype(BF16), w_out.astype(BF16)
    amax = lambda g: jnp.max(jnp.abs(g.astype(F32)), axis=-1)
    bound = lambda gq, gk, grp: (grp ** 0.5 * LOG2E * BOUND_MARGIN) * amax(gq) * amax(gk)
    b_diff = bound(diff_q_gain, diff_k_gain, DIFF_QK_DIM)
    b_fox = bound(fox_q_gain, fox_k_gain, HEAD_DIM)
    b_nsa = bound(nsa_q_gain, nsa_k_gain, HEAD_DIM)

    x2 = x.reshape(n, d).astype(F32)
    for l in range(depth):
        x2 = _ffn(x2, l, ffn1_g, ffn1_wi, ffn1_wo)
        (aqT, ak, avT, fqT, fk, fvT, faug, cT, nqT, ksw, vswT, kvc, u, vn, ngT) = _proj(
            x2, l, mix_g, w_cat, g32, g64, gains, fbias, batch, seq)

        lam_init = 0.8 - 0.6 * math.exp(-0.3 * l)
        o_a = _attn("diff", aqT, ak, avT, (l, lam_p, g64s), batch, seq, b_diff[l], lam_init)
        o_b = _attn("fox", fqT, fk, fvT, (faug, cT), batch, seq, b_fox[l])
        kc, vcT = _compress(kvc, l, *cmp_w, g64s, kgain, batch, seq)
        o_c = _nsa(nqT, kc, vcT, ksw, vswT, hot, ngT, batch, seq, b_nsa[l])

        x2 = _merge(x2, l, mix_g, o_a, o_b, o_c, u, vn, w_s, bt, w_gate, w_br, w_o)
        x2 = _ffn(x2, l, ffn2_g, ffn2_wi, ffn2_wo)
    return x2.reshape(batch, seq, d).astype(x.dtype)
```

```python
import functools
import math

import jax
import jax.numpy as jnp
from jax import lax
from jax.experimental import pallas as pl
from jax.experimental.pallas import tpu as pltpu

F32 = jnp.float32
BF16 = jnp.bfloat16

HEAD_DIM = 64
N_HEADS = 4
W_MIX = N_HEADS * HEAD_DIM
DIFF_QK_DIM = HEAD_DIM // 2
CMP_BLOCK = 32
CMP_STRIDE = 16
SLC_BLOCK = 64
TOPK = 16
WINDOW = 512
FORCE_SCORE = 1.0e4
GMLP_CHUNK = 128
RMS_EPS = 1e-6
NEG = -1.0e30
LOG2E = math.log2(math.e)

VMEM_LIMIT_BYTES = 56 * 1024 * 1024

FFN_TM = 512
FFN_TF = 256
PROJ_TM = 512
MERGE_TM = 512
ATTN_COLS = 2048
ATTN_CB = 512
ATTN_TK = 1024
SEL_VARIANTS = 4
NSA_TQ = 512
MAX_SEL_BLOCKS = 128
SCORE_BOUND = 40.0
BOUND_MARGIN = 1.02


def _dot(a, b):
    return jnp.dot(a, b, preferred_element_type=F32)


def _split3(c):
    hi = c.astype(BF16)
    r1 = c - hi.astype(F32)
    mid = r1.astype(BF16)
    lo = (r1 - mid.astype(F32)).astype(BF16)
    return hi, mid, lo


def _dot_exact_lhs(a, x, terms):
    return sum(_dot(a, part) for part in _split3(x)[:terms])


def _dot_exact_rhs(x, a, terms):
    return sum(_dot(part, a) for part in _split3(x)[:terms])


def _rms_rows(x):
    ms = jnp.mean(x * x, axis=-1, keepdims=True)
    return x * lax.rsqrt(ms + RMS_EPS)


def _group_norm(z, gmat, gain):
    ms = _dot_exact_rhs(z * z, gmat, 2)
    return z * lax.rsqrt(ms + RMS_EPS) * gain


def _cparams(*sem):
    return pltpu.CompilerParams(dimension_semantics=sem, vmem_limit_bytes=VMEM_LIMIT_BYTES)


def _ffn_body(x_ref, g_ref, wa_ref, wb_ref, wo_ref, o_ref, *, n_chunks, tf):
    x = x_ref[...]
    h = (_rms_rows(x) * g_ref[...]).astype(BF16)
    acc = jnp.zeros_like(x)
    for c in range(n_chunks):
        a = _dot(h, wa_ref[:, c * tf:(c + 1) * tf])
        b = _dot(h, wb_ref[:, c * tf:(c + 1) * tf])
        g = (a * jax.nn.sigmoid(a) * b).astype(BF16)
        acc = acc + _dot(g, wo_ref[c * tf:(c + 1) * tf, :])
    o_ref[...] = x + 0.5 * acc


def _layer(l, *block):
    return pl.BlockSpec((None,) + block, lambda *_: (l,) + (0,) * len(block))


def _ffn(x2, l, gain, w_in, w_out):
    n, d = x2.shape
    d_ff = w_out.shape[1]
    tm, tf = FFN_TM, FFN_TF
    assert n % tm == 0 and d_ff % tf == 0
    return pl.pallas_call(
        functools.partial(_ffn_body, n_chunks=d_ff // tf, tf=tf),
        grid=(n // tm,),
        in_specs=[pl.BlockSpec((tm, d), lambda i: (i, 0)),
                  _layer(l, 1, d),
                  pl.BlockSpec((None, d, d_ff), lambda i: (l, 0, 0)),
                  pl.BlockSpec((None, d, d_ff), lambda i: (l, 0, 1)),
                  _layer(l, d_ff, d)],
        out_specs=pl.BlockSpec((tm, d), lambda i: (i, 0)),
        out_shape=jax.ShapeDtypeStruct((n, d), F32),
        compiler_params=_cparams("arbitrary"),
        name="ffn",
    )(x2, gain, w_in, w_in, w_out)


_C_AQ, _C_AK, _C_AV = 0, 256, 512
_C_FQ, _C_FK, _C_FV = 768, 1024, 1280
_C_NQ = 1536
_C_KSW = 1792
_C_VSW = 1920
_C_KVC = 2048
_C_UV = 2176
_C_SMALL = 2688
_C_TOTAL = 2816
_GATE_COL = 16
_AUG_HI, _AUG_MID, _AUG_LO, _AUG_ONE = 0, 4, 8, 12


def _proj_body(x_ref, g_ref, w_ref, g32_ref, g64_ref, gains_ref, fbias_ref,
               aqT_ref, ak_ref, avT_ref, fqT_ref, fk_ref, fvT_ref, faug_ref, cT_ref,
               nqT_ref, ksw_ref, vswT_ref, kvc_ref, u_ref, vn_ref, ngT_ref,
               carry_ref, *, tiles_per_seq):
    i = pl.program_id(0)
    x = x_ref[...]
    tm = x.shape[0]
    h = (_rms_rows(x) * g_ref[...]).astype(BF16)
    z = _dot(h, w_ref[...])
    g32 = g32_ref[...]
    g64 = g64_ref[...]
    gains = gains_ref[...]

    def sec(c0, w=256):
        return z[:, c0:c0 + w]

    aq = _group_norm(sec(_C_AQ), g32, gains[0:1]) * (DIFF_QK_DIM ** -0.5 * LOG2E)
    aqT_ref[0] = aq.T.astype(BF16)
    ak_ref[...] = _group_norm(sec(_C_AK), g32, gains[1:2]).astype(BF16)
    avT_ref[0] = sec(_C_AV).T.astype(BF16)
    fq = _group_norm(sec(_C_FQ), g64, gains[2:3]) * (HEAD_DIM ** -0.5 * LOG2E)
    fqT_ref[0] = fq.T.astype(BF16)
    fk_ref[...] = _group_norm(sec(_C_FK), g64, gains[3:4]).astype(BF16)
    fvT_ref[0] = sec(_C_FV).T.astype(BF16)
    nq = _group_norm(sec(_C_NQ), g64, gains[4:5]) * (HEAD_DIM ** -0.5 * LOG2E)
    nqT_ref[0] = nq.T.astype(BF16)
    ksw_ref[...] = _group_norm(sec(_C_KSW, 128), g64[:128, :128], gains[5:6, :128]).astype(BF16)
    vswT_ref[0] = sec(_C_VSW, 128).T.astype(BF16)
    kvc_ref[...] = sec(_C_KVC, 128)
    guv = jax.nn.gelu(sec(_C_UV, 512))
    u_ref[...] = guv[:, :256]
    vn_ref[...] = _group_norm(guv[:, 256:], g64, gains[6:7]).astype(BF16)
    small = sec(_C_SMALL, 128)
    ngT_ref[0] = jax.nn.sigmoid(small).T[_GATE_COL:_GATE_COL + 16, :]
    t = small + fbias_ref[...]
    logf = jnp.minimum(t, 0.0) - jnp.log(1.0 + jnp.exp(-jnp.abs(t)))

    @pl.when(i % tiles_per_seq == 0)
    def _():
        carry_ref[...] = jnp.zeros_like(carry_ref)

    row = lax.broadcasted_iota(jnp.int32, (tm, tm), 0)
    col = lax.broadcasted_iota(jnp.int32, (tm, tm), 1)
    tril = jnp.where(row >= col, 1.0, 0.0).astype(BF16)
    csum = _dot_exact_lhs(tril, logf, 3) + carry_ref[0:1, :]
    carry_ref[...] = jnp.broadcast_to(csum[tm - 1:tm, :], carry_ref.shape)
    c2 = csum * LOG2E
    cT_ref[0] = c2.T[0:8, :]
    hi, mid, lo = _split3(c2)
    lane = lax.broadcasted_iota(jnp.int32, (tm, 128), 1)
    one = jnp.where(lane < _AUG_ONE + 3, 1.0, 0.0).astype(BF16)
    faug_ref[...] = jnp.where(lane < _AUG_MID, hi,
                              jnp.where(lane < _AUG_LO, mid, jnp.where(lane < _AUG_ONE, lo, one)))


def _proj(x2, l, gain, w_cat, g32, g64, gains, fbias, batch, seq):
    n, d = x2.shape
    tm = PROJ_TM
    assert seq % tm == 0
    tps = seq // tm
    nt = n // tm
    const = lambda i: (0, 0)
    row = lambda i: (i, 0)
    trn = lambda i: (i // tps, 0, i % tps)
    nat = lambda w, dt: (jax.ShapeDtypeStruct((n, w), dt), pl.BlockSpec((tm, w), row))
    tr = lambda r, dt: (jax.ShapeDtypeStruct((batch, r, seq), dt), pl.BlockSpec((1, r, tm), trn))
    outs = [
        tr(256, BF16), nat(256, BF16), tr(256, BF16),
        tr(256, BF16), nat(256, BF16), tr(256, BF16),
        nat(128, BF16), tr(8, F32),
        tr(256, BF16), nat(128, BF16), tr(128, BF16),
        nat(128, F32), nat(256, F32), nat(256, BF16), tr(16, F32),
    ]
    return pl.pallas_call(
        functools.partial(_proj_body, tiles_per_seq=tps),
        grid=(nt,),
        in_specs=[pl.BlockSpec((tm, d), row),
                  _layer(l, 1, d),
                  _layer(l, d, _C_TOTAL),
                  pl.BlockSpec((256, 256), const),
                  pl.BlockSpec((256, 256), const),
                  _layer(l, 8, 256),
                  _layer(l, 1, 128)],
        out_specs=[o[1] for o in outs],
        out_shape=[o[0] for o in outs],
        scratch_shapes=[pltpu.VMEM((8, 128), F32)],
        compiler_params=_cparams("arbitrary"),
        name="proj",
    )(x2, gain, w_cat, g32, g64, gains, fbias)


def _flash_causal(key_tile, val_tile, qs_ref, m_ref, l_ref, acc_ref, s_ref, *, q0, tq, tk, cb,
                  bounded):
    cols = qs_ref.shape[1]
    nb = cols // cb
    n_full = q0 // tk
    m_ref[...] = jnp.full_like(m_ref, NEG)
    l_ref[...] = jnp.zeros_like(l_ref)
    acc_ref[...] = jnp.zeros_like(acc_ref)
    krow = lax.broadcasted_iota(jnp.int32, (tk, cb), 0)
    kcol = lax.broadcasted_iota(jnp.int32, (tk, cb), 1)
    row_minus_q = [krow - ((j * cb + kcol) & (tq - 1)) for j in range(nb)]
    no_mask = jnp.int32(2 ** 30)

    def scores(kk, j, thr):
        s = _dot(kk, qs_ref[:, j * cb:(j + 1) * cb])
        if thr is not None:
            s = jnp.where(row_minus_q[j][0:s.shape[0]] <= thr, s, NEG)
        return s

    def carried(kk, j, thr):
        s = scores(kk, j, thr)
        if not bounded:
            return s
        p = jnp.exp2(s)
        l_ref[:, j * cb:(j + 1) * cb] += jnp.sum(p, axis=0, keepdims=True)
        return p.astype(BF16)

    s_ref[...] = carried(key_tile(0, tk), 0, jnp.where(n_full == 0, q0, no_mask))

    def step(ki, masked, next_masked, final=False, rows=tk):
        k0 = pl.multiple_of(ki * tk, tk)
        kk = key_tile(k0, rows)
        s = s_ref[0:rows, :]
        for j in range(nb):
            cs = slice(j * cb, (j + 1) * cb)
            s_next = None
            if j + 1 < nb:
                s_next = carried(kk, j + 1, q0 - k0 if masked else None)
            elif not final:
                k1 = pl.multiple_of(k0 + tk, tk)
                s_ref[...] = carried(key_tile(k1, tk), 0, q0 - k1 if next_masked else None)
            if bounded:
                acc_ref[:, cs] += _dot(val_tile(k0, j, rows), s)
            else:
                m_old = m_ref[:, cs]
                m_new = jnp.maximum(m_old, jnp.max(s, axis=0, keepdims=True))
                alpha = jnp.exp2(m_old - m_new)
                p = jnp.exp2(s - m_new)
                l_ref[:, cs] = alpha * l_ref[:, cs] + jnp.sum(p, axis=0, keepdims=True)
                acc_ref[:, cs] = (alpha * acc_ref[:, cs]
                                  + _dot(val_tile(k0, j, rows), p.astype(BF16)))
                m_ref[:, cs] = m_new
            s = s_next

    def loop_body(ki, carry):
        step(ki, False, False)
        return carry

    lax.fori_loop(0, n_full - 1, loop_body, 0)

    @pl.when(n_full > 0)
    def _():
        step(n_full - 1, False, True)

    if tq >= tk:
        n_diag = tq // tk
        for dt in range(n_diag):
            step(n_full + dt, True, True, final=dt == n_diag - 1)
    else:
        for rows in range(tq, tk + 1, tq):
            @pl.when(q0 - n_full * tk + tq == rows)
            def _(rows=rows):
                step(n_full, True, True, final=True, rows=rows)
    return acc_ref[...] / jnp.maximum(l_ref[...], 1e-30)


def _attn_body(*refs, mode, tq, tk, cb, lam_init, bounded):
    if mode == "fox":
        qT_ref, k_ref, vT_ref, aug_ref, cT_ref, o_ref, qs_ref, m_ref, l_ref, acc_ref, s_ref = refs
    else:
        qT_ref, k_ref, vT_ref, lam_ref, g64_ref, o_ref, qs_ref, m_ref, l_ref, acc_ref, s_ref = refs
    nv = 2 if mode == "fox" else 4
    wv = 128 // nv
    hp = pl.program_id(1)
    qi = pl.program_id(2)
    q0 = qi * tq

    qT = qT_ref[0]
    rowi = lax.broadcasted_iota(jnp.int32, (128, tq), 0)
    zero = jnp.zeros_like(qT)
    for vh in range(nv):
        keep = (rowi >= vh * wv) & (rowi < (vh + 1) * wv)
        qs_ref[0:128, vh * tq:(vh + 1) * tq] = jnp.where(keep, qT, zero)
        if mode == "fox":
            hd = 2 * hp + vh
            hi, mid, lo = _split3(cT_ref[0, pl.ds(hd, 1), :])
            neg = (rowi == _AUG_HI + hd) | (rowi == _AUG_MID + hd) | (rowi == _AUG_LO + hd)
            aug = jnp.where(neg, -1.0, 0.0).astype(BF16)
            aug = jnp.where(rowi == _AUG_ONE, hi, aug)
            aug = jnp.where(rowi == _AUG_ONE + 1, mid, aug)
            aug = jnp.where(rowi == _AUG_ONE + 2, lo, aug)
            qs_ref[128:256, vh * tq:(vh + 1) * tq] = aug

    def key_tile(k0, rows):
        kk = k_ref[0, pl.ds(k0, rows), :]
        if mode == "fox":
            kk = jnp.concatenate([kk, aug_ref[0, pl.ds(k0, rows), :]], axis=1)
        return kk

    def val_tile(k0, j, rows):
        hd = ((j * cb) // tq) // (nv // 2)
        return vT_ref[0, hd * HEAD_DIM:(hd + 1) * HEAD_DIM, pl.ds(k0, rows)]

    o = _flash_causal(key_tile, val_tile, qs_ref, m_ref, l_ref, acc_ref, s_ref,
                      q0=q0, tq=tq, tk=tk, cb=cb, bounded=bounded)
    if mode == "fox":
        out_t = jnp.concatenate([o[:, 0:tq], o[:, tq:2 * tq]], axis=0)
    else:
        lp = lam_ref[...]
        lam = (jnp.exp(jnp.sum(lp[0:1] * lp[1:2], axis=-1, keepdims=True))
               - jnp.exp(jnp.sum(lp[2:3] * lp[3:4], axis=-1, keepdims=True)) + lam_init)
        d0 = o[:, 0:tq] - lam * o[:, tq:2 * tq]
        d1 = o[:, 2 * tq:3 * tq] - lam * o[:, 3 * tq:4 * tq]
        dd = jnp.concatenate([d0, d1], axis=0)
        ms = _dot_exact_lhs(g64_ref[...], dd * dd, 2)
        out_t = dd * lax.rsqrt(ms + RMS_EPS) * (1.0 - lam_init)
    o_ref[0] = out_t.T.astype(o_ref.dtype)


def _attn(mode, qT, k, vT, extra, batch, seq, score_bound, lam_init=0.0):
    nv = 2 if mode == "fox" else 4
    tq = ATTN_COLS // nv
    tk, cb = ATTN_TK, ATTN_CB
    assert (tk % tq == 0 or tq % tk == 0) and seq % tk == 0 and (tq % cb == 0 or cb % tq == 0)
    assert (tq & (tq - 1)) == 0
    k3 = k.reshape(batch, seq, 256)
    in_specs = [pl.BlockSpec((1, 128, tq), lambda b, hp, qi: (b, hp, qi)),
                pl.BlockSpec((1, seq, 128), lambda b, hp, qi: (b, 0, hp)),
                pl.BlockSpec((1, 128, seq), lambda b, hp, qi: (b, hp, 0))]
    if mode == "fox":
        faug, cT = extra
        args = (qT, k3, vT, faug.reshape(batch, seq, 128), cT)
        in_specs += [pl.BlockSpec((1, seq, 128), lambda b, hp, qi: (b, 0, 0)),
                     pl.BlockSpec((1, 8, tq), lambda b, hp, qi: (b, 0, qi))]
        depth = 256
    else:
        l, lam_p, g64s = extra
        args = (qT, k3, vT, lam_p, g64s)
        in_specs += [_layer(l, 4, DIFF_QK_DIM),
                     pl.BlockSpec((128, 128), lambda b, hp, qi: (0, 0))]
        depth = 128
    cols = nv * tq
    def call(bounded):
        return pl.pallas_call(
            functools.partial(_attn_body, mode=mode, tq=tq, tk=tk, cb=cb, lam_init=lam_init,
                              bounded=bounded),
            grid=(batch, 2, seq // tq),
            in_specs=in_specs,
            out_specs=pl.BlockSpec((1, tq, 128), lambda b, hp, qi: (b, qi, hp)),
            out_shape=jax.ShapeDtypeStruct((batch, seq, 256), BF16),
            scratch_shapes=[pltpu.VMEM((depth, cols), BF16),
                            pltpu.VMEM((1, cols), F32),
                            pltpu.VMEM((1, cols), F32),
                            pltpu.VMEM((HEAD_DIM, cols), F32),
                            pltpu.VMEM((tk, cb), BF16 if bounded else F32)],
            compiler_params=_cparams("arbitrary", "arbitrary", "arbitrary"),
            name="attn_" + mode + ("_bounded" if bounded else ""),
        )(*args)

    out = lax.cond(score_bound <= SCORE_BOUND, lambda: call(True), lambda: call(False))
    return out.reshape(batch * seq, 256)


def _compress_body(x_ref, pe_ref, w1a_ref, w1b_ref, b1_ref, w2_ref, b2_ref, g64_ref, kg_ref,
                   kc_ref, vcT_ref):
    x = x_ref[0]
    nblk = x.shape[0]
    a = _dot((x + pe_ref[0:1, :]).astype(BF16), w1a_ref[...])
    b = _dot((x + pe_ref[1:2, :]).astype(BF16), w1b_ref[...])
    pre = a + pltpu.roll(b, nblk - 1, 0) + b1_ref[...]
    hid = jax.nn.gelu(pre).astype(BF16)
    out = _dot(hid, w2_ref[...]) + b2_ref[...]
    kc_ref[0] = _group_norm(out[:, 0:128], g64_ref[...], kg_ref[...]).astype(BF16)
    vcT_ref[0] = out[:, 128:256].T[0:HEAD_DIM, :].astype(BF16)


def _compress(kvc, l, pe_ext, w1a, w1b, b1, w2, b2, g64s, kgain, batch, seq):
    nblk = seq // CMP_STRIDE
    x = kvc.reshape(batch, nblk, CMP_STRIDE * 128)
    const = lambda b: (0, 0)
    return pl.pallas_call(
        _compress_body,
        grid=(batch,),
        in_specs=[pl.BlockSpec((1, nblk, CMP_STRIDE * 128), lambda b: (b, 0, 0)),
                  _layer(l, 2, CMP_STRIDE * 128),
                  _layer(l, *w1a.shape[1:]),
                  _layer(l, *w1b.shape[1:]),
                  _layer(l, *b1.shape[1:]),
                  _layer(l, *w2.shape[1:]),
                  _layer(l, *b2.shape[1:]),
                  pl.BlockSpec((128, 128), const),
                  _layer(l, 1, 128)],
        out_specs=[pl.BlockSpec((1, nblk, 128), lambda b: (b, 0, 0)),
                   pl.BlockSpec((1, HEAD_DIM, nblk), lambda b: (b, 0, 0))],
        out_shape=[jax.ShapeDtypeStruct((batch, nblk, 128), BF16),
                   jax.ShapeDtypeStruct((batch, HEAD_DIM, nblk), BF16)],
        compiler_params=_cparams("arbitrary"),
        name="nsa_compress",
    )(x, pe_ext, w1a, w1b, b1, w2, b2, g64s, kgain)


def _topk_select(imp, k):
    nsel, tq = imp.shape

    def count(mask):
        return jnp.sum(jnp.where(mask, 1.0, 0.0), axis=0, keepdims=True)

    kf = float(k)
    bits = jnp.zeros((1, tq), jnp.int32)
    for bit in range(30, -1, -1):
        cand = bits + (1 << bit)
        if bit < 23:
            cand = jnp.where(bits != 0, cand, 0x7F800000)
        ok = count(imp >= lax.bitcast_convert_type(cand, F32)) >= kf
        bits = jnp.where(ok, cand, bits)
    kth = jnp.where(count(imp >= 0.0) >= kf, lax.bitcast_convert_type(bits, F32), -1.0)
    gt = imp > kth
    eq = imp == kth
    need = kf - count(gt)
    r = lax.broadcasted_iota(jnp.int32, (nsel, nsel), 0)
    c = lax.broadcasted_iota(jnp.int32, (nsel, nsel), 1)
    tril = jnp.where(r >= c, 1.0, 0.0).astype(BF16)
    prefix_eq = _dot(tril, jnp.where(eq, 1.0, 0.0).astype(BF16))
    return gt | (eq & (prefix_eq <= need))


def _nsa_body(nqT_ref, kc_ref, vcT_ref, ksw_ref, vswT_ref, hot_ref, ngT_ref, o_ref,
              qs_ref, qw_ref, m_ref, l_ref, acc_ref, ocmp_ref, s_ref, *, tq, tk, cb, seq, bounded):
    qi = pl.program_id(1)
    q0 = qi * tq
    cols = N_HEADS * tq
    nblk = kc_ref.shape[1]
    nsel = seq // SLC_BLOCK
    nb = cols // cb

    zero = jnp.zeros((HEAD_DIM, tq), BF16)
    for hd in range(N_HEADS):
        qh = nqT_ref[0, hd * HEAD_DIM:(hd + 1) * HEAD_DIM, :]
        cs = slice(hd * tq, (hd + 1) * tq)
        qs_ref[0:HEAD_DIM, cs] = qh
        qs_ref[HEAD_DIM:128, cs] = zero
        qw_ref[0:HEAD_DIM, cs] = zero
        qw_ref[HEAD_DIM:128, cs] = qh

    def qoff(shape):
        return lax.broadcasted_iota(jnp.int32, shape, 1) & (tq - 1)

    def select_blocks(n_c, n_s):
        kc = kc_ref[0, 0:n_c, :]
        vcT = vcT_ref[0, :, 0:n_c]
        cend_minus_q = (lax.broadcasted_iota(jnp.int32, (n_c, cb), 0) * CMP_STRIDE
                        + (CMP_BLOCK - 1) - qoff((n_c, cb)))
        p_sum = None
        for j in range(nb):
            s = _dot(kc, qs_ref[0:128, j * cb:(j + 1) * cb])
            s = jnp.where(cend_minus_q <= q0, s, -jnp.inf)
            if bounded:
                e = jnp.exp2(s)
            else:
                m = jnp.max(s, axis=0, keepdims=True)
                m = jnp.where(m == -jnp.inf, 0.0, m)
                e = jnp.exp2(s - m)
            p = e / jnp.maximum(jnp.sum(e, axis=0, keepdims=True), 1e-30)
            ocmp_ref[:, j * cb:(j + 1) * cb] = _dot(vcT, p.astype(BF16))
            for c0 in range(0, cb, tq):
                ph = p[:, c0:c0 + tq]
                p_sum = ph if p_sum is None else p_sum + ph

        jj = lax.broadcasted_iota(jnp.int32, (n_s, n_c), 0)
        cc = lax.broadcasted_iota(jnp.int32, (n_s, n_c), 1)
        ov_t = jnp.where((cc * CMP_STRIDE < (jj + 1) * SLC_BLOCK)
                         & (cc * CMP_STRIDE + CMP_BLOCK > jj * SLC_BLOCK), 1.0, 0.0).astype(BF16)
        imp = _dot_exact_lhs(ov_t, p_sum, 3)
        sj = lax.broadcasted_iota(jnp.int32, (n_s, tq), 0)
        qpos_l = q0 + lax.broadcasted_iota(jnp.int32, (n_s, tq), 1)
        cur = qpos_l >> 6
        forced = (sj == 0) | (sj == cur) | (sj == cur - 1)
        valid = sj * SLC_BLOCK <= qpos_l
        imp = jnp.where(forced, FORCE_SCORE, jnp.where(valid, imp, -1.0))
        sel = _topk_select(imp, min(TOPK, n_s))
        bias = jnp.where(sel, 0.0, -(2.0 ** 30)).astype(BF16)
        if n_s < MAX_SEL_BLOCKS:
            bias = jnp.concatenate([bias, jnp.zeros((MAX_SEL_BLOCKS - n_s, tq), BF16)], axis=0)
        for hd in range(N_HEADS):
            qs_ref[128:256, hd * tq:(hd + 1) * tq] = bias

    n_tiles = seq // tq
    n_var = SEL_VARIANTS if n_tiles % SEL_VARIANTS == 0 and nsel // SEL_VARIANTS >= TOPK else 1
    for v in range(n_var):
        @pl.when(qi // (n_tiles // n_var) == v)
        def _(v=v):
            select_blocks(nblk * (v + 1) // n_var, nsel * (v + 1) // n_var)

    o_cmp = ocmp_ref[...]

    key_minus_q = lax.broadcasted_iota(jnp.int32, (tq, cb), 0) - qoff((tq, cb))

    def window_half(k0, own):
        kw = ksw_ref[0, pl.ds(k0, tq), :]
        vw = vswT_ref[0, HEAD_DIM:128, pl.ds(k0, tq)]
        keep = key_minus_q <= 0 if own else key_minus_q > 0
        s_next = _dot(kw, qw_ref[:, 0:cb])
        for j in range(nb):
            cs = slice(j * cb, (j + 1) * cb)
            s = jnp.where(keep, s_next, NEG)
            if j + 1 < nb:
                s_next = _dot(kw, qw_ref[:, (j + 1) * cb:(j + 2) * cb])
            if bounded:
                p = jnp.exp2(s)
                lsum, pv = jnp.sum(p, axis=0, keepdims=True), _dot(vw, p.astype(BF16))
                l_ref[:, cs] = lsum if own else l_ref[:, cs] + lsum
                acc_ref[:, cs] = pv if own else acc_ref[:, cs] + pv
            elif own:
                m_new = jnp.max(s, axis=0, keepdims=True)
                p = jnp.exp2(s - m_new)
                l_ref[:, cs] = jnp.sum(p, axis=0, keepdims=True)
                acc_ref[:, cs] = _dot(vw, p.astype(BF16))
                m_ref[:, cs] = m_new
            else:
                m_old = m_ref[:, cs]
                m_new = jnp.maximum(m_old, jnp.max(s, axis=0, keepdims=True))
                alpha = jnp.exp2(m_old - m_new)
                p = jnp.exp2(s - m_new)
                l_ref[:, cs] = alpha * l_ref[:, cs] + jnp.sum(p, axis=0, keepdims=True)
                acc_ref[:, cs] = alpha * acc_ref[:, cs] + _dot(vw, p.astype(BF16))
                m_ref[:, cs] = m_new

    window_half(pl.multiple_of(q0, tq), True)

    @pl.when(qi > 0)
    def _():
        window_half(pl.multiple_of(q0 - WINDOW, tq), False)

    o_win = acc_ref[...] / jnp.maximum(l_ref[...], 1e-30)

    def key_tile(k0, rows):
        return jnp.concatenate([ksw_ref[0, pl.ds(k0, rows), :], hot_ref[pl.ds(k0, rows), :]], axis=1)

    def val_tile(k0, j, rows):
        return vswT_ref[0, 0:HEAD_DIM, pl.ds(k0, rows)]

    o_sel = _flash_causal(key_tile, val_tile, qs_ref, m_ref, l_ref, acc_ref, s_ref,
                          q0=q0, tq=tq, tk=tk, cb=cb, bounded=bounded)

    gates = ngT_ref[0]
    out_t = []
    for hd in range(N_HEADS):
        cs = slice(hd * tq, (hd + 1) * tq)
        out_t.append(gates[3 * hd:3 * hd + 1] * o_cmp[:, cs]
                     + gates[3 * hd + 1:3 * hd + 2] * o_sel[:, cs]
                     + gates[3 * hd + 2:3 * hd + 3] * o_win[:, cs])
    o_ref[0] = jnp.concatenate(out_t, axis=0).T.astype(o_ref.dtype)


def _nsa(nqT, kc, vcT, ksw, vswT, hot, ngT, batch, seq, score_bound):
    tq, tk, cb = NSA_TQ, ATTN_TK, ATTN_CB
    cols = N_HEADS * tq
    nblk = seq // CMP_STRIDE
    assert seq % tk == 0 and (tk % tq == 0 or tq % tk == 0) and cb % tq == 0 and cols % cb == 0
    assert tq == WINDOW and seq % tq == 0 and (tq & (tq - 1)) == 0
    assert seq // SLC_BLOCK <= MAX_SEL_BLOCKS
    full = lambda b, qi: (b, 0, 0)
    args = (nqT, kc, vcT, ksw.reshape(batch, seq, 128), vswT, hot, ngT)

    def call(bounded):
        return pl.pallas_call(
            functools.partial(_nsa_body, tq=tq, tk=tk, cb=cb, seq=seq, bounded=bounded),
            grid=(batch, seq // tq),
            in_specs=[pl.BlockSpec((1, 256, tq), lambda b, qi: (b, 0, qi)),
                      pl.BlockSpec((1, nblk, 128), full),
                      pl.BlockSpec((1, HEAD_DIM, nblk), full),
                      pl.BlockSpec((1, seq, 128), full),
                      pl.BlockSpec((1, 128, seq), full),
                      pl.BlockSpec((seq, MAX_SEL_BLOCKS), lambda b, qi: (0, 0)),
                      pl.BlockSpec((1, 16, tq), lambda b, qi: (b, 0, qi))],
            out_specs=pl.BlockSpec((1, tq, 256), lambda b, qi: (b, qi, 0)),
            out_shape=jax.ShapeDtypeStruct((batch, seq, 256), BF16),
            scratch_shapes=[pltpu.VMEM((256, cols), BF16),
                            pltpu.VMEM((128, cols), BF16),
                            pltpu.VMEM((1, cols), F32),
                            pltpu.VMEM((1, cols), F32),
                            pltpu.VMEM((HEAD_DIM, cols), F32),
                            pltpu.VMEM((HEAD_DIM, cols), F32),
                            pltpu.VMEM((tk, cb), BF16 if bounded else F32)],
            compiler_params=_cparams("arbitrary", "arbitrary"),
            name="nsa" + ("_bounded" if bounded else ""),
        )(*args)

    out = lax.cond(score_bound <= SCORE_BOUND, lambda: call(True), lambda: call(False))
    return out.reshape(batch * seq, 256)


def _merge_body(x_ref, g_ref, oa_ref, ob_ref, oc_ref, u_ref, vn_ref, ws_ref, bt_ref,
                wg_ref, wbr_ref, wout_ref, o_ref):
    x = x_ref[...]
    tm, d = x.shape
    h = (_rms_rows(x) * g_ref[...]).astype(BF16)

    r = lax.broadcasted_iota(jnp.int32, (GMLP_CHUNK, GMLP_CHUNK), 0)
    c = lax.broadcasted_iota(jnp.int32, (GMLP_CHUNK, GMLP_CHUNK), 1)
    lane_grp = lax.broadcasted_iota(jnp.int32, (GMLP_CHUNK, W_MIX), 1) // HEAD_DIM
    w_tril = [jnp.where(r >= c, ws_ref[g], 0.0).astype(BF16) for g in range(N_HEADS)]
    od = []
    for ch in range(tm // GMLP_CHUNK):
        sl = slice(ch * GMLP_CHUNK, (ch + 1) * GMLP_CHUNK)
        vch = vn_ref[sl, :]
        sv = jnp.zeros((GMLP_CHUNK, W_MIX), F32)
        for g in range(N_HEADS):
            sv = jnp.where(lane_grp == g, _dot(w_tril[g], vch), sv)
        od.append(u_ref[sl, :] * (sv + bt_ref[...]))
    o_d = jnp.concatenate(od, axis=0).astype(BF16)

    branches = (oa_ref[...], ob_ref[...], oc_ref[...], o_d)
    acc = jnp.zeros_like(x)
    for n in range(4):
        gate = jax.nn.sigmoid(_dot(h, wg_ref[:, n * d:(n + 1) * d]))
        acc = acc + gate * _dot(branches[n], wbr_ref[n])
    o_ref[...] = x + _dot(acc.astype(BF16), wout_ref[...])


def _merge(x2, l, gain, oa, ob, oc, u, vn, w_s, bt, wg, wbr, wout):
    n, d = x2.shape
    tm = MERGE_TM
    row = lambda i: (i, 0)
    return pl.pallas_call(
        _merge_body,
        grid=(n // tm,),
        in_specs=[pl.BlockSpec((tm, d), row),
                  _layer(l, 1, d),
                  pl.BlockSpec((tm, 256), row),
                  pl.BlockSpec((tm, 256), row),
                  pl.BlockSpec((tm, 256), row),
                  pl.BlockSpec((tm, 256), row),
                  pl.BlockSpec((tm, 256), row),
                  _layer(l, N_HEADS, GMLP_CHUNK, GMLP_CHUNK),
                  _layer(l, GMLP_CHUNK, W_MIX),
                  _layer(l, d, 4 * d),
                  _layer(l, 4, W_MIX, d),
                  _layer(l, d, d)],
        out_specs=pl.BlockSpec((tm, d), row),
        out_shape=jax.ShapeDtypeStruct((n, d), F32),
        compiler_params=_cparams("arbitrary"),
        name="merge",
    )(x2, gain, oa, ob, oc, u, vn, w_s, bt, wg, wbr, wout)


def _block_diag_mean(width, group):
    idx = jnp.arange(width) // group
    return ((idx[:, None] == idx[None, :]).astype(F32) / float(group)).astype(BF16)


def _stage_proj_weight(w_in):
    nkv = 1796
    w_in = w_in.astype(BF16)
    lead = w_in.shape[:-1]
    flog = w_in[..., 1536:1540]
    cols = [w_in[..., 0:1536],
            w_in[..., 1540:1796],
            w_in[..., nkv + 128:nkv + 192],
            w_in[..., nkv + 256:nkv + 320],
            w_in[..., nkv + 192:nkv + 256],
            w_in[..., nkv + 320:nkv + 384],
            w_in[..., nkv:nkv + 128],
            w_in[..., 2192:2704],
            flog, flog, flog,
            jnp.zeros(lead + (_GATE_COL - 12,), w_in.dtype),
            w_in[..., 2180:2192],
            jnp.zeros(lead + (128 - _GATE_COL - 12,), w_in.dtype)]
    w = jnp.concatenate(cols, axis=-1)
    assert w.shape[-1] == _C_TOTAL
    return w, w_in[..., 2704:]


def _stage_compress(pe, w1, b1, w2, b2):
    hid = w1.shape[-1]
    half = CMP_STRIDE
    pe_ext = jnp.concatenate([pe[0], pe[1]], axis=-1)
    pe_ext = jnp.stack([pe_ext[:half].reshape(-1), pe_ext[half:].reshape(-1)])
    w1k = w1[0].reshape(CMP_BLOCK, HEAD_DIM, hid)
    w1v = w1[1].reshape(CMP_BLOCK, HEAD_DIM, hid)
    zk = jnp.zeros_like(w1k)
    top = jnp.concatenate([w1k, zk], axis=-1)
    bot = jnp.concatenate([zk, w1v], axis=-1)
    w1e = jnp.concatenate([top, bot], axis=1)
    w1a = w1e[:half].reshape(half * 128, 2 * hid).astype(BF16)
    w1b = w1e[half:].reshape(half * 128, 2 * hid).astype(BF16)
    b1e = jnp.concatenate([b1[0], b1[1]]).reshape(1, 2 * hid)
    z = jnp.zeros((hid, HEAD_DIM), F32)
    w2e = jnp.concatenate([jnp.concatenate([w2[0], z, z, z], axis=1),
                           jnp.concatenate([z, z, w2[1], z], axis=1)], axis=0).astype(BF16)
    zb = jnp.zeros((HEAD_DIM,), F32)
    b2e = jnp.concatenate([b2[0], zb, b2[1], zb]).reshape(1, 256)
    return pe_ext, w1a, w1b, b1e, w2e, b2e


def kernel(x, ffn1_norm, ffn1_w_in, ffn1_w_out, mix_norm, w_in, diff_q_gain, diff_k_gain, diff_lambda, fox_q_gain, fox_k_gain, fox_f_bias, nsa_q_gain, nsa_k_gain, nsa_cmp_pe, nsa_phi_w1, nsa_phi_b1, nsa_phi_w2, nsa_phi_b2, gmlp_v_gain, gmlp_w_s, gmlp_b_s, w_branch, w_out, ffn2_norm, ffn2_w_in, ffn2_w_out):
    batch, seq, d = x.shape
    depth = w_in.shape[0]
    n = batch * seq
    g32 = _block_diag_mean(256, DIFF_QK_DIM)
    g64 = _block_diag_mean(256, HEAD_DIM)
    g64s = g64[:128, :128]
    hot = (jnp.arange(seq)[:, None] // SLC_BLOCK == jnp.arange(MAX_SEL_BLOCKS)[None, :]).astype(BF16)

    norm3 = lambda g: g.reshape(depth, 1, d).astype(F32)
    ffn1_g, ffn2_g, mix_g = norm3(ffn1_norm), norm3(ffn2_norm), norm3(mix_norm)
    ffn1_wi, ffn1_wo = ffn1_w_in.astype(BF16), ffn1_w_out.astype(BF16)
    ffn2_wi, ffn2_wo = ffn2_w_in.astype(BF16), ffn2_w_out.astype(BF16)
    w_cat, w_gate = _stage_proj_weight(w_in)
    tile = lambda g, r: jnp.tile(g.astype(F32), (1, r))
    gains = jnp.stack([tile(diff_q_gain, 8), tile(diff_k_gain, 8), tile(fox_q_gain, 4),
                       tile(fox_k_gain, 4), tile(nsa_q_gain, 4), tile(nsa_k_gain, 4),
                       gmlp_v_gain.astype(F32), jnp.zeros((depth, 256), F32)], axis=1)
    fb = fox_f_bias.astype(F32)
    fbias = jnp.concatenate([fb, fb, fb, jnp.zeros((depth, 116), F32)], axis=1).reshape(depth, 1, 128)
    lam_p = diff_lambda.astype(F32)
    cmp_w = jax.vmap(_stage_compress)(nsa_cmp_pe, nsa_phi_w1, nsa_phi_b1, nsa_phi_w2, nsa_phi_b2)
    kgain = tile(nsa_k_gain, 2).reshape(depth, 1, 128)
    bt = jnp.repeat(jnp.swapaxes(gmlp_b_s, 1, 2), HEAD_DIM, axis=2).astype(F32)
    w_s = gmlp_w_s.astype(F32)
    w_br, w_o = w_branch.astype(BF16), w_out.astype(BF16)
    amax = lambda g: jnp.max(jnp.abs(g.astype(F32)), axis=-1)
    bound = lambda gq, gk, grp: (grp ** 0.5 * LOG2E * BOUND_MARGIN) * amax(gq) * amax(gk)
    b_diff = bound(diff_q_gain, diff_k_gain, DIFF_QK_DIM)
    b_fox = bound(fox_q_gain, fox_k_gain, HEAD_DIM)
    b_nsa = bound(nsa_q_gain, nsa_k_gain, HEAD_DIM)

    x2 = x.reshape(n, d).astype(F32)
    for l in range(depth):
        x2 = _ffn(x2, l, ffn1_g, ffn1_wi, ffn1_wo)
        (aqT, ak, avT, fqT, fk, fvT, faug, cT, nqT, ksw, vswT, kvc, u, vn, ngT) = _proj(
            x2, l, mix_g, w_cat, g32, g64, gains, fbias, batch, seq)

        lam_init = 0.8 - 0.6 * math.exp(-0.3 * l)
        o_a = _attn("diff", aqT, ak, avT, (l, lam_p, g64s), batch, seq, b_diff[l], lam_init)
        o_b = _attn("fox", fqT, fk, fvT, (faug, cT), batch, seq, b_fox[l])
        kc, vcT = _compress(kvc, l, *cmp_w, g64s, kgain, batch, seq)
        o_c = _nsa(nqT, kc, vcT, ksw, vswT, hot, ngT, batch, seq, b_nsa[l])

        x2 = _merge(x2, l, mix_g, o_a, o_b, o_c, u, vn, w_s, bt, w_gate, w_br, w_o)
        x2 = _ffn(x2, l, ffn2_g, ffn2_wi, ffn2_wo)
    return x2.reshape(batch, seq, d).astype(x.dtype)
```

```python
import functools
import math

import jax
import jax.numpy as jnp
from jax import lax
from jax.experimental import pallas as pl
from jax.experimental.pallas import tpu as pltpu

F32 = jnp.float32
BF16 = jnp.bfloat16

HEAD_DIM = 64
N_HEADS = 4
W_MIX = N_HEADS * HEAD_DIM
DIFF_QK_DIM = HEAD_DIM // 2
CMP_BLOCK = 32
CMP_STRIDE = 16
SLC_BLOCK = 64
TOPK = 16
WINDOW = 512
FORCE_SCORE = 1.0e4
GMLP_CHUNK = 128
RMS_EPS = 1e-6
NEG = -1.0e30
LOG2E = math.log2(math.e)

VMEM_LIMIT_BYTES = 56 * 1024 * 1024

FFN_TM = 512
FFN_TF = 256
PROJ_TM = 512
MERGE_TM = 512
ATTN_COLS = 2048
ATTN_CB = 512
ATTN_TK = 1024
SEL_VARIANTS = 4
NSA_TQ = 512
MAX_SEL_BLOCKS = 128
SCORE_BOUND = 40.0
BOUND_MARGIN = 1.02


def _dot(a, b):
    return jnp.dot(a, b, preferred_element_type=F32)


def _split3(c):
    hi = c.astype(BF16)
    r1 = c - hi.astype(F32)
    mid = r1.astype(BF16)
    lo = (r1 - mid.astype(F32)).astype(BF16)
    return hi, mid, lo


def _dot_exact_lhs(a, x, terms):
    return sum(_dot(a, part) for part in _split3(x)[:terms])


def _dot_exact_rhs(x, a, terms):
    return sum(_dot(part, a) for part in _split3(x)[:terms])


def _rms_rows(x):
    ms = jnp.mean(x * x, axis=-1, keepdims=True)
    return x * lax.rsqrt(ms + RMS_EPS)


def _group_norm(z, gmat, gain):
    ms = _dot_exact_rhs(z * z, gmat, 2)
    return z * lax.rsqrt(ms + RMS_EPS) * gain


def _cparams(*sem):
    return pltpu.CompilerParams(dimension_semantics=sem, vmem_limit_bytes=VMEM_LIMIT_BYTES)


def _ffn_body(x_ref, g_ref, wa_ref, wb_ref, wo_ref, o_ref, *, n_chunks, tf):
    x = x_ref[...]
    h = (_rms_rows(x) * g_ref[...]).astype(BF16)
    acc = jnp.zeros_like(x)
    for c in range(n_chunks):
        a = _dot(h, wa_ref[:, c * tf:(c + 1) * tf])
        b = _dot(h, wb_ref[:, c * tf:(c + 1) * tf])
        g = (a * jax.nn.sigmoid(a) * b).astype(BF16)
        acc = acc + _dot(g, wo_ref[c * tf:(c + 1) * tf, :])
    o_ref[...] = x + 0.5 * acc


def _layer(l, *block):
    return pl.BlockSpec((None,) + block, lambda *_: (l,) + (0,) * len(block))


def _ffn(x2, l, gain, w_in, w_out):
    n, d = x2.shape
    d_ff = w_out.shape[1]
    tm, tf = FFN_TM, FFN_TF
    assert n % tm == 0 and d_ff % tf == 0
    return pl.pallas_call(
        functools.partial(_ffn_body, n_chunks=d_ff // tf, tf=tf),
        grid=(n // tm,),
        in_specs=[pl.BlockSpec((tm, d), lambda i: (i, 0)),
                  _layer(l, 1, d),
                  pl.BlockSpec((None, d, d_ff), lambda i: (l, 0, 0)),
                  pl.BlockSpec((None, d, d_ff), lambda i: (l, 0, 1)),
                  _layer(l, d_ff, d)],
        out_specs=pl.BlockSpec((tm, d), lambda i: (i, 0)),
        out_shape=jax.ShapeDtypeStruct((n, d), F32),
        compiler_params=_cparams("arbitrary"),
        name="ffn",
    )(x2, gain, w_in, w_in, w_out)


_C_AQ, _C_AK, _C_AV = 0, 256, 512
_C_FQ, _C_FK, _C_FV = 768, 1024, 1280
_C_NQ = 1536
_C_KSW = 1792
_C_VSW = 1920
_C_KVC = 2048
_C_UV = 2176
_C_SMALL = 2688
_C_TOTAL = 2816
_GATE_COL = 16
_AUG_HI, _AUG_MID, _AUG_LO, _AUG_ONE = 0, 4, 8, 12


def _proj_body(x_ref, g_ref, w_ref, g32_ref, g64_ref, gains_ref, fbias_ref,
               aqT_ref, ak_ref, avT_ref, fqT_ref, fk_ref, fvT_ref, faug_ref, cT_ref,
               nqT_ref, ksw_ref, vswT_ref, kvc_ref, u_ref, vn_ref, ngT_ref,
               carry_ref, *, tiles_per_seq):
    i = pl.program_id(0)
    x = x_ref[...]
    tm = x.shape[0]
    h = (_rms_rows(x) * g_ref[...]).astype(BF16)
    z = _dot(h, w_ref[...])
    g32 = g32_ref[...]
    g64 = g64_ref[...]
    gains = gains_ref[...]

    def sec(c0, w=256):
        return z[:, c0:c0 + w]

    aq = _group_norm(sec(_C_AQ), g32, gains[0:1]) * (DIFF_QK_DIM ** -0.5 * LOG2E)
    aqT_ref[0] = aq.T.astype(BF16)
    ak_ref[...] = _group_norm(sec(_C_AK), g32, gains[1:2]).astype(BF16)
    avT_ref[0] = sec(_C_AV).T.astype(BF16)
    fq = _group_norm(sec(_C_FQ), g64, gains[2:3]) * (HEAD_DIM ** -0.5 * LOG2E)
    fqT_ref[0] = fq.T.astype(BF16)
    fk_ref[...] = _group_norm(sec(_C_FK), g64, gains[3:4]).astype(BF16)
    fvT_ref[0] = sec(_C_FV).T.astype(BF16)
    nq = _group_norm(sec(_C_NQ), g64, gains[4:5]) * (HEAD_DIM ** -0.5 * LOG2E)
    nqT_ref[0] = nq.T.astype(BF16)
    ksw_ref[...] = _group_norm(sec(_C_KSW, 128), g64[:128, :128], gains[5:6, :128]).astype(BF16)
    vswT_ref[0] = sec(_C_VSW, 128).T.astype(BF16)
    kvc_ref[...] = sec(_C_KVC, 128)
    guv = jax.nn.gelu(sec(_C_UV, 512))
    u_ref[...] = guv[:, :256]
    vn_ref[...] = _group_norm(guv[:, 256:], g64, gains[6:7]).astype(BF16)
    small = sec(_C_SMALL, 128)
    ngT_ref[0] = jax.nn.sigmoid(small).T[_GATE_COL:_GATE_COL + 16, :]
    t = small + fbias_ref[...]
    logf = jnp.minimum(t, 0.0) - jnp.log(1.0 + jnp.exp(-jnp.abs(t)))

    @pl.when(i % tiles_per_seq == 0)
    def _():
        carry_ref[...] = jnp.zeros_like(carry_ref)

    row = lax.broadcasted_iota(jnp.int32, (tm, tm), 0)
    col = lax.broadcasted_iota(jnp.int32, (tm, tm), 1)
    tril = jnp.where(row >= col, 1.0, 0.0).astype(BF16)
    csum = _dot_exact_lhs(tril, logf, 3) + carry_ref[0:1, :]
    carry_ref[...] = jnp.broadcast_to(csum[tm - 1:tm, :], carry_ref.shape)
    c2 = csum * LOG2E
    cT_ref[0] = c2.T[0:8, :]
    hi, mid, lo = _split3(c2)
    lane = lax.broadcasted_iota(jnp.int32, (tm, 128), 1)
    one = jnp.where(lane < _AUG_ONE + 3, 1.0, 0.0).astype(BF16)
    faug_ref[...] = jnp.where(lane < _AUG_MID, hi,
                              jnp.where(lane < _AUG_LO, mid, jnp.where(lane < _AUG_ONE, lo, one)))


def _proj(x2, l, gain, w_cat, g32, g64, gains, fbias, batch, seq):
    n, d = x2.shape
    tm = PROJ_TM
    assert seq % tm == 0
    tps = seq // tm
    nt = n // tm
    const = lambda i: (0, 0)
    row = lambda i: (i, 0)
    trn = lambda i: (i // tps, 0, i % tps)
    nat = lambda w, dt: (jax.ShapeDtypeStruct((n, w), dt), pl.BlockSpec((tm, w), row))
    tr = lambda r, dt: (jax.ShapeDtypeStruct((batch, r, seq), dt), pl.BlockSpec((1, r, tm), trn))
    outs = [
        tr(256, BF16), nat(256, BF16), tr(256, BF16),
        tr(256, BF16), nat(256, BF16), tr(256, BF16),
        nat(128, BF16), tr(8, F32),
        tr(256, BF16), nat(128, BF16), tr(128, BF16),
        nat(128, F32), nat(256, F32), nat(256, BF16), tr(16, F32),
    ]
    return pl.pallas_call(
        functools.partial(_proj_body, tiles_per_seq=tps),
        grid=(nt,),
        in_specs=[pl.BlockSpec((tm, d), row),
                  _layer(l, 1, d),
                  _layer(l, d, _C_TOTAL),
                  pl.BlockSpec((256, 256), const),
                  pl.BlockSpec((256, 256), const),
                  _layer(l, 8, 256),
                  _layer(l, 1, 128)],
        out_specs=[o[1] for o in outs],
        out_shape=[o[0] for o in outs],
        scratch_shapes=[pltpu.VMEM((8, 128), F32)],
        compiler_params=_cparams("arbitrary"),
        name="proj",
    )(x2, gain, w_cat, g32, g64, gains, fbias)


def _flash_causal(key_tile, val_tile, qs_ref, m_ref, l_ref, acc_ref, s_ref, *, q0, tq, tk, cb,
                  bounded):
    cols = qs_ref.shape[1]
    nb = cols // cb
    n_full = q0 // tk
    m_ref[...] = jnp.full_like(m_ref, NEG)
    l_ref[...] = jnp.zeros_like(l_ref)
    acc_ref[...] = jnp.zeros_like(acc_ref)
    krow = lax.broadcasted_iota(jnp.int32, (tk, cb), 0)
    kcol = lax.broadcasted_iota(jnp.int32, (tk, cb), 1)
    row_minus_q = [krow - ((j * cb + kcol) & (tq - 1)) for j in range(nb)]
    no_mask = jnp.int32(2 ** 30)

    def scores(kk, j, thr):
        s = _dot(kk, qs_ref[:, j * cb:(j + 1) * cb])
        if thr is not None:
            s = jnp.where(row_minus_q[j][0:s.shape[0]] <= thr, s, NEG)
        return s

    def carried(kk, j, thr):
        s = scores(kk, j, thr)
        if not bounded:
            return s
        p = jnp.exp2(s)
        l_ref[:, j * cb:(j + 1) * cb] += jnp.sum(p, axis=0, keepdims=True)
        return p.astype(BF16)

    s_ref[...] = carried(key_tile(0, tk), 0, jnp.where(n_full == 0, q0, no_mask))

    def step(ki, masked, next_masked, final=False, rows=tk):
        k0 = pl.multiple_of(ki * tk, tk)
        kk = key_tile(k0, rows)
        s = s_ref[0:rows, :]
        for j in range(nb):
            cs = slice(j * cb, (j + 1) * cb)
            s_next = None
            if j + 1 < nb:
                s_next = carried(kk, j + 1, q0 - k0 if masked else None)
            elif not final:
                k1 = pl.multiple_of(k0 + tk, tk)
                s_ref[...] = carried(key_tile(k1, tk), 0, q0 - k1 if next_masked else None)
            if bounded:
                acc_ref[:, cs] += _dot(val_tile(k0, j, rows), s)
            else:
                m_old = m_ref[:, cs]
                m_new = jnp.maximum(m_old, jnp.max(s, axis=0, keepdims=True))
                alpha = jnp.exp2(m_old - m_new)
                p = jnp.exp2(s - m_new)
                l_ref[:, cs] = alpha * l_ref[:, cs] + jnp.sum(p, axis=0, keepdims=True)
                acc_ref[:, cs] = (alpha * acc_ref[:, cs]
                                  + _dot(val_tile(k0, j, rows), p.astype(BF16)))
                m_ref[:, cs] = m_new
            s = s_next

    def loop_body(ki, carry):
        step(ki, False, False)
        return carry

    lax.fori_loop(0, n_full - 1, loop_body, 0)

    @pl.when(n_full > 0)
    def _():
        step(n_full - 1, False, True)

    if tq >= tk:
        n_diag = tq // tk
        for dt in range(n_diag):
            step(n_full + dt, True, True, final=dt == n_diag - 1)
    else:
        for rows in range(tq, tk + 1, tq):
            @pl.when(q0 - n_full * tk + tq == rows)
            def _(rows=rows):
                step(n_full, True, True, final=True, rows=rows)
    return acc_ref[...] / jnp.maximum(l_ref[...], 1e-30)


def _attn_body(*refs, mode, tq, tk, cb, lam_init, bounded):
    if mode == "fox":
        qT_ref, k_ref, vT_ref, aug_ref, cT_ref, o_ref, qs_ref, m_ref, l_ref, acc_ref, s_ref = refs
    else:
        qT_ref, k_ref, vT_ref, lam_ref, g64_ref, o_ref, qs_ref, m_ref, l_ref, acc_ref, s_ref = refs
    nv = 2 if mode == "fox" else 4
    wv = 128 // nv
    hp = pl.program_id(1)
    qi = pl.program_id(2)
    q0 = qi * tq

    qT = qT_ref[0]
    rowi = lax.broadcasted_iota(jnp.int32, (128, tq), 0)
    zero = jnp.zeros_like(qT)
    for vh in range(nv):
        keep = (rowi >= vh * wv) & (rowi < (vh + 1) * wv)
        qs_ref[0:128, vh * tq:(vh + 1) * tq] = jnp.where(keep, qT, zero)
        if mode == "fox":
            hd = 2 * hp + vh
            hi, mid, lo = _split3(cT_ref[0, pl.ds(hd, 1), :])
            neg = (rowi == _AUG_HI + hd) | (rowi == _AUG_MID + hd) | (rowi == _AUG_LO + hd)
            aug = jnp.where(neg, -1.0, 0.0).astype(BF16)
            aug = jnp.where(rowi == _AUG_ONE, hi, aug)
            aug = jnp.where(rowi == _AUG_ONE + 1, mid, aug)
            aug = jnp.where(rowi == _AUG_ONE + 2, lo, aug)
            qs_ref[128:256, vh * tq:(vh + 1) * tq] = aug

    def key_tile(k0, rows):
        kk = k_ref[0, pl.ds(k0, rows), :]
        if mode == "fox":
            kk = jnp.concatenate([kk, aug_ref[0, pl.ds(k0, rows), :]], axis=1)
        return kk

    def val_tile(k0, j, rows):
        hd = ((j * cb) // tq) // (nv // 2)
        return vT_ref[0, hd * HEAD_DIM:(hd + 1) * HEAD_DIM, pl.ds(k0, rows)]

    o = _flash_causal(key_tile, val_tile, qs_ref, m_ref, l_ref, acc_ref, s_ref,
                      q0=q0, tq=tq, tk=tk, cb=cb, bounded=bounded)
    if mode == "fox":
        out_t = jnp.concatenate([o[:, 0:tq], o[:, tq:2 * tq]], axis=0)
    else:
        lp = lam_ref[...]
        lam = (jnp.exp(jnp.sum(lp[0:1] * lp[1:2], axis=-1, keepdims=True))
               - jnp.exp(jnp.sum(lp[2:3] * lp[3:4], axis=-1, keepdims=True)) + lam_init)
        d0 = o[:, 0:tq] - lam * o[:, tq:2 * tq]
        d1 = o[:, 2 * tq:3 * tq] - lam * o[:, 3 * tq:4 * tq]
        dd = jnp.concatenate([d0, d1], axis=0)
        ms = _dot_exact_lhs(g64_ref[...], dd * dd, 2)
        out_t = dd * lax.rsqrt(ms + RMS_EPS) * (1.0 - lam_init)
    o_ref[0] = out_t.T.astype(o_ref.dtype)


def _attn(mode, qT, k, vT, extra, batch, seq, score_bound, lam_init=0.0):
    nv = 2 if mode == "fox" else 4
    tq = ATTN_COLS // nv
    tk, cb = ATTN_TK, ATTN_CB
    assert (tk % tq == 0 or tq % tk == 0) and seq % tk == 0 and (tq % cb == 0 or cb % tq == 0)
    assert (tq & (tq - 1)) == 0
    k3 = k.reshape(batch, seq, 256)
    in_specs = [pl.BlockSpec((1, 128, tq), lambda b, hp, qi: (b, hp, qi)),
                pl.BlockSpec((1, seq, 128), lambda b, hp, qi: (b, 0, hp)),
                pl.BlockSpec((1, 128, seq), lambda b, hp, qi: (b, hp, 0))]
    if mode == "fox":
        faug, cT = extra
        args = (qT, k3, vT, faug.reshape(batch, seq, 128), cT)
        in_specs += [pl.BlockSpec((1, seq, 128), lambda b, hp, qi: (b, 0, 0)),
                     pl.BlockSpec((1, 8, tq), lambda b, hp, qi: (b, 0, qi))]
        depth = 256
    else:
        l, lam_p, g64s = extra
        args = (qT, k3, vT, lam_p, g64s)
        in_specs += [_layer(l, 4, DIFF_QK_DIM),
                     pl.BlockSpec((128, 128), lambda b, hp, qi: (0, 0))]
        depth = 128
    cols = nv * tq
    def call(bounded):
        return pl.pallas_call(
            functools.partial(_attn_body, mode=mode, tq=tq, tk=tk, cb=cb, lam_init=lam_init,
                              bounded=bounded),
            grid=(batch, 2, seq // tq),
            in_specs=in_specs,
            out_specs=pl.BlockSpec((1, tq, 128), lambda b, hp, qi: (b, qi, hp)),
            out_shape=jax.ShapeDtypeStruct((batch, seq, 256), BF16),
            scratch_shapes=[pltpu.VMEM((depth, cols), BF16),
                            pltpu.VMEM((1, cols), F32),
                            pltpu.VMEM((1, cols), F32),
                            pltpu.VMEM((HEAD_DIM, cols), F32),
                            pltpu.VMEM((tk, cb), BF16 if bounded else F32)],
            compiler_params=_cparams("arbitrary", "arbitrary", "arbitrary"),
            name="attn_" + mode + ("_bounded" if bounded else ""),
        )(*args)

    out = lax.cond(score_bound <= SCORE_BOUND, lambda: call(True), lambda: call(False))
    return out.reshape(batch * seq, 256)


def _compress_body(x_ref, pe_ref, w1a_ref, w1b_ref, b1_ref, w2_ref, b2_ref, g64_ref, kg_ref,
                   kc_ref, vcT_ref):
    x = x_ref[0]
    nblk = x.shape[0]
    a = _dot((x + pe_ref[0:1, :]).astype(BF16), w1a_ref[...])
    b = _dot((x + pe_ref[1:2, :]).astype(BF16), w1b_ref[...])
    pre = a + pltpu.roll(b, nblk - 1, 0) + b1_ref[...]
    hid = jax.nn.gelu(pre).astype(BF16)
    out = _dot(hid, w2_ref[...]) + b2_ref[...]
    kc_ref[0] = _group_norm(out[:, 0:128], g64_ref[...], kg_ref[...]).astype(BF16)
    vcT_ref[0] = out[:, 128:256].T[0:HEAD_DIM, :].astype(BF16)


def _compress(kvc, l, pe_ext, w1a, w1b, b1, w2, b2, g64s, kgain, batch, seq):
    nblk = seq // CMP_STRIDE
    x = kvc.reshape(batch, nblk, CMP_STRIDE * 128)
    const = lambda b: (0, 0)
    return pl.pallas_call(
        _compress_body,
        grid=(batch,),
        in_specs=[pl.BlockSpec((1, nblk, CMP_STRIDE * 128), lambda b: (b, 0, 0)),
                  _layer(l, 2, CMP_STRIDE * 128),
                  _layer(l, *w1a.shape[1:]),
                  _layer(l, *w1b.shape[1:]),
                  _layer(l, *b1.shape[1:]),
                  _layer(l, *w2.shape[1:]),
                  _layer(l, *b2.shape[1:]),
                  pl.BlockSpec((128, 128), const),
                  _layer(l, 1, 128)],
        out_specs=[pl.BlockSpec((1, nblk, 128), lambda b: (b, 0, 0)),
                   pl.BlockSpec((1, HEAD_DIM, nblk), lambda b: (b, 0, 0))],
        out_shape=[jax.ShapeDtypeStruct((batch, nblk, 128), BF16),
                   jax.ShapeDtypeStruct((batch, HEAD_DIM, nblk), BF16)],
        compiler_params=_cparams("arbitrary"),
        name="nsa_compress",
    )(x, pe_ext, w1a, w1b, b1, w2, b2, g64s, kgain)


def _topk_select(imp, k):
    nsel, tq = imp.shape

    def count(mask):
        return jnp.sum(jnp.where(mask, 1.0, 0.0), axis=0, keepdims=True)

    kf = float(k)
    bits = jnp.zeros((1, tq), jnp.int32)
    for bit in range(30, -1, -1):
        cand = bits + (1 << bit)
        if bit < 23:
            cand = jnp.where(bits != 0, cand, 0x7F800000)
        ok = count(imp >= lax.bitcast_convert_type(cand, F32)) >= kf
        bits = jnp.where(ok, cand, bits)
    kth = jnp.where(count(imp >= 0.0) >= kf, lax.bitcast_convert_type(bits, F32), -1.0)
    gt = imp > kth
    eq = imp == kth
    need = kf - count(gt)
    r = lax.broadcasted_iota(jnp.int32, (nsel, nsel), 0)
    c = lax.broadcasted_iota(jnp.int32, (nsel, nsel), 1)
    tril = jnp.where(r >= c, 1.0, 0.0).astype(BF16)
    prefix_eq = _dot(tril, jnp.where(eq, 1.0, 0.0).astype(BF16))
    return gt | (eq & (prefix_eq <= need))


def _nsa_body(nqT_ref, kc_ref, vcT_ref, ksw_ref, vswT_ref, hot_ref, ngT_ref, o_ref,
              qs_ref, qw_ref, m_ref, l_ref, acc_ref, ocmp_ref, s_ref, *, tq, tk, cb, seq, bounded):
    qi = pl.program_id(1)
    q0 = qi * tq
    cols = N_HEADS * tq
    nblk = kc_ref.shape[1]
    nsel = seq // SLC_BLOCK
    nb = cols // cb

    zero = jnp.zeros((HEAD_DIM, tq), BF16)
    for hd in range(N_HEADS):
        qh = nqT_ref[0, hd * HEAD_DIM:(hd + 1) * HEAD_DIM, :]
        cs = slice(hd * tq, (hd + 1) * tq)
        qs_ref[0:HEAD_DIM, cs] = qh
        qs_ref[HEAD_DIM:128, cs] = zero
        qw_ref[0:HEAD_DIM, cs] = zero
        qw_ref[HEAD_DIM:128, cs] = qh

    def qoff(shape):
        return lax.broadcasted_iota(jnp.int32, shape, 1) & (tq - 1)

    def select_blocks(n_c, n_s):
        kc = kc_ref[0, 0:n_c, :]
        vcT = vcT_ref[0, :, 0:n_c]
        cend_minus_q = (lax.broadcasted_iota(jnp.int32, (n_c, cb), 0) * CMP_STRIDE
                        + (CMP_BLOCK - 1) - qoff((n_c, cb)))
        p_sum = None
        for j in range(nb):
            s = _dot(kc, qs_ref[0:128, j * cb:(j + 1) * cb])
            s = jnp.where(cend_minus_q <= q0, s, -jnp.inf)
            if bounded:
                e = jnp.exp2(s)
            else:
                m = jnp.max(s, axis=0, keepdims=True)
                m = jnp.where(m == -jnp.inf, 0.0, m)
                e = jnp.exp2(s - m)
            p = e / jnp.maximum(jnp.sum(e, axis=0, keepdims=True), 1e-30)
            ocmp_ref[:, j * cb:(j + 1) * cb] = _dot(vcT, p.astype(BF16))
            for c0 in range(0, cb, tq):
                ph = p[:, c0:c0 + tq]
                p_sum = ph if p_sum is None else p_sum + ph

        jj = lax.broadcasted_iota(jnp.int32, (n_s, n_c), 0)
        cc = lax.broadcasted_iota(jnp.int32, (n_s, n_c), 1)
        ov_t = jnp.where((cc * CMP_STRIDE < (jj + 1) * SLC_BLOCK)
                         & (cc * CMP_STRIDE + CMP_BLOCK > jj * SLC_BLOCK), 1.0, 0.0).astype(BF16)
        imp = _dot_exact_lhs(ov_t, p_sum, 3)
        sj = lax.broadcasted_iota(jnp.int32, (n_s, tq), 0)
        qpos_l = q0 + lax.broadcasted_iota(jnp.int32, (n_s, tq), 1)
        cur = qpos_l >> 6
        forced = (sj == 0) | (sj == cur) | (sj == cur - 1)
        valid = sj * SLC_BLOCK <= qpos_l
        imp = jnp.where(forced, FORCE_SCORE, jnp.where(valid, imp, -1.0))
        sel = _topk_select(imp, min(TOPK, n_s))
        bias = jnp.where(sel, 0.0, -(2.0 ** 30)).astype(BF16)
        if n_s < MAX_SEL_BLOCKS:
            bias = jnp.concatenate([bias, jnp.zeros((MAX_SEL_BLOCKS - n_s, tq), BF16)], axis=0)
        for hd in range(N_HEADS):
            qs_ref[128:256, hd * tq:(hd + 1) * tq] = bias

    n_tiles = seq // tq
    n_var = SEL_VARIANTS if n_tiles % SEL_VARIANTS == 0 and nsel // SEL_VARIANTS >= TOPK else 1
    for v in range(n_var):
        @pl.when(qi // (n_tiles // n_var) == v)
        def _(v=v):
            select_blocks(nblk * (v + 1) // n_var, nsel * (v + 1) // n_var)

    o_cmp = ocmp_ref[...]

    key_minus_q = lax.broadcasted_iota(jnp.int32, (tq, cb), 0) - qoff((tq, cb))

    def window_half(k0, own):
        kw = ksw_ref[0, pl.ds(k0, tq), :]
        vw = vswT_ref[0, HEAD_DIM:128, pl.ds(k0, tq)]
        keep = key_minus_q <= 0 if own else key_minus_q > 0
        s_next = _dot(kw, qw_ref[:, 0:cb])
        for j in range(nb):
            cs = slice(j * cb, (j + 1) * cb)
            s = jnp.where(keep, s_next, NEG)
            if j + 1 < nb:
                s_next = _dot(kw, qw_ref[:, (j + 1) * cb:(j + 2) * cb])
            if bounded:
                p = jnp.exp2(s)
                lsum, pv = jnp.sum(p, axis=0, keepdims=True), _dot(vw, p.astype(BF16))
                l_ref[:, cs] = lsum if own else l_ref[:, cs] + lsum
                acc_ref[:, cs] = pv if own else acc_ref[:, cs] + pv
            elif own:
                m_new = jnp.max(s, axis=0, keepdims=True)
                p = jnp.exp2(s - m_new)
                l_ref[:, cs] = jnp.sum(p, axis=0, keepdims=True)
                acc_ref[:, cs] = _dot(vw, p.astype(BF16))
                m_ref[:, cs] = m_new
            else:
                m_old = m_ref[:, cs]
                m_new = jnp.maximum(m_old, jnp.max(s, axis=0, keepdims=True))
                alpha = jnp.exp2(m_old - m_new)
                p = jnp.exp2(s - m_new)
                l_ref[:, cs] = alpha * l_ref[:, cs] + jnp.sum(p, axis=0, keepdims=True)
                acc_ref[:, cs] = alpha * acc_ref[:, cs] + _dot(vw, p.astype(BF16))
                m_ref[:, cs] = m_new

    window_half(pl.multiple_of(q0, tq), True)

    @pl.when(qi > 0)
    def _():
        window_half(pl.multiple_of(q0 - WINDOW, tq), False)

    o_win = acc_ref[...] / jnp.maximum(l_ref[...], 1e-30)

    def key_tile(k0, rows):
        return jnp.concatenate([ksw_ref[0, pl.ds(k0, rows), :], hot_ref[pl.ds(k0, rows), :]], axis=1)

    def val_tile(k0, j, rows):
        return vswT_ref[0, 0:HEAD_DIM, pl.ds(k0, rows)]

    o_sel = _flash_causal(key_tile, val_tile, qs_ref, m_ref, l_ref, acc_ref, s_ref,
                          q0=q0, tq=tq, tk=tk, cb=cb, bounded=bounded)

    gates = ngT_ref[0]
    out_t = []
    for hd in range(N_HEADS):
        cs = slice(hd * tq, (hd + 1) * tq)
        out_t.append(gates[3 * hd:3 * hd + 1] * o_cmp[:, cs]
                     + gates[3 * hd + 1:3 * hd + 2] * o_sel[:, cs]
                     + gates[3 * hd + 2:3 * hd + 3] * o_win[:, cs])
    o_ref[0] = jnp.concatenate(out_t, axis=0).T.astype(o_ref.dtype)


def _nsa(nqT, kc, vcT, ksw, vswT, hot, ngT, batch, seq, score_bound):
    tq, tk, cb = NSA_TQ, ATTN_TK, ATTN_CB
    cols = N_HEADS * tq
    nblk = seq // CMP_STRIDE
    assert seq % tk == 0 and (tk % tq == 0 or tq % tk == 0) and cb % tq == 0 and cols % cb == 0
    assert tq == WINDOW and seq % tq == 0 and (tq & (tq - 1)) == 0
    assert seq // SLC_BLOCK <= MAX_SEL_BLOCKS
    full = lambda b, qi: (b, 0, 0)
    args = (nqT, kc, vcT, ksw.reshape(batch, seq, 128), vswT, hot, ngT)

    def call(bounded):
        return pl.pallas_call(
            functools.partial(_nsa_body, tq=tq, tk=tk, cb=cb, seq=seq, bounded=bounded),
            grid=(batch, seq // tq),
            in_specs=[pl.BlockSpec((1, 256, tq), lambda b, qi: (b, 0, qi)),
                      pl.BlockSpec((1, nblk, 128), full),
                      pl.BlockSpec((1, HEAD_DIM, nblk), full),
                      pl.BlockSpec((1, seq, 128), full),
                      pl.BlockSpec((1, 128, seq), full),
                      pl.BlockSpec((seq, MAX_SEL_BLOCKS), lambda b, qi: (0, 0)),
                      pl.BlockSpec((1, 16, tq), lambda b, qi: (b, 0, qi))],
            out_specs=pl.BlockSpec((1, tq, 256), lambda b, qi: (b, qi, 0)),
            out_shape=jax.ShapeDtypeStruct((batch, seq, 256), BF16),
            scratch_shapes=[pltpu.VMEM((256, cols), BF16),
                            pltpu.VMEM((128, cols), BF16),
                            pltpu.VMEM((1, cols), F32),
                            pltpu.VMEM((1, cols), F32),
                            pltpu.VMEM((HEAD_DIM, cols), F32),
                            pltpu.VMEM((HEAD_DIM, cols), F32),
                            pltpu.VMEM((tk, cb), BF16 if bounded else F32)],
            compiler_params=_cparams("arbitrary", "arbitrary"),
            name="nsa" + ("_bounded" if bounded else ""),
        )(*args)

    out = lax.cond(score_bound <= SCORE_BOUND, lambda: call(True), lambda: call(False))
    return out.reshape(batch * seq, 256)


def _merge_body(x_ref, g_ref, oa_ref, ob_ref, oc_ref, u_ref, vn_ref, ws_ref, bt_ref,
                wg_ref, wbr_ref, wout_ref, o_ref):
    x = x_ref[...]
    tm, d = x.shape
    h = (_rms_rows(x) * g_ref[...]).astype(BF16)

    r = lax.broadcasted_iota(jnp.int32, (GMLP_CHUNK, GMLP_CHUNK), 0)
    c = lax.broadcasted_iota(jnp.int32, (GMLP_CHUNK, GMLP_CHUNK), 1)
    lane_grp = lax.broadcasted_iota(jnp.int32, (GMLP_CHUNK, W_MIX), 1) // HEAD_DIM
    w_tril = [jnp.where(r >= c, ws_ref[g], 0.0).astype(BF16) for g in range(N_HEADS)]
    od = []
    for ch in range(tm // GMLP_CHUNK):
        sl = slice(ch * GMLP_CHUNK, (ch + 1) * GMLP_CHUNK)
        vch = vn_ref[sl, :]
        sv = jnp.zeros((GMLP_CHUNK, W_MIX), F32)
        for g in range(N_HEADS):
            sv = jnp.where(lane_grp == g, _dot(w_tril[g], vch), sv)
        od.append(u_ref[sl, :] * (sv + bt_ref[...]))
    o_d = jnp.concatenate(od, axis=0).astype(BF16)

    branches = (oa_ref[...], ob_ref[...], oc_ref[...], o_d)
    acc = jnp.zeros_like(x)
    for n in range(4):
        gate = jax.nn.sigmoid(_dot(h, wg_ref[:, n * d:(n + 1) * d]))
        acc = acc + gate * _dot(branches[n], wbr_ref[n])
    o_ref[...] = x + _dot(acc.astype(BF16), wout_ref[...])


def _merge_ffn_body(x_ref, g_ref, oa_ref, ob_ref, oc_ref, u_ref, vn_ref, ws_ref, bt_ref,
                    wg_ref, wbr_ref, wout_ref, g2_ref, wa_ref, wb_ref, wo_ref, o_ref, mid_ref,
                    *, n_chunks, tf):
    _merge_body(x_ref, g_ref, oa_ref, ob_ref, oc_ref, u_ref, vn_ref, ws_ref, bt_ref,
                wg_ref, wbr_ref, wout_ref, mid_ref)
    _ffn_body(mid_ref, g2_ref, wa_ref, wb_ref, wo_ref, o_ref, n_chunks=n_chunks, tf=tf)


def _merge_ffn(x2, l, gain, oa, ob, oc, u, vn, w_s, bt, wg, wbr, wout, gain2, w_in2, w_out2):
    n, d = x2.shape
    d_ff = w_out2.shape[1]
    tm, tf = MERGE_TM, FFN_TF
    row = lambda i: (i, 0)
    once = pl.Buffered(1)

    def layer1(*block, col=0):
        idx = (l,) + (0,) * (len(block) - 1) + (col,)
        return pl.BlockSpec((None,) + block, lambda i: idx, pipeline_mode=once)

    return pl.pallas_call(
        functools.partial(_merge_ffn_body, n_chunks=d_ff // tf, tf=tf),
        grid=(n // tm,),
        in_specs=[pl.BlockSpec((tm, d), row),
                  _layer(l, 1, d),
                  pl.BlockSpec((tm, 256), row),
                  pl.BlockSpec((tm, 256), row),
                  pl.BlockSpec((tm, 256), row),
                  pl.BlockSpec((tm, 256), row),
                  pl.BlockSpec((tm, 256), row),
                  _layer(l, N_HEADS, GMLP_CHUNK, GMLP_CHUNK),
                  _layer(l, GMLP_CHUNK, W_MIX),
                  layer1(d, 4 * d),
                  layer1(4, W_MIX, d),
                  layer1(d, d),
                  _layer(l, 1, d),
                  layer1(d, d_ff, col=0),
                  layer1(d, d_ff, col=1),
                  layer1(d_ff, d)],
        out_specs=pl.BlockSpec((tm, d), row),
        out_shape=jax.ShapeDtypeStruct((n, d), F32),
        scratch_shapes=[pltpu.VMEM((tm, d), F32)],
        compiler_params=_cparams("arbitrary"),
        name="merge_ffn",
    )(x2, gain, oa, ob, oc, u, vn, w_s, bt, wg, wbr, wout, gain2, w_in2, w_in2, w_out2)


def _block_diag_mean(width, group):
    idx = jnp.arange(width) // group
    return ((idx[:, None] == idx[None, :]).astype(F32) / float(group)).astype(BF16)


def _stage_proj_weight(w_in):
    nkv = 1796
    w_in = w_in.astype(BF16)
    lead = w_in.shape[:-1]
    flog = w_in[..., 1536:1540]
    cols = [w_in[..., 0:1536],
            w_in[..., 1540:1796],
            w_in[..., nkv + 128:nkv + 192],
            w_in[..., nkv + 256:nkv + 320],
            w_in[..., nkv + 192:nkv + 256],
            w_in[..., nkv + 320:nkv + 384],
            w_in[..., nkv:nkv + 128],
            w_in[..., 2192:2704],
            flog, flog, flog,
            jnp.zeros(lead + (_GATE_COL - 12,), w_in.dtype),
            w_in[..., 2180:2192],
            jnp.zeros(lead + (128 - _GATE_COL - 12,), w_in.dtype)]
    w = jnp.concatenate(cols, axis=-1)
    assert w.shape[-1] == _C_TOTAL
    return w, w_in[..., 2704:]


def _stage_compress(pe, w1, b1, w2, b2):
    hid = w1.shape[-1]
    half = CMP_STRIDE
    pe_ext = jnp.concatenate([pe[0], pe[1]], axis=-1)
    pe_ext = jnp.stack([pe_ext[:half].reshape(-1), pe_ext[half:].reshape(-1)])
    w1k = w1[0].reshape(CMP_BLOCK, HEAD_DIM, hid)
    w1v = w1[1].reshape(CMP_BLOCK, HEAD_DIM, hid)
    zk = jnp.zeros_like(w1k)
    top = jnp.concatenate([w1k, zk], axis=-1)
    bot = jnp.concatenate([zk, w1v], axis=-1)
    w1e = jnp.concatenate([top, bot], axis=1)
    w1a = w1e[:half].reshape(half * 128, 2 * hid).astype(BF16)
    w1b = w1e[half:].reshape(half * 128, 2 * hid).astype(BF16)
    b1e = jnp.concatenate([b1[0], b1[1]]).reshape(1, 2 * hid)
    z = jnp.zeros((hid, HEAD_DIM), F32)
    w2e = jnp.concatenate([jnp.concatenate([w2[0], z, z, z], axis=1),
                           jnp.concatenate([z, z, w2[1], z], axis=1)], axis=0).astype(BF16)
    zb = jnp.zeros((HEAD_DIM,), F32)
    b2e = jnp.concatenate([b2[0], zb, b2[1], zb]).reshape(1, 256)
    return pe_ext, w1a, w1b, b1e, w2e, b2e


def kernel(x, ffn1_norm, ffn1_w_in, ffn1_w_out, mix_norm, w_in, diff_q_gain, diff_k_gain, diff_lambda, fox_q_gain, fox_k_gain, fox_f_bias, nsa_q_gain, nsa_k_gain, nsa_cmp_pe, nsa_phi_w1, nsa_phi_b1, nsa_phi_w2, nsa_phi_b2, gmlp_v_gain, gmlp_w_s, gmlp_b_s, w_branch, w_out, ffn2_norm, ffn2_w_in, ffn2_w_out):
    batch, seq, d = x.shape
    depth = w_in.shape[0]
    n = batch * seq
    g32 = _block_diag_mean(256, DIFF_QK_DIM)
    g64 = _block_diag_mean(256, HEAD_DIM)
    g64s = g64[:128, :128]
    hot = (jnp.arange(seq)[:, None] // SLC_BLOCK == jnp.arange(MAX_SEL_BLOCKS)[None, :]).astype(BF16)

    norm3 = lambda g: g.reshape(depth, 1, d).astype(F32)
    ffn1_g, ffn2_g, mix_g = norm3(ffn1_norm), norm3(ffn2_norm), norm3(mix_norm)
    ffn1_wi, ffn1_wo = ffn1_w_in.astype(BF16), ffn1_w_out.astype(BF16)
    ffn2_wi, ffn2_wo = ffn2_w_in.astype(BF16), ffn2_w_out.astype(BF16)
    w_cat, w_gate = _stage_proj_weight(w_in)
    tile = lambda g, r: jnp.tile(g.astype(F32), (1, r))
    gains = jnp.stack([tile(diff_q_gain, 8), tile(diff_k_gain, 8), tile(fox_q_gain, 4),
                       tile(fox_k_gain, 4), tile(nsa_q_gain, 4), tile(nsa_k_gain, 4),
                       gmlp_v_gain.astype(F32), jnp.zeros((depth, 256), F32)], axis=1)
    fb = fox_f_bias.astype(F32)
    fbias = jnp.concatenate([fb, fb, fb, jnp.zeros((depth, 116), F32)], axis=1).reshape(depth, 1, 128)
    lam_p = diff_lambda.astype(F32)
    cmp_w = jax.vmap(_stage_compress)(nsa_cmp_pe, nsa_phi_w1, nsa_phi_b1, nsa_phi_w2, nsa_phi_b2)
    kgain = tile(nsa_k_gain, 2).reshape(depth, 1, 128)
    bt = jnp.repeat(jnp.swapaxes(gmlp_b_s, 1, 2), HEAD_DIM, axis=2).astype(F32)
    w_s = gmlp_w_s.astype(F32)
    w_br, w_o = w_branch.astype(BF16), w_out.astype(BF16)
    amax = lambda g: jnp.max(jnp.abs(g.astype(F32)), axis=-1)
    bound = lambda gq, gk, grp: (grp ** 0.5 * LOG2E * BOUND_MARGIN) * amax(gq) * amax(gk)
    b_diff = bound(diff_q_gain, diff_k_gain, DIFF_QK_DIM)
    b_fox = bound(fox_q_gain, fox_k_gain, HEAD_DIM)
    b_nsa = bound(nsa_q_gain, nsa_k_gain, HEAD_DIM)

    x2 = x.reshape(n, d).astype(F32)
    for l in range(depth):
        x2 = _ffn(x2, l, ffn1_g, ffn1_wi, ffn1_wo)
        (aqT, ak, avT, fqT, fk, fvT, faug, cT, nqT, ksw, vswT, kvc, u, vn, ngT) = _proj(
            x2, l, mix_g, w_cat, g32, g64, gains, fbias, batch, seq)

        lam_init = 0.8 - 0.6 * math.exp(-0.3 * l)
        o_a = _attn("diff", aqT, ak, avT, (l, lam_p, g64s), batch, seq, b_diff[l], lam_init)
        o_b = _attn("fox", fqT, fk, fvT, (faug, cT), batch, seq, b_fox[l])
        kc, vcT = _compress(kvc, l, *cmp_w, g64s, kgain, batch, seq)
        o_c = _nsa(nqT, kc, vcT, ksw, vswT, hot, ngT, batch, seq, b_nsa[l])

        x2 = _merge_ffn(x2, l, mix_g, o_a, o_b, o_c, u, vn, w_s, bt, w_gate, w_br, w_o,
                        ffn2_g, ffn2_wi, ffn2_wo)
    return x2.reshape(batch, seq, d).astype(x.dtype)
```
